```python
import math
import jax, jax.numpy as jnp
from jax import lax
import numpy as np

D_MODEL = 1024
BATCH = 8
SEQ = 2048
DEPTH = 4

HEAD_DIM = 64
D_MIX = D_MODEL
N_MIX_HEADS = D_MIX // HEAD_DIM
C_HEADS = N_MIX_HEADS // 4
A_HEADS = (N_MIX_HEADS - C_HEADS) // 2
B_HEADS = N_MIX_HEADS - C_HEADS - A_HEADS
A_QK_DIM = HEAD_DIM // 2
A_WIDTH = A_HEADS * HEAD_DIM
B_WIDTH = B_HEADS * HEAD_DIM
C_WIDTH = C_HEADS * HEAD_DIM
DILATED_PATTERNS = ((128, 1), (512, 4), (2048, 16))
BLOCK_Q = 128
ROPE_THETA = 10000.0
D_FF = ((8 * D_MODEL // 3 + 255) // 256) * 256
C_W_RANK = 64
C_A_RANK = 64
C_V_RANK = 32
C_G_RANK = 128
DEEPNORM_ALPHA = (2 * DEPTH) ** 0.25
DEEPNORM_BETA = (8 * DEPTH) ** -0.25
LN_EPS = 1e-5
RMS_EPS = 1e-5
C_GN_EPS = 64e-5
NEG_INF = -1e30
A_Q_COLS = 2 * A_HEADS * A_QK_DIM
B_COLS = B_WIDTH
C_COLS = 3 * C_WIDTH + C_W_RANK + C_A_RANK + C_G_RANK
IN_SIZES = (A_Q_COLS, A_Q_COLS, A_WIDTH, B_COLS, B_COLS, B_COLS, C_COLS)
IN_SPLIT_IDX = [int(v) for v in np.cumsum(IN_SIZES)[:-1]]
N_IN = int(sum(IN_SIZES))
C_SPLIT_IDX = [int(v) for v in np.cumsum((C_WIDTH, C_WIDTH, C_WIDTH, C_W_RANK, C_A_RANK))]

kernel_name = "hybrid_diffattn_dilated_rwkv7_macaron_deepnorm"


def layer_norm(x, g, b):
    xf = x.astype(jnp.float32)
    mu = jnp.mean(xf, -1, keepdims=True)
    var = jnp.mean(jnp.square(xf - mu), -1, keepdims=True)
    return ((xf - mu) * lax.rsqrt(var + LN_EPS) * g + b).astype(x.dtype)


def rms_norm(x, g):
    xf = x.astype(jnp.float32)
    return (xf * lax.rsqrt(jnp.mean(jnp.square(xf), -1, keepdims=True) + RMS_EPS) * g).astype(x.dtype)


def swiglu(x, w_gate, w_up, w_down):
    return (jax.nn.silu(x @ w_gate) * (x @ w_up)) @ w_down


def rope(x, positions):
    d = x.shape[-1]
    inv = ROPE_THETA ** (-jnp.arange(0, d, 2, dtype=jnp.float32) / d)
    ang = positions.astype(jnp.float32)[:, None] * inv[None, :]
    ang = ang.reshape((ang.shape[0],) + (1,) * (x.ndim - 3) + (d // 2,))
    cos, sin = jnp.cos(ang).astype(x.dtype), jnp.sin(ang).astype(x.dtype)
    x1, x2 = x[..., : d // 2], x[..., d // 2:]
    return jnp.concatenate([x1 * cos - x2 * sin, x2 * cos + x1 * sin], axis=-1)


def token_shift(x):
    return jnp.pad(x[:, :-1], ((0, 0), (1, 0), (0, 0)))


def diff_attention(q, k, v, lam):
    bsz, s_len, h, _, dq = q.shape
    nb = s_len // BLOCK_Q
    scale = dq ** -0.5
    qb = jnp.moveaxis(q.reshape(bsz, nb, BLOCK_Q, h, 2, dq), 1, 0)
    vf = v.astype(jnp.float32)
    key_pos = jnp.arange(s_len)

    def one_block(args):
        q_blk, n = args
        q_pos = n * BLOCK_Q + jnp.arange(BLOCK_Q)
        s = jnp.einsum('bqhmd,bkhmd->bhmqk', q_blk, k).astype(jnp.float32) * scale
        causal = key_pos[None, :] <= q_pos[:, None]
        p = jax.nn.softmax(jnp.where(causal, s, NEG_INF), axis=-1)
        attn = p[:, :, 0] - lam * p[:, :, 1]
        return jnp.einsum('bhqk,bkhd->bqhd', attn, vf)

    out = lax.map(one_block, (qb, jnp.arange(nb)))
    return jnp.moveaxis(out, 0, 1).reshape(bsz, s_len, h, v.shape[-1])


def dilated_branch(q, k, v, window, dilation):
    bsz, s_len, h, d = q.shape
    span = window // dilation
    group = dilation * span
    s_pad = -(-s_len // group) * group
    nb = s_pad // group
    pad = ((0, 0), (0, s_pad - s_len), (0, 0), (0, 0))

    def strided_blocks(t):
        return jnp.pad(t, pad).reshape(bsz, nb, span, dilation, h, d)

    def with_prev(t):
        prev = jnp.pad(t[:, :-1], ((0, 0), (1, 0), (0, 0), (0, 0), (0, 0), (0, 0)))
        return jnp.concatenate([prev, t], axis=2)

    qb = strided_blocks(q)
    kc = with_prev(strided_blocks(k))
    vc = with_prev(strided_blocks(v)).astype(jnp.float32)
    s = jnp.einsum('bnqrhd,bnkrhd->bnrhqk', qb, kc).astype(jnp.float32) * (d ** -0.5)
    i = jnp.arange(span)[:, None]
    c = jnp.arange(2 * span)[None, :]
    n = jnp.arange(nb)[:, None, None]
    valid = (c >= i) & (c <= i + span) & ((n > 0) | (c >= span))
    s = jnp.where(valid[None, :, None, None], s, NEG_INF)
    m = jnp.max(s, axis=-1, keepdims=True)
    e = jnp.exp(s - m)
    denom = jnp.sum(e, axis=-1)
    o = jnp.einsum('bnrhqk,bnkrhd->bnqrhd', e, vc)
    o = o / jnp.moveaxis(denom, 4, 2)[..., None]
    lse = jnp.moveaxis(m[..., 0] + jnp.log(denom), 4, 2)
    o = o.reshape(bsz, s_pad, h, d)[:, :s_len]
    lse = lse.reshape(bsz, s_pad, h)[:, :s_len]
    return o, lse


def dilated_attention(q, k, v):
    outs, lses = [], []
    for window, dilation in DILATED_PATTERNS:
        o, lse = dilated_branch(q, k, v, window, dilation)
        outs.append(o)
        lses.append(lse)
    wts = jax.nn.softmax(jnp.stack(lses), axis=0)
    return jnp.sum(wts[..., None] * jnp.stack(outs), axis=0).astype(q.dtype)


def rwkv7_scan(r, decay, k, v, kk, a):
    bsz, _, h, d = r.shape

    def step(state, inp):
        r_t, w_t, k_t, v_t, kk_t, a_t = inp
        sa = jnp.einsum('bhvk,bhk->bhv', state, -kk_t)
        state = (state * w_t[:, :, None, :] + sa[..., None] * (kk_t * a_t)[:, :, None, :]
                 + v_t[..., None] * k_t[:, :, None, :])
        return state, jnp.einsum('bhvk,bhk->bhv', state, r_t)

    xs = tuple(jnp.moveaxis(t, 1, 0) for t in (r, decay, k, v, kk, a))
    _, ys = lax.scan(step, jnp.zeros((bsz, h, d, d), jnp.float32), xs)
    return jnp.moveaxis(ys, 0, 1)


def rwkv7_mixer(c, c_mu, w0, w2, a0, a2, g2, k_k, k_a, r_k, gn_g, gn_b, v_first, v_res):
    bsz, s_len, _ = c.shape
    c = c + (token_shift(c) - c) * c_mu
    r, k, v, xw, xa, xg = jnp.split(c, C_SPLIT_IDX, axis=-1)
    if v_first is None:
        v_first = v
    else:
        v0, v1, v2 = v_res
        v = v + (v_first - v) * jax.nn.sigmoid(v0 + (v @ v1) @ v2)
    f32 = lambda t: t.astype(jnp.float32)
    decay = jnp.exp(-math.exp(-0.5) * jax.nn.sigmoid(f32(w0 + jnp.tanh(xw) @ w2)))
    a = jax.nn.sigmoid(f32(a0 + xa @ a2))
    g = jax.nn.sigmoid(xg) @ g2
    heads = lambda t: t.reshape(bsz, s_len, C_HEADS, HEAD_DIM)
    rh, kh, vh, ah, dh = heads(f32(r)), heads(f32(k)), heads(f32(v)), heads(a), heads(decay)
    kk = kh * k_k.reshape(C_HEADS, HEAD_DIM)
    kk = kk / jnp.maximum(jnp.sqrt(jnp.sum(jnp.square(kk), -1, keepdims=True)), 1e-12)
    kh = kh * (1.0 + (ah - 1.0) * k_a.reshape(C_HEADS, HEAD_DIM))
    y = rwkv7_scan(rh, dh, kh, vh, kk, ah)
    mu = jnp.mean(y, -1, keepdims=True)
    var = jnp.mean(jnp.square(y - mu), -1, keepdims=True)
    y = ((y - mu) * lax.rsqrt(var + C_GN_EPS) * gn_g.reshape(C_HEADS, HEAD_DIM)
         + gn_b.reshape(C_HEADS, HEAD_DIM))
    y = y + jnp.sum(rh * kh * r_k, -1, keepdims=True) * vh
    y = y.reshape(bsz, s_len, C_WIDTH) * g
    return y.astype(c.dtype), v_first


def setup_inputs(seed: int = 0) -> dict:
    key = jax.random.key(seed)
    ks = iter(jax.random.split(key, 40))
    nrm = lambda shape, scale: jax.random.normal(next(ks), shape, jnp.float32) * scale
    L = DEPTH
    return {
        "x": nrm((BATCH, SEQ, D_MODEL), 1.0),
        "ffn_a_gate": nrm((L, D_MODEL, D_FF), D_MODEL ** -0.5),
        "ffn_a_up": nrm((L, D_MODEL, D_FF), D_MODEL ** -0.5),
        "ffn_a_down": nrm((L, D_FF, D_MODEL), D_FF ** -0.5 * DEEPNORM_BETA),
        "ffn_b_gate": nrm((L, D_MODEL, D_FF), D_MODEL ** -0.5),
        "ffn_b_up": nrm((L, D_MODEL, D_FF), D_MODEL ** -0.5),
        "ffn_b_down": nrm((L, D_FF, D_MODEL), D_FF ** -0.5 * DEEPNORM_BETA),
        "ln_g": 1.0 + nrm((L, 3, D_MODEL), 0.02),
        "ln_b": nrm((L, 3, D_MODEL), 0.02),
        "w_in": nrm((L, D_MODEL, N_IN), D_MODEL ** -0.5),
        "w_out": nrm((L, D_MIX, D_MODEL), D_MIX ** -0.5 * DEEPNORM_BETA),
        "a_lam_q1": nrm((L, A_QK_DIM), 0.1),
        "a_lam_k1": nrm((L, A_QK_DIM), 0.1),
        "a_lam_q2": nrm((L, A_QK_DIM), 0.1),
        "a_lam_k2": nrm((L, A_QK_DIM), 0.1),
        "a_norm_g": 1.0 + nrm((L, HEAD_DIM), 0.02),
        "b_norm_g": 1.0 + nrm((L, HEAD_DIM), 0.02),
        "c_mu": jax.random.uniform(next(ks), (L, C_COLS), jnp.float32),
        "c_w0": nrm((L, C_WIDTH), 0.5),
        "c_w2": nrm((L, C_W_RANK, C_WIDTH), C_W_RANK ** -0.5),
        "c_a0": nrm((L, C_WIDTH), 0.5),
        "c_a2": nrm((L, C_A_RANK, C_WIDTH), C_A_RANK ** -0.5),
        "c_g2": nrm((L, C_G_RANK, C_WIDTH), C_G_RANK ** -0.5),
        "c_k_k": 0.85 + nrm((L, C_WIDTH), 0.05),
        "c_k_a": 1.0 + nrm((L, C_WIDTH), 0.05),
        "c_r_k": nrm((L, C_HEADS, HEAD_DIM), 0.1),
        "c_gn_g": 1.0 + nrm((L, C_WIDTH), 0.02),
        "c_gn_b": nrm((L, C_WIDTH), 0.02),
        "c_v0": nrm((L - 1, C_WIDTH), 0.5),
        "c_v1": nrm((L - 1, C_WIDTH, C_V_RANK), C_WIDTH ** -0.5),
        "c_v2": nrm((L - 1, C_V_RANK, C_WIDTH), C_V_RANK ** -0.5),
    }


def reference(x, ffn_a_gate, ffn_a_up, ffn_a_down, ffn_b_gate, ffn_b_up, ffn_b_down,
              ln_g, ln_b, w_in, w_out, a_lam_q1, a_lam_k1, a_lam_q2, a_lam_k2,
              a_norm_g, b_norm_g, c_mu, c_w0, c_w2, c_a0, c_a2, c_g2, c_k_k, c_k_a,
              c_r_k, c_gn_g, c_gn_b, c_v0, c_v1, c_v2):
    bsz, s_len, _ = x.shape
    positions = jnp.arange(s_len)
    v_first = None
    for l in range(DEPTH):
        x = layer_norm(DEEPNORM_ALPHA * x + 0.5 * swiglu(x, ffn_a_gate[l], ffn_a_up[l], ffn_a_down[l]),
                       ln_g[l, 0], ln_b[l, 0])

        aq, ak, av, bq, bk, bv, cc = jnp.split(x @ w_in[l], IN_SPLIT_IDX, axis=-1)

        lambda_init = 0.8 - 0.6 * math.exp(-0.3 * l)
        lam = (jnp.exp(jnp.sum(a_lam_q1[l] * a_lam_k1[l]).astype(jnp.float32))
               - jnp.exp(jnp.sum(a_lam_q2[l] * a_lam_k2[l]).astype(jnp.float32)) + lambda_init)
        aq = rope(aq.reshape(bsz, s_len, A_HEADS, 2, A_QK_DIM), positions)
        ak = rope(ak.reshape(bsz, s_len, A_HEADS, 2, A_QK_DIM), positions)
        o_a = diff_attention(aq, ak, av.reshape(bsz, s_len, A_HEADS, HEAD_DIM), lam)
        o_a = (rms_norm(o_a, a_norm_g[l]) * (1.0 - lambda_init)).astype(x.dtype)
        o_a = o_a.reshape(bsz, s_len, A_WIDTH)

        bq = rope(bq.reshape(bsz, s_len, B_HEADS, HEAD_DIM), positions)
        bk = rope(bk.reshape(bsz, s_len, B_HEADS, HEAD_DIM), positions)
        o_b = dilated_attention(bq, bk, bv.reshape(bsz, s_len, B_HEADS, HEAD_DIM))
        o_b = rms_norm(o_b, b_norm_g[l]).reshape(bsz, s_len, B_WIDTH)

        v_res = None if l == 0 else (c_v0[l - 1], c_v1[l - 1], c_v2[l - 1])
        o_c, v_first = rwkv7_mixer(cc, c_mu[l], c_w0[l], c_w2[l], c_a0[l], c_a2[l], c_g2[l],
                                   c_k_k[l], c_k_a[l], c_r_k[l], c_gn_g[l], c_gn_b[l],
                                   v_first, v_res)

        mix = jnp.concatenate([o_a, o_b, o_c], axis=-1) @ w_out[l]
        x = layer_norm(DEEPNORM_ALPHA * x + mix, ln_g[l, 1], ln_b[l, 1])

        x = layer_norm(DEEPNORM_ALPHA * x + 0.5 * swiglu(x, ffn_b_gate[l], ffn_b_up[l], ffn_b_down[l]),
                       ln_g[l, 2], ln_b[l, 2])
    return x
```

```python
import functools
import math

import numpy as np
import jax
import jax.numpy as jnp
from jax import lax
from jax.experimental import pallas as pl
from jax.experimental.pallas import tpu as pltpu

F32 = jnp.float32
BF16 = jnp.bfloat16

HEAD_DIM = 64
A_HEADS = 6
B_HEADS = 6
C_HEADS = 4
A_QK_DIM = 32
A_WIDTH = A_HEADS * HEAD_DIM
B_WIDTH = B_HEADS * HEAD_DIM
C_WIDTH = C_HEADS * HEAD_DIM
C_W_RANK = 64
C_A_RANK = 64
C_V_RANK = 32
C_G_RANK = 128
C_COLS = 3 * C_WIDTH + C_W_RANK + C_A_RANK + C_G_RANK
DILATED_PATTERNS = ((128, 1), (512, 4), (2048, 16))
ROPE_THETA = 10000.0
LN_EPS = 1e-5
RMS_EPS = 1e-5
C_GN_EPS = 64e-5
NEG_INF = -1e30
DECAY_RATE = math.exp(-0.5)

LANES = 128
SUBLANES = 8
VMEM_LIMIT_BYTES = 56 * 1024 * 1024

ROW_TILE = 512
FF_CHUNK = 256
ATT_BLOCK = 256
RWKV_CHUNK = 64
RWKV_CHUNKS_PER_STEP = 2

_NN = (((1,), (0,)), ((), ()))
_NT = (((1,), (1,)), ((), ()))


def _dot(a, b, dims=_NN):
    return lax.dot_general(a, b, dims, preferred_element_type=F32)


def _bdot(a, b, dims=_NN):
    return _dot(a.astype(BF16), b.astype(BF16), dims)


def _split2(x):
    hi = x.astype(BF16)
    lo = (x - hi.astype(F32)).astype(BF16)
    return hi, lo


def _split3(x):
    hi = x.astype(BF16)
    r = x - hi.astype(F32)
    mid = r.astype(BF16)
    lo = (r - mid.astype(F32)).astype(BF16)
    return hi, mid, lo


def _dot3(a, b, dims=_NN):
    a_hi, a_lo = _split2(a)
    b_hi, b_lo = _split2(b)
    return _dot(a_lo, b_hi, dims) + _dot(a_hi, b_lo, dims) + _dot(a_hi, b_hi, dims)


def _dot_sel_lhs(sel, x, dims=_NN):
    h, m, l = _split3(x)
    return _dot(sel, l, dims) + _dot(sel, m, dims) + _dot(sel, h, dims)


def _dot_sel_rhs(x, sel, dims=_NN):
    h, m, l = _split3(x)
    return _dot(l, sel, dims) + _dot(m, sel, dims) + _dot(h, sel, dims)


def _layer_norm(y, g, b):
    mu = jnp.mean(y, axis=-1, keepdims=True)
    d = y - mu
    var = jnp.mean(d * d, axis=-1, keepdims=True)
    return d * lax.rsqrt(var + LN_EPS) * g + b


def _cparams(sem):
    return pltpu.CompilerParams(dimension_semantics=sem, vmem_limit_bytes=VMEM_LIMIT_BYTES)


def _resident(shape):
    nd = len(shape)
    return pl.BlockSpec(shape, lambda *_: (0,) * nd, pipeline_mode=pl.Buffered(1))


def _ffn_ln_kernel(x_ref, wg_ref, wu_ref, wd_ref, g_ref, b_ref, o_ref, *, alpha, d_ff):
    x = x_ref[...]
    xb = x.astype(BF16)
    acc = jnp.zeros(x.shape, F32)
    for c0 in range(0, d_ff, FF_CHUNK):
        gate = _dot(xb, wg_ref[:, c0:c0 + FF_CHUNK])
        up = _dot(xb, wu_ref[:, c0:c0 + FF_CHUNK])
        h = (gate * jax.nn.sigmoid(gate)) * up
        acc = acc + _dot(h.astype(BF16), wd_ref[c0:c0 + FF_CHUNK, :])
    y = alpha * x + 0.5 * acc
    o_ref[...] = _layer_norm(y, g_ref[...], b_ref[...])


def _ffn_ln(x, wg, wu, wd, g, b, alpha):
    t, d = x.shape
    d_ff = wg.shape[1]
    tm = min(ROW_TILE, t)
    return pl.pallas_call(
        functools.partial(_ffn_ln_kernel, alpha=alpha, d_ff=d_ff),
        out_shape=jax.ShapeDtypeStruct((t, d), F32),
        grid=(t // tm,),
        in_specs=[
            pl.BlockSpec((tm, d), lambda i: (i, 0)),
            _resident((d, d_ff)), _resident((d, d_ff)), _resident((d_ff, d)),
            _resident((1, d)), _resident((1, d)),
        ],
        out_specs=pl.BlockSpec((tm, d), lambda i: (i, 0)),
        compiler_params=_cparams(("parallel",)),
        name="ffn_ln",
    )(x, wg, wu, wd, g, b)


def _rope_lanes(y, cos, sin_signed, group):
    half = group // 2
    lane = lax.broadcasted_iota(jnp.int32, (1, LANES), 1)
    first = (lane % group) < half
    swapped = jnp.where(first, pltpu.roll(y, LANES - half, 1), pltpu.roll(y, half, 1))
    return y * cos + swapped * sin_signed


def _in_proj_kernel(x_ref, w_ref, ca_ref, sa_ref, cb_ref, sb_ref,
                    aq_ref, ak_ref, av_ref, bq_ref, bk_ref, bv_ref, cc_ref):
    xb = x_ref[...].astype(BF16)
    ca, sa, cb, sb = ca_ref[...], sa_ref[...], cb_ref[...], sb_ref[...]
    a_scale = A_QK_DIM ** -0.5
    b_scale = HEAD_DIM ** -0.5
    col = 0
    plan = ((aq_ref, A_WIDTH, (ca, sa, A_QK_DIM), a_scale),
            (ak_ref, A_WIDTH, (ca, sa, A_QK_DIM), None),
            (av_ref, A_WIDTH, None, None),
            (bq_ref, B_WIDTH, (cb, sb, HEAD_DIM), b_scale),
            (bk_ref, B_WIDTH, (cb, sb, HEAD_DIM), None),
            (bv_ref, B_WIDTH, None, None))
    for ref, width, rope, scale in plan:
        for s0 in range(0, width, LANES):
            y = _dot(xb, w_ref[:, col + s0:col + s0 + LANES])
            if rope is not None:
                y = _rope_lanes(y, *rope)
            if scale is not None:
                y = y * scale
            ref[:, s0:s0 + LANES] = y.astype(ref.dtype)
        col += width
    for s0 in range(0, C_COLS, 2 * LANES):
        cc_ref[:, s0:s0 + 2 * LANES] = _dot(xb, w_ref[:, col + s0:col + s0 + 2 * LANES])


def _in_proj(x, w_in, tabs, s_len):
    t, d = x.shape
    n_in = w_in.shape[1]
    tm = min(ROW_TILE, s_len)
    spb = s_len // tm
    row = lambda i: (i, 0)
    tab = pl.BlockSpec((tm, LANES), lambda i: (i % spb, 0))
    qkv = jax.ShapeDtypeStruct((t, A_WIDTH), BF16)
    return pl.pallas_call(
        _in_proj_kernel,
        out_shape=(qkv,) * 6 + (jax.ShapeDtypeStruct((t, C_COLS), F32),),
        grid=(t // tm,),
        in_specs=[pl.BlockSpec((tm, d), row), _resident((d, n_in)), tab, tab, tab, tab],
        out_specs=tuple(pl.BlockSpec((tm, A_WIDTH), row) for _ in range(6))
        + (pl.BlockSpec((tm, C_COLS), row),),
        compiler_params=_cparams(("parallel",)),
        name="in_proj",
    )(x, w_in, *tabs)


def _rope_tables(s_len, group):
    half = group // 2
    inv = ROPE_THETA ** (-jnp.arange(0, group, 2, dtype=F32) / group)
    ang = jnp.arange(s_len, dtype=F32)[:, None] * inv[None, :]
    lane = np.arange(LANES)
    idx = lane % half
    sign = np.where((lane % group) < half, -1.0, 1.0).astype(np.float32)
    cos = jnp.cos(ang)[:, idx]
    sin = jnp.sin(ang)[:, idx] * sign[None, :]
    return cos, sin


def _diff_attn_kernel(lam_ref, g_ref, q_ref, k_ref, v_ref, o_ref, m_ref, l_ref, acc_ref,
                      *, blk, lam_init):
    qi = pl.program_id(2)
    lane = lax.broadcasted_iota(jnp.int32, (1, LANES), 1)
    q = q_ref[0]
    qm = [jnp.where((lane // A_QK_DIM) == c, q, jnp.zeros_like(q)) for c in range(4)]

    m_ref[...] = jnp.full(m_ref.shape, NEG_INF, F32)
    l_ref[...] = jnp.zeros(l_ref.shape, F32)
    acc_ref[...] = jnp.zeros(acc_ref.shape, F32)

    def block(kb, diagonal):
        start = pl.multiple_of(kb * blk, blk)
        k = k_ref[0, pl.ds(start, blk), :]
        v = v_ref[0, pl.ds(start, blk), :]
        if diagonal:
            r = lax.broadcasted_iota(jnp.int32, (blk, blk), 0)
            c_ = lax.broadcasted_iota(jnp.int32, (blk, blk), 1)
            causal = c_ <= r
        for c in range(4):
            s = _dot(qm[c], k, _NT)
            if diagonal:
                s = jnp.where(causal, s, NEG_INF)
            m_prev = m_ref[c]
            m_new = jnp.maximum(m_prev, jnp.max(s, axis=1, keepdims=True))
            alpha = jnp.exp(m_prev - m_new)
            p = jnp.exp(s - m_new)
            l_ref[c] = alpha * l_ref[c] + jnp.sum(p, axis=1, keepdims=True)
            acc_ref[c] = alpha * acc_ref[c] + _dot(p.astype(BF16), v)
            m_ref[c] = m_new

    def body(kb, carry):
        block(kb, False)
        return carry

    lax.fori_loop(0, qi, body, 0)
    block(qi, True)

    lp = lam_ref[...]
    lam = (jnp.exp(jnp.sum(lp[0:1] * lp[1:2], axis=1, keepdims=True))
           - jnp.exp(jnp.sum(lp[2:3] * lp[3:4], axis=1, keepdims=True)) + lam_init)
    heads = []
    for h in range(2):
        o0 = acc_ref[2 * h] / l_ref[2 * h]
        o1 = acc_ref[2 * h + 1] / l_ref[2 * h + 1]
        heads.append(o0 - lam * o1)
    lo = lane < HEAD_DIM
    o = jnp.where(lo, heads[0], heads[1])
    sq = o * o
    s0 = jnp.sum(jnp.where(lo, sq, 0.0), axis=1, keepdims=True)
    s1 = jnp.sum(jnp.where(lo, 0.0, sq), axis=1, keepdims=True)
    ms = jnp.where(lo, s0, s1) * (1.0 / HEAD_DIM)
    o = o * lax.rsqrt(ms + RMS_EPS) * g_ref[...] * (1.0 - lam_init)
    o_ref[0] = o.astype(o_ref.dtype)


def _diff_attn(q, k, v, lam_p, g, lam_init):
    b, s, w = q.shape
    blk = min(ATT_BLOCK, s)
    pairs = w // LANES
    return pl.pallas_call(
        functools.partial(_diff_attn_kernel, blk=blk, lam_init=lam_init),
        out_shape=jax.ShapeDtypeStruct((b, s, w), BF16),
        grid=(b, pairs, s // blk),
        in_specs=[
            _resident((4, LANES)), _resident((1, LANES)),
            pl.BlockSpec((1, blk, LANES), lambda bi, p, i: (bi, i, p)),
            pl.BlockSpec((1, s, LANES), lambda bi, p, i: (bi, 0, p)),
            pl.BlockSpec((1, s, LANES), lambda bi, p, i: (bi, 0, p)),
        ],
        out_specs=pl.BlockSpec((1, blk, LANES), lambda bi, p, i: (bi, i, p)),
        scratch_shapes=[pltpu.VMEM((4, blk, 1), F32), pltpu.VMEM((4, blk, 1), F32),
                        pltpu.VMEM((4, blk, LANES), F32)],
        compiler_params=_cparams(("parallel", "parallel", "arbitrary")),
        name="diff_attn",
    )(lam_p, g, q, k, v)


def _dilated_counts(blk):
    far = max(w for w, _ in DILATED_PATTERNS if w < max(p[0] for p in DILATED_PATTERNS))
    n_off = far // blk + 2
    i = np.arange(blk)[:, None]
    j = np.arange(blk)[None, :]
    out = np.zeros((n_off, blk, blk), np.float32)
    for o in range(n_off):
        delta = o * blk + i - j
        for window, dil in DILATED_PATTERNS:
            out[o] += ((delta >= 0) & (delta <= window) & (delta % dil == 0))
    return out


def _dil_attn_kernel(cnt_ref, g_ref, q_ref, k_ref, v_ref, o_ref, m_ref, l_ref, acc_ref,
                     *, blk, n_off):
    qi = pl.program_id(2)
    lane = lax.broadcasted_iota(jnp.int32, (1, LANES), 1)
    lo = lane < HEAD_DIM
    q = q_ref[0]
    zero = jnp.zeros_like(q)
    qm = [jnp.where(lo, q, zero), jnp.where(lo, zero, q)]

    m_ref[...] = jnp.full(m_ref.shape, NEG_INF, F32)
    l_ref[...] = jnp.zeros(l_ref.shape, F32)
    acc_ref[...] = jnp.zeros(acc_ref.shape, F32)

    def body(kb, carry):
        start = pl.multiple_of(kb * blk, blk)
        k = k_ref[0, pl.ds(start, blk), :]
        v = v_ref[0, pl.ds(start, blk), :]
        cnt = cnt_ref[jnp.minimum(qi - kb, n_off - 1)]
        live = cnt > 0.0
        for h in range(2):
            s = jnp.where(live, _dot(qm[h], k, _NT), NEG_INF)
            m_prev = m_ref[h]
            m_new = jnp.maximum(m_prev, jnp.max(s, axis=1, keepdims=True))
            alpha = jnp.exp(m_prev - m_new)
            p = cnt * jnp.exp(s - m_new)
            l_ref[h] = alpha * l_ref[h] + jnp.sum(p, axis=1, keepdims=True)
            acc_ref[h] = alpha * acc_ref[h] + _dot(p.astype(BF16), v)
            m_ref[h] = m_new
        return carry

    lax.fori_loop(0, qi + 1, body, 0)

    o = jnp.where(lo, acc_ref[0] / l_ref[0], acc_ref[1] / l_ref[1])
    sq = o * o
    s0 = jnp.sum(jnp.where(lo, sq, 0.0), axis=1, keepdims=True)
    s1 = jnp.sum(jnp.where(lo, 0.0, sq), axis=1, keepdims=True)
    ms = jnp.where(lo, s0, s1) * (1.0 / HEAD_DIM)
    o_ref[0] = (o * lax.rsqrt(ms + RMS_EPS) * g_ref[...]).astype(o_ref.dtype)


def _dil_attn(q, k, v, counts, g):
    b, s, w = q.shape
    blk = counts.shape[1]
    n_off = counts.shape[0]
    pairs = w // LANES
    return pl.pallas_call(
        functools.partial(_dil_attn_kernel, blk=blk, n_off=n_off),
        out_shape=jax.ShapeDtypeStruct((b, s, w), BF16),
        grid=(b, pairs, s // blk),
        in_specs=[
            _resident(counts.shape), _resident((1, LANES)),
            pl.BlockSpec((1, blk, LANES), lambda bi, p, i: (bi, i, p)),
            pl.BlockSpec((1, s, LANES), lambda bi, p, i: (bi, 0, p)),
            pl.BlockSpec((1, s, LANES), lambda bi, p, i: (bi, 0, p)),
        ],
        out_specs=pl.BlockSpec((1, blk, LANES), lambda bi, p, i: (bi, i, p)),
        scratch_shapes=[pltpu.VMEM((2, blk, 1), F32), pltpu.VMEM((2, blk, 1), F32),
                        pltpu.VMEM((2, blk, LANES), F32)],
        compiler_params=_cparams(("parallel", "parallel", "arbitrary")),
        name="dil_attn",
    )(counts, g, q, k, v)


def _solve_unit_lower(a, x):
    c = x.shape[0]
    nblk = c // SUBLANES
    rows = [x[SUBLANES * i:SUBLANES * (i + 1)] for i in range(nblk)]
    for t in range(c - 1):
        bi = t // SUBLANES
        xt = rows[bi][t % SUBLANES:t % SUBLANES + 1, :]
        for i in range(bi, nblk):
            a_col = a[SUBLANES * i:SUBLANES * (i + 1), t:t + 1]
            rows[i] = rows[i] - a_col * xt
    return jnp.concatenate(rows, axis=0)


def _to_slab(col_block, half, lower):
    lane = lax.broadcasted_iota(jnp.int32, (1, LANES), 1)
    want_low = lower
    have_low = half == 0
    src = col_block if want_low == have_low else pltpu.roll(col_block, HEAD_DIM, 1)
    keep = (lane < HEAD_DIM) if want_low else (lane >= HEAD_DIM)
    return jnp.where(keep, src, 0.0)


def _rwkv_kernel(*refs, n_chunks, has_vres):
    if has_vres:
        (cc_ref, vf_ref, mu_ref, w0_ref, w2_ref, a0_ref, a2_ref, g2_ref, kk_ref, ka_ref, rk_ref,
         gng_ref, gnb_ref, v0_ref, v1_ref, v2_ref, bd_ref, tri_ref,
         o_ref, carry_ref, hs_ref) = refs
    else:
        (cc_ref, mu_ref, w0_ref, w2_ref, a0_ref, a2_ref, g2_ref, kk_ref, ka_ref, rk_ref,
         gng_ref, gnb_ref, bd_ref, tri_ref,
         o_ref, vf_out_ref, carry_ref, hs_ref) = refs
    cw = C_WIDTH
    cl = RWKV_CHUNK

    @pl.when(pl.program_id(1) == 0)
    def _():
        carry_ref[...] = jnp.zeros(carry_ref.shape, F32)
        hs_ref[...] = jnp.zeros(hs_ref.shape, F32)

    c_raw = cc_ref[0]
    tb = c_raw.shape[0]
    row = lax.broadcasted_iota(jnp.int32, (tb, 1), 0)
    prev = jnp.where(row == 0, carry_ref[...], pltpu.roll(c_raw, 1, 0))
    carry_ref[...] = c_raw[tb - 1:tb, :]
    c = c_raw + (prev - c_raw) * mu_ref[...]

    r = c[:, 0:cw]
    k = c[:, cw:2 * cw]
    v = c[:, 2 * cw:3 * cw]
    xwa = c[:, 3 * cw:3 * cw + LANES]
    xg = c[:, 3 * cw + LANES:3 * cw + 2 * LANES]
    if has_vres:
        mix = jax.nn.sigmoid(v0_ref[...] + _dot3(_dot3(v, v1_ref[...]), v2_ref[...]))
        v = v + (vf_ref[0] - v) * mix
    else:
        vf_out_ref[0] = v
    bd = bd_ref[...]
    lw = -DECAY_RATE * jax.nn.sigmoid(w0_ref[...] + _dot3(jnp.tanh(xwa), w2_ref[...]))
    a = jax.nn.sigmoid(a0_ref[...] + _dot3(xwa, a2_ref[...]))
    gate = _dot3(jax.nn.sigmoid(xg), g2_ref[...])
    kkv = k * kk_ref[...]
    kappa = kkv / jnp.maximum(jnp.sqrt(_dot_sel_rhs(kkv * kkv, bd)), 1e-12)
    k2 = k * (1.0 + (a - 1.0) * ka_ref[...])
    bonus = _dot_sel_rhs(r * k2 * rk_ref[...], bd)
    beta = a * kappa

    lane = lax.broadcasted_iota(jnp.int32, (1, LANES), 1)
    low = lane < HEAD_DIM
    lane2 = lax.broadcasted_iota(jnp.int32, (1, cw), 1)
    rr = lax.broadcasted_iota(jnp.int32, (2 * cl, LANES), 0)
    cidx = lax.broadcasted_iota(jnp.int32, (2 * cl, LANES), 1) % cl
    tri_mask = cidx < jnp.where(rr < cl, rr, rr - cl + 1)
    eye = (lax.broadcasted_iota(jnp.int32, (2 * cl, LANES), 0)
           == lax.broadcasted_iota(jnp.int32, (2 * cl, LANES), 1))
    tri = tri_ref[...]

    y_chunks = []
    for ci in range(n_chunks):
        sl = slice(ci * cl, (ci + 1) * cl)
        lwc = lw[sl]
        cum = _dot_sel_lhs(tri, lwc)
        cum_end = cum[cl - 1:cl, :]
        rt = r[sl] * jnp.exp(cum)
        kt = kappa[sl] * jnp.exp(cum - lwc)
        e_inv = jnp.exp(-cum)
        bt = beta[sl] * e_inv
        k2t = k2[sl] * e_inv
        e_end = jnp.exp(cum_end - cum)
        bh = beta[sl] * e_end
        kh = k2[sl] * e_end
        p_end = jnp.exp(cum_end)
        vc = v[sl]
        right = jnp.concatenate([bt, k2t], axis=0)
        y_heads = []
        for h in range(C_HEADS):
            cb, half = divmod(h, 2)
            cs = slice(cb * LANES, (cb + 1) * LANES)
            hm = (lane2 // HEAD_DIM) == h
            left = jnp.concatenate([jnp.where(hm, kt, 0.0), jnp.where(hm, rt, 0.0)], axis=0)
            m = jnp.where(tri_mask, _dot3(left, right, _NT), 0.0)
            vw = _to_slab(vc[:, cs], half, lower=False)
            kw = _to_slab(kt[:, cs], half, lower=True)
            rw = _to_slab(rt[:, cs], half, lower=True)
            bkw = jnp.concatenate([_to_slab(bh[:, cs], half, True), _to_slab(kh[:, cs], half, True)], axis=0)
            zeros = jnp.zeros((cl, LANES), F32)
            rhs = kw + _dot3(m[0:cl], jnp.concatenate([zeros, vw], axis=0))
            x = _solve_unit_lower(m[0:cl], rhs)
            st = jnp.concatenate([-x, vw], axis=0)
            qy = rw + _dot3(m[cl:2 * cl], st)
            pe = _to_slab(jnp.broadcast_to(p_end[:, cs], (2 * cl, LANES)), half, True)
            gh = jnp.where(eye, pe, 0.0) + _dot3(bkw.T, st)
            hs = hs_ref[h]
            y_heads.append(qy + _dot3(qy, hs))
            h_new = gh + _dot3(gh, hs)
            keep = (lax.broadcasted_iota(jnp.int32, (2 * cl, LANES), 0) < cl) & (lane >= HEAD_DIM)
            hs_ref[h] = jnp.where(keep, h_new, 0.0)
        cols = [jnp.where(low, pltpu.roll(y_heads[2 * cb], HEAD_DIM, 1), y_heads[2 * cb + 1])
                for cb in range(C_HEADS // 2)]
        y_chunks.append(jnp.concatenate(cols, axis=1))
    y = jnp.concatenate(y_chunks, axis=0) if n_chunks > 1 else y_chunks[0]

    inv = 1.0 / HEAD_DIM
    mean = _dot_sel_rhs(y, bd) * inv
    d = y - mean
    var = _dot_sel_rhs(d * d, bd) * inv
    yn = d * lax.rsqrt(var + C_GN_EPS) * gng_ref[...] + gnb_ref[...]
    out = (yn + bonus * v) * gate
    o_ref[0] = out.astype(o_ref.dtype)


def _rwkv(cc, v_first, p, consts):
    b, s, _ = cc.shape
    n_chunks = RWKV_CHUNKS_PER_STEP
    tb = n_chunks * RWKV_CHUNK
    has_vres = v_first is not None
    blk_cc = pl.BlockSpec((1, tb, C_COLS), lambda bi, j: (bi, j, 0))
    blk_cw = pl.BlockSpec((1, tb, C_WIDTH), lambda bi, j: (bi, j, 0))
    vec = lambda n: _resident((1, n))
    params = [p["mu"], p["w0"], p["w2"], p["a0"], p["a2"], p["g2"], p["k_k"], p["k_a"], p["r_k"],
              p["gn_g"], p["gn_b"]]
    specs = [vec(C_COLS), vec(C_WIDTH), _resident((LANES, C_WIDTH)), vec(C_WIDTH),
             _resident((LANES, C_WIDTH)), _resident((LANES, C_WIDTH)), vec(C_WIDTH), vec(C_WIDTH),
             vec(C_WIDTH), vec(C_WIDTH), vec(C_WIDTH)]
    if has_vres:
        params += [p["v0"], p["v1"], p["v2"]]
        specs += [vec(C_WIDTH), _resident((C_WIDTH, LANES)), _resident((LANES, C_WIDTH))]
    params += [consts["bd"], consts["tri"]]
    specs += [_resident((C_WIDTH, C_WIDTH)), _resident((RWKV_CHUNK, RWKV_CHUNK))]
    o_shape = jax.ShapeDtypeStruct((b, s, C_WIDTH), BF16)
    if has_vres:
        args = [cc, v_first] + params
        in_specs = [blk_cc, blk_cw] + specs
        out_shape, out_specs = o_shape, blk_cw
    else:
        args = [cc] + params
        in_specs = [blk_cc] + specs
        out_shape = (o_shape, jax.ShapeDtypeStruct((b, s, C_WIDTH), F32))
        out_specs = (blk_cw, blk_cw)
    return pl.pallas_call(
        functools.partial(_rwkv_kernel, n_chunks=n_chunks, has_vres=has_vres),
        out_shape=out_shape,
        grid=(b, s // tb),
        in_specs=in_specs,
        out_specs=out_specs,
        scratch_shapes=[pltpu.VMEM((1, C_COLS), F32),
                        pltpu.VMEM((C_HEADS, 2 * RWKV_CHUNK, LANES), F32)],
        compiler_params=_cparams(("parallel", "arbitrary")),
        name="rwkv_vres" if has_vres else "rwkv_first",
    )(*args)


def _out_proj_ln_kernel(x_ref, oa_ref, ob_ref, oc_ref, w_ref, g_ref, b_ref, o_ref, *, alpha):
    mix = (_dot(oa_ref[...], w_ref[0:A_WIDTH, :])
           + _dot(ob_ref[...], w_ref[A_WIDTH:A_WIDTH + B_WIDTH, :])
           + _dot(oc_ref[...], w_ref[A_WIDTH + B_WIDTH:, :]))
    o_ref[...] = _layer_norm(alpha * x_ref[...] + mix, g_ref[...], b_ref[...])


def _out_proj_ln(x, oa, ob, oc, w_out, g, b, alpha):
    t, d = x.shape
    tm = min(ROW_TILE, t)
    row = lambda i: (i, 0)
    return pl.pallas_call(
        functools.partial(_out_proj_ln_kernel, alpha=alpha),
        out_shape=jax.ShapeDtypeStruct((t, d), F32),
        grid=(t // tm,),
        in_specs=[pl.BlockSpec((tm, d), row), pl.BlockSpec((tm, A_WIDTH), row),
                  pl.BlockSpec((tm, B_WIDTH), row), pl.BlockSpec((tm, C_WIDTH), row),
                  _resident(w_out.shape), _resident((1, d)), _resident((1, d))],
        out_specs=pl.BlockSpec((tm, d), row),
        compiler_params=_cparams(("parallel",)),
        name="out_proj_ln",
    )(x, oa, ob, oc, w_out, g, b)


def _pad_rows(w, total, offset):
    return jnp.zeros((total, w.shape[1]), w.dtype).at[offset:offset + w.shape[0]].set(w)


def kernel(x, ffn_a_gate, ffn_a_up, ffn_a_down, ffn_b_gate, ffn_b_up, ffn_b_down, ln_g, ln_b, w_in, w_out, a_lam_q1, a_lam_k1, a_lam_q2, a_lam_k2, a_norm_g, b_norm_g, c_mu, c_w0, c_w2, c_a0, c_a2, c_g2, c_k_k, c_k_a, c_r_k, c_gn_g, c_gn_b, c_v0, c_v1, c_v2):
    bsz, s_len, d_model = x.shape
    depth = w_in.shape[0]
    alpha = (2 * depth) ** 0.25
    t = bsz * s_len

    tabs = _rope_tables(s_len, A_QK_DIM) + _rope_tables(s_len, HEAD_DIM)
    counts = jnp.asarray(_dilated_counts(min(ATT_BLOCK, s_len)))
    head_of = np.arange(C_WIDTH) // HEAD_DIM
    consts = {
        "bd": jnp.asarray(head_of[:, None] == head_of[None, :], BF16),
        "tri": jnp.asarray(np.tril(np.ones((RWKV_CHUNK, RWKV_CHUNK), np.float32)), BF16),
    }
    row = lambda vct: vct.reshape(1, -1)
    tile2 = lambda vct: jnp.tile(vct, LANES // HEAD_DIM).reshape(1, LANES)

    h = x.reshape(t, d_model)
    v_first = None
    for l in range(depth):
        h = _ffn_ln(h, ffn_a_gate[l].astype(BF16), ffn_a_up[l].astype(BF16), ffn_a_down[l].astype(BF16),
                    row(ln_g[l, 0]), row(ln_b[l, 0]), alpha)

        aq, ak, av, bq, bk, bv, cc = _in_proj(h, w_in[l].astype(BF16), tabs, s_len)
        shp = (bsz, s_len, A_WIDTH)

        lam_init = 0.8 - 0.6 * math.exp(-0.3 * l)
        lam_p = jnp.zeros((4, LANES), F32).at[:, :A_QK_DIM].set(
            jnp.stack([a_lam_q1[l], a_lam_k1[l], a_lam_q2[l], a_lam_k2[l]]))
        o_a = _diff_attn(aq.reshape(shp), ak.reshape(shp), av.reshape(shp), lam_p,
                         tile2(a_norm_g[l]), lam_init)
        o_b = _dil_attn(bq.reshape(shp), bk.reshape(shp), bv.reshape(shp), counts, tile2(b_norm_g[l]))

        p = {"mu": row(c_mu[l]), "w0": row(c_w0[l]), "a0": row(c_a0[l]),
             "w2": _pad_rows(c_w2[l], LANES, 0), "a2": _pad_rows(c_a2[l], LANES, C_W_RANK),
             "g2": c_g2[l], "k_k": row(c_k_k[l]), "k_a": row(c_k_a[l]), "r_k": row(c_r_k[l]),
             "gn_g": row(c_gn_g[l]), "gn_b": row(c_gn_b[l])}
        cc3 = cc.reshape(bsz, s_len, C_COLS)
        if l == 0:
            o_c, v_first = _rwkv(cc3, None, p, consts)
        else:
            p["v0"] = row(c_v0[l - 1])
            p["v1"] = jnp.zeros((C_WIDTH, LANES), F32).at[:, :C_V_RANK].set(c_v1[l - 1])
            p["v2"] = _pad_rows(c_v2[l - 1], LANES, 0)
            o_c = _rwkv(cc3, v_first, p, consts)

        h = _out_proj_ln(h, o_a.reshape(t, A_WIDTH), o_b.reshape(t, B_WIDTH), o_c.reshape(t, C_WIDTH),
                         w_out[l].astype(BF16), row(ln_g[l, 1]), row(ln_b[l, 1]), alpha)
        h = _ffn_ln(h, ffn_b_gate[l].astype(BF16), ffn_b_up[l].astype(BF16), ffn_b_down[l].astype(BF16),
                    row(ln_g[l, 2]), row(ln_b[l, 2]), alpha)
    return h.reshape(bsz, s_len, d_model)
```

```python
import functools
import math

import numpy as np
import jax
import jax.numpy as jnp
from jax import lax
from jax.experimental import pallas as pl
from jax.experimental.pallas import tpu as pltpu

F32 = jnp.float32
BF16 = jnp.bfloat16

HEAD_DIM = 64
A_HEADS = 6
B_HEADS = 6
C_HEADS = 4
A_QK_DIM = 32
A_WIDTH = A_HEADS * HEAD_DIM
B_WIDTH = B_HEADS * HEAD_DIM
C_WIDTH = C_HEADS * HEAD_DIM
C_W_RANK = 64
C_A_RANK = 64
C_V_RANK = 32
C_G_RANK = 128
C_COLS = 3 * C_WIDTH + C_W_RANK + C_A_RANK + C_G_RANK
DILATED_PATTERNS = ((128, 1), (512, 4), (2048, 16))
ROPE_THETA = 10000.0
LN_EPS = 1e-5
RMS_EPS = 1e-5
C_GN_EPS = 64e-5
NEG_INF = -1e30
DECAY_RATE = math.exp(-0.5)
LOG2_E = math.log2(math.e)

LANES = 128
SUBLANES = 8
VMEM_LIMIT_BYTES = 56 * 1024 * 1024

ROW_TILE = 512
FF_CHUNK = 256
ATT_BLOCK = 512
RWKV_CHUNK = 64
RWKV_CHUNKS_PER_STEP = 2
SOLVE_BLOCK = 16

_NN = (((1,), (0,)), ((), ()))
_NT = (((1,), (1,)), ((), ()))


def _dot(a, b, dims=_NN):
    return lax.dot_general(a, b, dims, preferred_element_type=F32)


def _split2(x):
    hi = x.astype(BF16)
    lo = (x - hi.astype(F32)).astype(BF16)
    return hi, lo


def _split3(x):
    hi = x.astype(BF16)
    r = x - hi.astype(F32)
    mid = r.astype(BF16)
    lo = (r - mid.astype(F32)).astype(BF16)
    return hi, mid, lo


def _dot3(a, b, dims=_NN):
    a_hi, a_lo = _split2(a)
    b_hi, b_lo = _split2(b)
    return _dot(a_lo, b_hi, dims) + _dot(a_hi, b_lo, dims) + _dot(a_hi, b_hi, dims)


def _dot_sel_lhs(sel, x, dims=_NN):
    h, m, l = _split3(x)
    return _dot(sel, l, dims) + _dot(sel, m, dims) + _dot(sel, h, dims)


def _dot_sel_rhs(x, sel, dims=_NN):
    h, m, l = _split3(x)
    return _dot(l, sel, dims) + _dot(m, sel, dims) + _dot(h, sel, dims)


def _layer_norm(y, g, b):
    mu = jnp.mean(y, axis=-1, keepdims=True)
    d = y - mu
    var = jnp.mean(d * d, axis=-1, keepdims=True)
    return d * lax.rsqrt(var + LN_EPS) * g + b


def _cparams(sem):
    return pltpu.CompilerParams(dimension_semantics=sem, vmem_limit_bytes=VMEM_LIMIT_BYTES)


def _resident(shape):
    nd = len(shape)
    return pl.BlockSpec(shape, lambda *_: (0,) * nd, pipeline_mode=pl.Buffered(1))


def _ffn_ln_kernel(x_ref, wg_ref, wu_ref, wd_ref, g_ref, b_ref, o_ref, *, alpha, d_ff):
    x = x_ref[...]
    xb = x.astype(BF16)
    acc = jnp.zeros(x.shape, F32)
    for c0 in range(0, d_ff, FF_CHUNK):
        gate = _dot(xb, wg_ref[:, c0:c0 + FF_CHUNK])
        up = _dot(xb, wu_ref[:, c0:c0 + FF_CHUNK])
        h = (gate * jax.nn.sigmoid(gate)) * up
        acc = acc + _dot(h.astype(BF16), wd_ref[c0:c0 + FF_CHUNK, :])
    y = alpha * x + 0.5 * acc
    o_ref[...] = _layer_norm(y, g_ref[...], b_ref[...])


def _ffn_ln(x, wg, wu, wd, g, b, alpha):
    t, d = x.shape
    d_ff = wg.shape[1]
    tm = min(ROW_TILE, t)
    return pl.pallas_call(
        functools.partial(_ffn_ln_kernel, alpha=alpha, d_ff=d_ff),
        out_shape=jax.ShapeDtypeStruct((t, d), F32),
        grid=(t // tm,),
        in_specs=[
            pl.BlockSpec((tm, d), lambda i: (i, 0)),
            _resident((d, d_ff)), _resident((d, d_ff)), _resident((d_ff, d)),
            _resident((1, d)), _resident((1, d)),
        ],
        out_specs=pl.BlockSpec((tm, d), lambda i: (i, 0)),
        compiler_params=_cparams(("parallel",)),
        name="ffn_ln",
    )(x, wg, wu, wd, g, b)


def _rope_lanes(y, cos, sin_signed, group):
    half = group // 2
    lane = lax.broadcasted_iota(jnp.int32, (1, LANES), 1)
    first = (lane % group) < half
    swapped = jnp.where(first, pltpu.roll(y, LANES - half, 1), pltpu.roll(y, half, 1))
    return y * cos + swapped * sin_signed


def _in_proj_kernel(x_ref, w_ref, ca_ref, sa_ref, cb_ref, sb_ref,
                    aq_ref, ak_ref, av_ref, bq_ref, bk_ref, bv_ref, cc_ref):
    xb = x_ref[...].astype(BF16)
    ca, sa, cb, sb = ca_ref[...], sa_ref[...], cb_ref[...], sb_ref[...]
    a_scale = A_QK_DIM ** -0.5 * LOG2_E
    b_scale = HEAD_DIM ** -0.5 * LOG2_E
    col = 0
    plan = ((aq_ref, A_WIDTH, (ca, sa, A_QK_DIM), a_scale),
            (ak_ref, A_WIDTH, (ca, sa, A_QK_DIM), None),
            (av_ref, A_WIDTH, None, None),
            (bq_ref, B_WIDTH, (cb, sb, HEAD_DIM), b_scale),
            (bk_ref, B_WIDTH, (cb, sb, HEAD_DIM), None),
            (bv_ref, B_WIDTH, None, None))
    for ref, width, rope, scale in plan:
        for s0 in range(0, width, LANES):
            y = _dot(xb, w_ref[:, col + s0:col + s0 + LANES])
            if rope is not None:
                y = _rope_lanes(y, *rope)
            if scale is not None:
                y = y * scale
            ref[:, s0:s0 + LANES] = y.astype(ref.dtype)
        col += width
    for s0 in range(0, C_COLS, 2 * LANES):
        cc_ref[:, s0:s0 + 2 * LANES] = _dot(xb, w_ref[:, col + s0:col + s0 + 2 * LANES])


def _in_proj(x, w_in, tabs, s_len):
    t, d = x.shape
    n_in = w_in.shape[1]
    tm = min(ROW_TILE, s_len)
    spb = s_len // tm
    row = lambda i: (i, 0)
    tab = pl.BlockSpec((tm, LANES), lambda i: (i % spb, 0))
    qkv = jax.ShapeDtypeStruct((t, A_WIDTH), BF16)
    return pl.pallas_call(
        _in_proj_kernel,
        out_shape=(qkv,) * 6 + (jax.ShapeDtypeStruct((t, C_COLS), F32),),
        grid=(t // tm,),
        in_specs=[pl.BlockSpec((tm, d), row), _resident((d, n_in)), tab, tab, tab, tab],
        out_specs=tuple(pl.BlockSpec((tm, A_WIDTH), row) for _ in range(6))
        + (pl.BlockSpec((tm, C_COLS), row),),
        compiler_params=_cparams(("parallel",)),
        name="in_proj",
    )(x, w_in, *tabs)


def _rope_tables(s_len, group):
    half = group // 2
    inv = ROPE_THETA ** (-jnp.arange(0, group, 2, dtype=F32) / group)
    ang = jnp.arange(s_len, dtype=F32)[:, None] * inv[None, :]
    lane = np.arange(LANES)
    idx = lane % half
    sign = np.where((lane % group) < half, -1.0, 1.0).astype(np.float32)
    cos = jnp.cos(ang)[:, idx]
    sin = jnp.sin(ang)[:, idx] * sign[None, :]
    return cos, sin


def _attn_two_pass(q_stack_ref, k_ref, v_ref, s_ref, mx_ref, acc_ref, blk, n_comb):
    lane = lax.broadcasted_iota(jnp.int32, (1, LANES), 1)
    low = lane < HEAD_DIM
    mx_ref[...] = jnp.full(mx_ref.shape, NEG_INF, F32)
    acc_ref[...] = jnp.zeros(acc_ref.shape, F32)

    def pass_a(kb, bias):
        start = pl.multiple_of(kb * blk, blk)
        s = _dot(q_stack_ref[...], k_ref[0, pl.ds(start, blk), :], _NT)
        if bias is not None:
            s = s + bias
        s_ref[kb] = s
        m = mx_ref[...]
        for j in range(blk // LANES):
            m = jnp.maximum(m, s[:, j * LANES:(j + 1) * LANES])
        mx_ref[...] = m

    def pass_b(kb, carry):
        start = pl.multiple_of(kb * blk, blk)
        v = v_ref[0, pl.ds(start, blk), :]
        one = jnp.ones_like(v)
        vx = (jnp.where(low, v, one), jnp.where(low, one, v))
        s = s_ref[kb]
        m = mx_ref[...]
        p = jnp.concatenate([jnp.exp2(s[:, j * LANES:(j + 1) * LANES] - m)
                             for j in range(blk // LANES)], axis=1).astype(BF16)
        per_head = n_comb // 2
        for c in range(n_comb):
            acc_ref[c] = acc_ref[c] + _dot(p[c * blk:(c + 1) * blk], vx[c // per_head])
        return carry

    return pass_a, pass_b


def _finish_max(mx_ref):
    mx_ref[...] = jnp.broadcast_to(jnp.max(mx_ref[...], axis=1, keepdims=True), mx_ref.shape)


def _normalized(acc):
    return acc / pltpu.roll(acc, HEAD_DIM, 1)


def _head_rms(o, g):
    lane = lax.broadcasted_iota(jnp.int32, (1, LANES), 1)
    low = lane < HEAD_DIM
    sq = o * o
    s0 = jnp.sum(jnp.where(low, sq, 0.0), axis=1, keepdims=True)
    s1 = jnp.sum(jnp.where(low, 0.0, sq), axis=1, keepdims=True)
    ms = jnp.where(low, s0, s1) * (1.0 / HEAD_DIM)
    return o * lax.rsqrt(ms + RMS_EPS) * g


def _diff_attn_kernel(lam_ref, g_ref, q_ref, k_ref, v_ref, o_ref, q4_ref, s_ref, mx_ref, acc_ref,
                      *, blk, lam_init):
    qi = pl.program_id(2)
    lane = lax.broadcasted_iota(jnp.int32, (1, LANES), 1)
    q = q_ref[0]
    for c in range(4):
        q4_ref[c * blk:(c + 1) * blk, :] = jnp.where((lane // A_QK_DIM) == c, q, jnp.zeros_like(q))

    pass_a, pass_b = _attn_two_pass(q4_ref, k_ref, v_ref, s_ref, mx_ref, acc_ref, blk, 4)

    def body_a(kb, carry):
        pass_a(kb, None)
        return carry

    lax.fori_loop(0, qi, body_a, 0)
    r = lax.broadcasted_iota(jnp.int32, (4 * blk, blk), 0) % blk
    c_ = lax.broadcasted_iota(jnp.int32, (4 * blk, blk), 1)
    pass_a(qi, jnp.where(c_ <= r, 0.0, NEG_INF))
    _finish_max(mx_ref)
    lax.fori_loop(0, qi + 1, pass_b, 0)

    lp = lam_ref[...]
    lam = (jnp.exp(jnp.sum(lp[0:1] * lp[1:2], axis=1, keepdims=True))
           - jnp.exp(jnp.sum(lp[2:3] * lp[3:4], axis=1, keepdims=True)) + lam_init)
    heads = [_normalized(acc_ref[2 * h]) - lam * _normalized(acc_ref[2 * h + 1]) for h in range(2)]
    o = jnp.where(lane < HEAD_DIM, heads[0], heads[1])
    o_ref[0] = (_head_rms(o, g_ref[...]) * (1.0 - lam_init)).astype(o_ref.dtype)


def _attn_call(kernel_fn, name, n_comb, consts, const_specs, q, k, v, blk):
    b, s, w = q.shape
    pairs = w // LANES
    return pl.pallas_call(
        kernel_fn,
        out_shape=jax.ShapeDtypeStruct((b, s, w), BF16),
        grid=(b, pairs, s // blk),
        in_specs=const_specs + [
            pl.BlockSpec((1, blk, LANES), lambda bi, p, i: (bi, i, p)),
            pl.BlockSpec((1, s, LANES), lambda bi, p, i: (bi, 0, p)),
            pl.BlockSpec((1, s, LANES), lambda bi, p, i: (bi, 0, p)),
        ],
        out_specs=pl.BlockSpec((1, blk, LANES), lambda bi, p, i: (bi, i, p)),
        scratch_shapes=[pltpu.VMEM((n_comb * blk, LANES), BF16),
                        pltpu.VMEM((s // blk, n_comb * blk, blk), F32),
                        pltpu.VMEM((n_comb * blk, LANES), F32),
                        pltpu.VMEM((n_comb, blk, LANES), F32)],
        compiler_params=_cparams(("parallel", "parallel", "arbitrary")),
        name=name,
    )(*consts, q, k, v)


def _diff_attn(q, k, v, lam_p, g, lam_init):
    blk = min(ATT_BLOCK, q.shape[1])
    return _attn_call(functools.partial(_diff_attn_kernel, blk=blk, lam_init=lam_init), "diff_attn", 4,
                      [lam_p, g], [_resident((4, LANES)), _resident((1, LANES))], q, k, v, blk)


def _dilated_log2_counts(blk):
    far = max(w for w, _ in DILATED_PATTERNS if w < max(p[0] for p in DILATED_PATTERNS))
    n_off = far // blk + 2
    i = np.arange(blk)[:, None]
    j = np.arange(blk)[None, :]
    cnt = np.zeros((n_off, blk, blk), np.float64)
    for o in range(n_off):
        delta = o * blk + i - j
        for window, dil in DILATED_PATTERNS:
            cnt[o] += ((delta >= 0) & (delta <= window) & (delta % dil == 0))
    return np.where(cnt > 0, np.log2(np.maximum(cnt, 1.0)), NEG_INF).astype(np.float32)


def _dil_attn_kernel(bias_ref, g_ref, q_ref, k_ref, v_ref, o_ref, q2_ref, s_ref, mx_ref, acc_ref,
                     *, blk, n_off):
    qi = pl.program_id(2)
    lane = lax.broadcasted_iota(jnp.int32, (1, LANES), 1)
    low = lane < HEAD_DIM
    q = q_ref[0]
    zero = jnp.zeros_like(q)
    q2_ref[0:blk, :] = jnp.where(low, q, zero)
    q2_ref[blk:2 * blk, :] = jnp.where(low, zero, q)

    pass_a, pass_b = _attn_two_pass(q2_ref, k_ref, v_ref, s_ref, mx_ref, acc_ref, blk, 2)

    def body_a(kb, carry):
        bias = bias_ref[jnp.minimum(qi - kb, n_off - 1)]
        pass_a(kb, jnp.concatenate([bias, bias], axis=0))
        return carry

    lax.fori_loop(0, qi + 1, body_a, 0)
    _finish_max(mx_ref)
    lax.fori_loop(0, qi + 1, pass_b, 0)

    o = jnp.where(low, _normalized(acc_ref[0]), _normalized(acc_ref[1]))
    o_ref[0] = _head_rms(o, g_ref[...]).astype(o_ref.dtype)


def _dil_attn(q, k, v, bias, g):
    blk = bias.shape[1]
    return _attn_call(functools.partial(_dil_attn_kernel, blk=blk, n_off=bias.shape[0]), "dil_attn", 2,
                      [bias, g], [_resident(bias.shape), _resident((1, LANES))], q, k, v, blk)


def _solve_unit_lower(a_list, x_list):
    n = len(x_list)
    c = x_list[0].shape[0]
    nb = c // SOLVE_BLOCK
    sub = SOLVE_BLOCK // SUBLANES
    a_split = [_split2(a[:, 0:c]) for a in a_list]
    xs = [[x[SOLVE_BLOCK * i:SOLVE_BLOCK * (i + 1)] for i in range(nb)] for x in x_list]
    zeros = jnp.zeros((SOLVE_BLOCK, LANES), F32)
    for b in range(nb):
        base = b * SOLVE_BLOCK
        rows = [[xs[u][b][SUBLANES * i:SUBLANES * (i + 1)] for i in range(sub)] for u in range(n)]
        for t in range(SOLVE_BLOCK - 1):
            bi = t // SUBLANES
            for u in range(n):
                xt = rows[u][bi][t % SUBLANES:t % SUBLANES + 1, :]
                for i in range(bi, sub):
                    r0 = base + SUBLANES * i
                    a_col = a_list[u][r0:r0 + SUBLANES, base + t:base + t + 1]
                    rows[u][i] = rows[u][i] - a_col * xt
        for u in range(n):
            xs[u][b] = jnp.concatenate(rows[u], axis=0)
        if b + 1 < nb:
            below = slice(base + SOLVE_BLOCK, c)
            for u in range(n):
                a_hi, a_lo = a_split[u]
                x_hi, x_lo = _split2(jnp.concatenate(
                    [zeros] * b + [xs[u][b]] + [zeros] * (nb - 1 - b), axis=0))
                upd = _dot(a_lo[below], x_hi) + _dot(a_hi[below], x_lo) + _dot(a_hi[below], x_hi)
                for j in range(b + 1, nb):
                    xs[u][j] = xs[u][j] - upd[(j - b - 1) * SOLVE_BLOCK:(j - b) * SOLVE_BLOCK]
    return [jnp.concatenate(x, axis=0) for x in xs]


def _to_slab(col_block, half, lower):
    lane = lax.broadcasted_iota(jnp.int32, (1, LANES), 1)
    want_low = lower
    have_low = half == 0
    src = col_block if want_low == have_low else pltpu.roll(col_block, HEAD_DIM, 1)
    keep = (lane < HEAD_DIM) if want_low else (lane >= HEAD_DIM)
    return jnp.where(keep, src, 0.0)


def _rwkv_kernel(*refs, n_chunks, has_vres):
    if has_vres:
        (cc_ref, vf_ref, mu_ref, w0_ref, w2_ref, a0_ref, a2_ref, g2_ref, kk_ref, ka_ref, rk_ref,
         gng_ref, gnb_ref, v0_ref, v1_ref, v2_ref, bd_ref, tri_ref,
         o_ref, carry_ref, hs_ref) = refs
    else:
        (cc_ref, mu_ref, w0_ref, w2_ref, a0_ref, a2_ref, g2_ref, kk_ref, ka_ref, rk_ref,
         gng_ref, gnb_ref, bd_ref, tri_ref,
         o_ref, vf_out_ref, carry_ref, hs_ref) = refs
    cw = C_WIDTH
    cl = RWKV_CHUNK

    @pl.when(pl.program_id(1) == 0)
    def _():
        carry_ref[...] = jnp.zeros(carry_ref.shape, F32)
        hs_ref[...] = jnp.zeros(hs_ref.shape, F32)

    c_raw = cc_ref[0]
    tb = c_raw.shape[0]
    row = lax.broadcasted_iota(jnp.int32, (tb, 1), 0)
    prev = jnp.where(row == 0, carry_ref[...], pltpu.roll(c_raw, 1, 0))
    carry_ref[...] = c_raw[tb - 1:tb, :]
    c = c_raw + (prev - c_raw) * mu_ref[...]

    r = c[:, 0:cw]
    k = c[:, cw:2 * cw]
    v = c[:, 2 * cw:3 * cw]
    xwa = c[:, 3 * cw:3 * cw + LANES]
    xg = c[:, 3 * cw + LANES:3 * cw + 2 * LANES]
    if has_vres:
        mix = jax.nn.sigmoid(v0_ref[...] + _dot3(_dot3(v, v1_ref[...]), v2_ref[...]))
        v = v + (vf_ref[0] - v) * mix
    else:
        vf_out_ref[0] = v
    bd = bd_ref[...]
    lw = -DECAY_RATE * jax.nn.sigmoid(w0_ref[...] + _dot3(jnp.tanh(xwa), w2_ref[...]))
    a = jax.nn.sigmoid(a0_ref[...] + _dot3(xwa, a2_ref[...]))
    gate = _dot3(jax.nn.sigmoid(xg), g2_ref[...])
    kkv = k * kk_ref[...]
    kappa = kkv / jnp.maximum(jnp.sqrt(_dot_sel_rhs(kkv * kkv, bd)), 1e-12)
    k2 = k * (1.0 + (a - 1.0) * ka_ref[...])
    bonus = _dot_sel_rhs(r * k2 * rk_ref[...], bd)
    beta = a * kappa

    lane = lax.broadcasted_iota(jnp.int32, (1, LANES), 1)
    low = lane < HEAD_DIM
    lane2 = lax.broadcasted_iota(jnp.int32, (1, cw), 1)
    rr = lax.broadcasted_iota(jnp.int32, (2 * cl, LANES), 0)
    cidx = lax.broadcasted_iota(jnp.int32, (2 * cl, LANES), 1) % cl
    tri_mask = cidx < jnp.where(rr < cl, rr, rr - cl + 1)
    eye = (lax.broadcasted_iota(jnp.int32, (2 * cl, LANES), 0)
           == lax.broadcasted_iota(jnp.int32, (2 * cl, LANES), 1))
    tri = tri_ref[...]

    zeros = jnp.zeros((cl, LANES), F32)
    units = []
    for ci in range(n_chunks):
        sl = slice(ci * cl, (ci + 1) * cl)
        lwc = lw[sl]
        cum = _dot_sel_lhs(tri, lwc)
        cum_end = cum[cl - 1:cl, :]
        rt = r[sl] * jnp.exp(cum)
        kt = kappa[sl] * jnp.exp(cum - lwc)
        e_inv = jnp.exp(-cum)
        e_end = jnp.exp(cum_end - cum)
        bh = beta[sl] * e_end
        kh = k2[sl] * e_end
        p_end = jnp.exp(cum_end)
        vc = v[sl]
        right = jnp.concatenate([beta[sl] * e_inv, k2[sl] * e_inv], axis=0)
        for h in range(C_HEADS):
            cb, half = divmod(h, 2)
            cs = slice(cb * LANES, (cb + 1) * LANES)
            hm = (lane2 // HEAD_DIM) == h
            units.append(dict(
                ci=ci, h=h, right=right,
                left=jnp.concatenate([jnp.where(hm, kt, 0.0), jnp.where(hm, rt, 0.0)], axis=0),
                vw=_to_slab(vc[:, cs], half, lower=False),
                kw=_to_slab(kt[:, cs], half, lower=True),
                rw=_to_slab(rt[:, cs], half, lower=True),
                bkw=jnp.concatenate([_to_slab(bh[:, cs], half, True), _to_slab(kh[:, cs], half, True)], axis=0),
                pe=_to_slab(jnp.broadcast_to(p_end[:, cs], (2 * cl, LANES)), half, True)))
    for u in units:
        u["m"] = jnp.where(tri_mask, _dot3(u["left"], u["right"], _NT), 0.0)
    for u in units:
        u["rhs"] = u["kw"] + _dot3(u["m"][0:cl], jnp.concatenate([zeros, u["vw"]], axis=0))
    xs = _solve_unit_lower([u["m"][0:cl] for u in units], [u["rhs"] for u in units])
    for u, x in zip(units, xs):
        u["st"] = jnp.concatenate([-x, u["vw"]], axis=0)
    for u in units:
        u["qy"] = u["rw"] + _dot3(u["m"][cl:2 * cl], u["st"])
    for u in units:
        u["gh"] = jnp.where(eye, u["pe"], 0.0) + _dot3(u["bkw"].T, u["st"])
    keep = (lax.broadcasted_iota(jnp.int32, (2 * cl, LANES), 0) < cl) & (lane >= HEAD_DIM)
    for u in units:
        hs = hs_ref[u["h"]]
        u["y"] = u["qy"] + _dot3(u["qy"], hs)
        hs_ref[u["h"]] = jnp.where(keep, u["gh"] + _dot3(u["gh"], hs), 0.0)
    y_chunks = []
    for ci in range(n_chunks):
        yh = [u["y"] for u in units if u["ci"] == ci]
        cols = [jnp.where(low, pltpu.roll(yh[2 * cb], HEAD_DIM, 1), yh[2 * cb + 1])
                for cb in range(C_HEADS // 2)]
        y_chunks.append(jnp.concatenate(cols, axis=1))
    y = jnp.concatenate(y_chunks, axis=0) if n_chunks > 1 else y_chunks[0]

    inv = 1.0 / HEAD_DIM
    mean = _dot_sel_rhs(y, bd) * inv
    d = y - mean
    var = _dot_sel_rhs(d * d, bd) * inv
    yn = d * lax.rsqrt(var + C_GN_EPS) * gng_ref[...] + gnb_ref[...]
    out = (yn + bonus * v) * gate
    o_ref[0] = out.astype(o_ref.dtype)


def _rwkv(cc, v_first, p, consts):
    b, s, _ = cc.shape
    n_chunks = RWKV_CHUNKS_PER_STEP
    tb = n_chunks * RWKV_CHUNK
    has_vres = v_first is not None
    blk_cc = pl.BlockSpec((1, tb, C_COLS), lambda bi, j: (bi, j, 0))
    blk_cw = pl.BlockSpec((1, tb, C_WIDTH), lambda bi, j: (bi, j, 0))
    vec = lambda n: _resident((1, n))
    params = [p["mu"], p["w0"], p["w2"], p["a0"], p["a2"], p["g2"], p["k_k"], p["k_a"], p["r_k"],
              p["gn_g"], p["gn_b"]]
    specs = [vec(C_COLS), vec(C_WIDTH), _resident((LANES, C_WIDTH)), vec(C_WIDTH),
             _resident((LANES, C_WIDTH)), _resident((LANES, C_WIDTH)), vec(C_WIDTH), vec(C_WIDTH),
             vec(C_WIDTH), vec(C_WIDTH), vec(C_WIDTH)]
    if has_vres:
        params += [p["v0"], p["v1"], p["v2"]]
        specs += [vec(C_WIDTH), _resident((C_WIDTH, LANES)), _resident((LANES, C_WIDTH))]
    params += [consts["bd"], consts["tri"]]
    specs += [_resident((C_WIDTH, C_WIDTH)), _resident((RWKV_CHUNK, RWKV_CHUNK))]
    o_shape = jax.ShapeDtypeStruct((b, s, C_WIDTH), BF16)
    if has_vres:
        args = [cc, v_first] + params
        in_specs = [blk_cc, blk_cw] + specs
        out_shape, out_specs = o_shape, blk_cw
    else:
        args = [cc] + params
        in_specs = [blk_cc] + specs
        out_shape = (o_shape, jax.ShapeDtypeStruct((b, s, C_WIDTH), F32))
        out_specs = (blk_cw, blk_cw)
    return pl.pallas_call(
        functools.partial(_rwkv_kernel, n_chunks=n_chunks, has_vres=has_vres),
        out_shape=out_shape,
        grid=(b, s // tb),
        in_specs=in_specs,
        out_specs=out_specs,
        scratch_shapes=[pltpu.VMEM((1, C_COLS), F32),
                        pltpu.VMEM((C_HEADS, 2 * RWKV_CHUNK, LANES), F32)],
        compiler_params=_cparams(("parallel", "arbitrary")),
        name="rwkv_vres" if has_vres else "rwkv_first",
    )(*args)


def _out_proj_ln_kernel(x_ref, oa_ref, ob_ref, oc_ref, w_ref, g_ref, b_ref, o_ref, *, alpha):
    mix = (_dot(oa_ref[...], w_ref[0:A_WIDTH, :])
           + _dot(ob_ref[...], w_ref[A_WIDTH:A_WIDTH + B_WIDTH, :])
           + _dot(oc_ref[...], w_ref[A_WIDTH + B_WIDTH:, :]))
    o_ref[...] = _layer_norm(alpha * x_ref[...] + mix, g_ref[...], b_ref[...])


def _out_proj_ln(x, oa, ob, oc, w_out, g, b, alpha):
    t, d = x.shape
    tm = min(ROW_TILE, t)
    row = lambda i: (i, 0)
    return pl.pallas_call(
        functools.partial(_out_proj_ln_kernel, alpha=alpha),
        out_shape=jax.ShapeDtypeStruct((t, d), F32),
        grid=(t // tm,),
        in_specs=[pl.BlockSpec((tm, d), row), pl.BlockSpec((tm, A_WIDTH), row),
                  pl.BlockSpec((tm, B_WIDTH), row), pl.BlockSpec((tm, C_WIDTH), row),
                  _resident(w_out.shape), _resident((1, d)), _resident((1, d))],
        out_specs=pl.BlockSpec((tm, d), row),
        compiler_params=_cparams(("parallel",)),
        name="out_proj_ln",
    )(x, oa, ob, oc, w_out, g, b)


def _pad_rows(w, total, offset):
    return jnp.zeros((total, w.shape[1]), w.dtype).at[offset:offset + w.shape[0]].set(w)


def kernel(x, ffn_a_gate, ffn_a_up, ffn_a_down, ffn_b_gate, ffn_b_up, ffn_b_down, ln_g, ln_b, w_in, w_out, a_lam_q1, a_lam_k1, a_lam_q2, a_lam_k2, a_norm_g, b_norm_g, c_mu, c_w0, c_w2, c_a0, c_a2, c_g2, c_k_k, c_k_a, c_r_k, c_gn_g, c_gn_b, c_v0, c_v1, c_v2):
    bsz, s_len, d_model = x.shape
    depth = w_in.shape[0]
    alpha = (2 * depth) ** 0.25
    t = bsz * s_len

    tabs = _rope_tables(s_len, A_QK_DIM) + _rope_tables(s_len, HEAD_DIM)
    dil_bias = jnp.asarray(_dilated_log2_counts(min(ATT_BLOCK, s_len)))
    head_of = np.arange(C_WIDTH) // HEAD_DIM
    consts = {
        "bd": jnp.asarray(head_of[:, None] == head_of[None, :], BF16),
        "tri": jnp.asarray(np.tril(np.ones((RWKV_CHUNK, RWKV_CHUNK), np.float32)), BF16),
    }
    row = lambda vct: vct.reshape(1, -1)
    tile2 = lambda vct: jnp.tile(vct, LANES // HEAD_DIM).reshape(1, LANES)

    h = x.reshape(t, d_model)
    v_first = None
    for l in range(depth):
        h = _ffn_ln(h, ffn_a_gate[l].astype(BF16), ffn_a_up[l].astype(BF16), ffn_a_down[l].astype(BF16),
                    row(ln_g[l, 0]), row(ln_b[l, 0]), alpha)

        aq, ak, av, bq, bk, bv, cc = _in_proj(h, w_in[l].astype(BF16), tabs, s_len)
        shp = (bsz, s_len, A_WIDTH)

        lam_init = 0.8 - 0.6 * math.exp(-0.3 * l)
        lam_p = jnp.zeros((4, LANES), F32).at[:, :A_QK_DIM].set(
            jnp.stack([a_lam_q1[l], a_lam_k1[l], a_lam_q2[l], a_lam_k2[l]]))
        o_a = _diff_attn(aq.reshape(shp), ak.reshape(shp), av.reshape(shp), lam_p,
                         tile2(a_norm_g[l]), lam_init)
        o_b = _dil_attn(bq.reshape(shp), bk.reshape(shp), bv.reshape(shp), dil_bias, tile2(b_norm_g[l]))

        p = {"mu": row(c_mu[l]), "w0": row(c_w0[l]), "a0": row(c_a0[l]),
             "w2": _pad_rows(c_w2[l], LANES, 0), "a2": _pad_rows(c_a2[l], LANES, C_W_RANK),
             "g2": c_g2[l], "k_k": row(c_k_k[l]), "k_a": row(c_k_a[l]), "r_k": row(c_r_k[l]),
             "gn_g": row(c_gn_g[l]), "gn_b": row(c_gn_b[l])}
        cc3 = cc.reshape(bsz, s_len, C_COLS)
        if l == 0:
            o_c, v_first = _rwkv(cc3, None, p, consts)
        else:
            p["v0"] = row(c_v0[l - 1])
            p["v1"] = jnp.zeros((C_WIDTH, LANES), F32).at[:, :C_V_RANK].set(c_v1[l - 1])
            p["v2"] = _pad_rows(c_v2[l - 1], LANES, 0)
            o_c = _rwkv(cc3, v_first, p, consts)

        h = _out_proj_ln(h, o_a.reshape(t, A_WIDTH), o_b.reshape(t, B_WIDTH), o_c.reshape(t, C_WIDTH),
                         w_out[l].astype(BF16), row(ln_g[l, 1]), row(ln_b[l, 1]), alpha)
        h = _ffn_ln(h, ffn_b_gate[l].astype(BF16), ffn_b_up[l].astype(BF16), ffn_b_down[l].astype(BF16),
                    row(ln_g[l, 2]), row(ln_b[l, 2]), alpha)
    return h.reshape(bsz, s_len, d_model)
```

```python
import functools
import math

import numpy as np
import jax
import jax.numpy as jnp
from jax import lax
from jax.experimental import pallas as pl
from jax.experimental.pallas import tpu as pltpu

F32 = jnp.float32
BF16 = jnp.bfloat16

HEAD_DIM = 64
A_HEADS = 6
B_HEADS = 6
C_HEADS = 4
A_QK_DIM = 32
A_WIDTH = A_HEADS * HEAD_DIM
B_WIDTH = B_HEADS * HEAD_DIM
C_WIDTH = C_HEADS * HEAD_DIM
C_W_RANK = 64
C_A_RANK = 64
C_V_RANK = 32
C_G_RANK = 128
C_COLS = 3 * C_WIDTH + C_W_RANK + C_A_RANK + C_G_RANK
DILATED_PATTERNS = ((128, 1), (512, 4), (2048, 16))
ROPE_THETA = 10000.0
LN_EPS = 1e-5
RMS_EPS = 1e-5
C_GN_EPS = 64e-5
NEG_INF = -1e30
DECAY_RATE = math.exp(-0.5)
LOG2_E = math.log2(math.e)

LANES = 128
SUBLANES = 8
VMEM_LIMIT_BYTES = 56 * 1024 * 1024

ROW_TILE = 512
FF_CHUNK = 256
ATT_BLOCK = 512
RWKV_CHUNK = 64
RWKV_CHUNKS_PER_STEP = 4
SOLVE_BLOCK = 16

_NN = (((1,), (0,)), ((), ()))
_NT = (((1,), (1,)), ((), ()))


def _dot(a, b, dims=_NN):
    return lax.dot_general(a, b, dims, preferred_element_type=F32)


def _split2(x):
    hi = x.astype(BF16)
    lo = (x - hi.astype(F32)).astype(BF16)
    return hi, lo


def _split3(x):
    hi = x.astype(BF16)
    r = x - hi.astype(F32)
    mid = r.astype(BF16)
    lo = (r - mid.astype(F32)).astype(BF16)
    return hi, mid, lo


def _dot2l(a, b, dims=_NN):
    a_hi, a_lo = _split2(a)
    b_hi = b.astype(BF16)
    return _dot(a_lo, b_hi, dims) + _dot(a_hi, b_hi, dims)


def _dot2r(a, b, dims=_NN):
    a_hi = a.astype(BF16)
    b_hi, b_lo = _split2(b)
    return _dot(a_hi, b_lo, dims) + _dot(a_hi, b_hi, dims)


def _dot1(a, b, dims=_NN):
    return _dot(a.astype(BF16), b.astype(BF16), dims)


def _dot_sel_lhs(sel, x, dims=_NN):
    h, m, l = _split3(x)
    return _dot(sel, l, dims) + _dot(sel, m, dims) + _dot(sel, h, dims)


def _dot_sel_rhs(x, sel):
    h, l = _split2(x)
    return _dot(l, sel) + _dot(h, sel)


def _layer_norm(y, g, b):
    mu = jnp.mean(y, axis=-1, keepdims=True)
    d = y - mu
    var = jnp.mean(d * d, axis=-1, keepdims=True)
    return d * lax.rsqrt(var + LN_EPS) * g + b


def _cparams(sem):
    return pltpu.CompilerParams(dimension_semantics=sem, vmem_limit_bytes=VMEM_LIMIT_BYTES)


def _resident(shape):
    nd = len(shape)
    return pl.BlockSpec(shape, lambda *_: (0,) * nd, pipeline_mode=pl.Buffered(1))


def _resident_layer(shape, layer):
    nd = len(shape)
    return pl.BlockSpec((None,) + tuple(shape), lambda *_: (layer,) + (0,) * nd,
                        pipeline_mode=pl.Buffered(1))


def _ffn_ln_kernel(x_ref, wg_ref, wu_ref, wd_ref, g_ref, b_ref, o_ref, *, alpha, d_ff):
    x = x_ref[...]
    xb = x.astype(BF16)
    acc = jnp.zeros(x.shape, F32)
    for c0 in range(0, d_ff, FF_CHUNK):
        gate = _dot(xb, wg_ref[:, c0:c0 + FF_CHUNK])
        up = _dot(xb, wu_ref[:, c0:c0 + FF_CHUNK])
        h = (gate * jax.nn.sigmoid(gate)) * up
        acc = acc + _dot(h.astype(BF16), wd_ref[c0:c0 + FF_CHUNK, :])
    y = alpha * x + 0.5 * acc
    o_ref[...] = _layer_norm(y, g_ref[...], b_ref[...])


def _ffn_ln(x, wg, wu, wd, g, b, alpha, layer):
    t, d = x.shape
    d_ff = wg.shape[2]
    tm = min(ROW_TILE, t)
    return pl.pallas_call(
        functools.partial(_ffn_ln_kernel, alpha=alpha, d_ff=d_ff),
        out_shape=jax.ShapeDtypeStruct((t, d), F32),
        grid=(t // tm,),
        in_specs=[
            pl.BlockSpec((tm, d), lambda i: (i, 0)),
            _resident_layer((d, d_ff), layer), _resident_layer((d, d_ff), layer),
            _resident_layer((d_ff, d), layer),
            _resident((1, d)), _resident((1, d)),
        ],
        out_specs=pl.BlockSpec((tm, d), lambda i: (i, 0)),
        compiler_params=_cparams(("parallel",)),
        name="ffn_ln",
    )(x, wg, wu, wd, g, b)


def _rope_lanes(y, cos, sin_signed, group):
    half = group // 2
    lane = lax.broadcasted_iota(jnp.int32, (1, LANES), 1)
    first = (lane % group) < half
    swapped = jnp.where(first, pltpu.roll(y, LANES - half, 1), pltpu.roll(y, half, 1))
    return y * cos + swapped * sin_signed


def _in_proj_kernel(x_ref, w_ref, ca_ref, sa_ref, cb_ref, sb_ref,
                    aq_ref, ak_ref, av_ref, bq_ref, bk_ref, bv_ref, cc_ref):
    xb = x_ref[...].astype(BF16)
    ca, sa, cb, sb = ca_ref[...], sa_ref[...], cb_ref[...], sb_ref[...]
    a_scale = A_QK_DIM ** -0.5 * LOG2_E
    b_scale = HEAD_DIM ** -0.5 * LOG2_E
    plan = ((aq_ref, A_WIDTH, (ca, sa, A_QK_DIM), a_scale),
            (ak_ref, A_WIDTH, (ca, sa, A_QK_DIM), None),
            (av_ref, A_WIDTH, None, None),
            (bq_ref, B_WIDTH, (cb, sb, HEAD_DIM), b_scale),
            (bk_ref, B_WIDTH, (cb, sb, HEAD_DIM), None),
            (bv_ref, B_WIDTH, None, None))
    segs = [(ref, s0, rope, scale) for ref, width, rope, scale in plan for s0 in range(0, width, LANES)]
    wide = 2 * LANES
    for c0 in range(0, len(segs) * LANES, wide):
        y = _dot(xb, w_ref[:, c0:c0 + wide])
        for j in range(wide // LANES):
            ref, s0, rope, scale = segs[c0 // LANES + j]
            z = y[:, j * LANES:(j + 1) * LANES]
            if rope is not None:
                z = _rope_lanes(z, *rope)
            if scale is not None:
                z = z * scale
            ref[:, s0:s0 + LANES] = z.astype(ref.dtype)
    col = len(segs) * LANES
    for s0 in range(0, C_COLS, wide):
        cc_ref[:, s0:s0 + wide] = _dot(xb, w_ref[:, col + s0:col + s0 + wide])


def _in_proj(x, w_in, tabs, s_len, layer):
    t, d = x.shape
    n_in = w_in.shape[2]
    tm = min(ROW_TILE, s_len)
    spb = s_len // tm
    row = lambda i: (i, 0)
    tab = pl.BlockSpec((tm, LANES), lambda i: (i % spb, 0))
    qkv = jax.ShapeDtypeStruct((t, A_WIDTH), BF16)
    return pl.pallas_call(
        _in_proj_kernel,
        out_shape=(qkv,) * 6 + (jax.ShapeDtypeStruct((t, C_COLS), F32),),
        grid=(t // tm,),
        in_specs=[pl.BlockSpec((tm, d), row), _resident_layer((d, n_in), layer), tab, tab, tab, tab],
        out_specs=tuple(pl.BlockSpec((tm, A_WIDTH), row) for _ in range(6))
        + (pl.BlockSpec((tm, C_COLS), row),),
        compiler_params=_cparams(("parallel",)),
        name="in_proj",
    )(x, w_in, *tabs)


def _rope_tables(s_len, group):
    half = group // 2
    inv = ROPE_THETA ** (-jnp.arange(0, group, 2, dtype=F32) / group)
    ang = jnp.arange(s_len, dtype=F32)[:, None] * inv[None, :]
    lane = np.arange(LANES)
    idx = lane % half
    sign = np.where((lane % group) < half, -1.0, 1.0).astype(np.float32)
    cos = jnp.cos(ang)[:, idx]
    sin = jnp.sin(ang)[:, idx] * sign[None, :]
    return cos, sin


def _attn_two_pass(q_stack_ref, k_ref, v_ref, s_ref, mx_ref, acc_ref, blk, n_comb):
    lane = lax.broadcasted_iota(jnp.int32, (1, LANES), 1)
    low = lane < HEAD_DIM
    mx_ref[...] = jnp.full(mx_ref.shape, NEG_INF, F32)
    acc_ref[...] = jnp.zeros(acc_ref.shape, F32)

    def pass_a(kb, bias):
        start = pl.multiple_of(kb * blk, blk)
        s = _dot(q_stack_ref[...], k_ref[0, pl.ds(start, blk), :], _NT)
        if bias is not None:
            s = s + bias
        s_ref[kb] = s
        m = mx_ref[...]
        for j in range(blk // LANES):
            m = jnp.maximum(m, s[:, j * LANES:(j + 1) * LANES])
        mx_ref[...] = m

    def pass_b(kb, carry):
        start = pl.multiple_of(kb * blk, blk)
        v = v_ref[0, pl.ds(start, blk), :]
        one = jnp.ones_like(v)
        vx = (jnp.where(low, v, one), jnp.where(low, one, v))
        s = s_ref[kb]
        m = mx_ref[...]
        p = jnp.concatenate([jnp.exp2(s[:, j * LANES:(j + 1) * LANES] - m)
                             for j in range(blk // LANES)], axis=1).astype(BF16)
        per_head = n_comb // 2
        for c in range(n_comb):
            acc_ref[c] = acc_ref[c] + _dot(p[c * blk:(c + 1) * blk], vx[c // per_head])
        return carry

    return pass_a, pass_b


def _finish_max(mx_ref):
    mx_ref[...] = jnp.broadcast_to(jnp.max(mx_ref[...], axis=1, keepdims=True), mx_ref.shape)


def _normalized(acc):
    return acc / pltpu.roll(acc, HEAD_DIM, 1)


def _head_rms(o, g):
    lane = lax.broadcasted_iota(jnp.int32, (1, LANES), 1)
    low = lane < HEAD_DIM
    sq = o * o
    s0 = jnp.sum(jnp.where(low, sq, 0.0), axis=1, keepdims=True)
    s1 = jnp.sum(jnp.where(low, 0.0, sq), axis=1, keepdims=True)
    ms = jnp.where(low, s0, s1) * (1.0 / HEAD_DIM)
    return o * lax.rsqrt(ms + RMS_EPS) * g


def _diff_attn_kernel(lam_ref, g_ref, q_ref, k_ref, v_ref, o_ref, q4_ref, s_ref, mx_ref, acc_ref,
                      *, blk, lam_init):
    qi = pl.program_id(2)
    lane = lax.broadcasted_iota(jnp.int32, (1, LANES), 1)
    q = q_ref[0]
    for c in range(4):
        q4_ref[c * blk:(c + 1) * blk, :] = jnp.where((lane // A_QK_DIM) == c, q, jnp.zeros_like(q))

    pass_a, pass_b = _attn_two_pass(q4_ref, k_ref, v_ref, s_ref, mx_ref, acc_ref, blk, 4)

    def body_a(kb, carry):
        pass_a(kb, None)
        return carry

    lax.fori_loop(0, qi, body_a, 0)
    r = lax.broadcasted_iota(jnp.int32, (4 * blk, blk), 0) % blk
    c_ = lax.broadcasted_iota(jnp.int32, (4 * blk, blk), 1)
    pass_a(qi, jnp.where(c_ <= r, 0.0, NEG_INF))
    _finish_max(mx_ref)
    lax.fori_loop(0, qi + 1, pass_b, 0)

    lp = lam_ref[...]
    lam = (jnp.exp(jnp.sum(lp[0:1] * lp[1:2], axis=1, keepdims=True))
           - jnp.exp(jnp.sum(lp[2:3] * lp[3:4], axis=1, keepdims=True)) + lam_init)
    heads = [_normalized(acc_ref[2 * h]) - lam * _normalized(acc_ref[2 * h + 1]) for h in range(2)]
    o = jnp.where(lane < HEAD_DIM, heads[0], heads[1])
    o_ref[0] = (_head_rms(o, g_ref[...]) * (1.0 - lam_init)).astype(o_ref.dtype)


def _attn_call(kernel_fn, name, n_comb, consts, const_specs, q, k, v, blk):
    b, s, w = q.shape
    pairs = w // LANES
    return pl.pallas_call(
        kernel_fn,
        out_shape=jax.ShapeDtypeStruct((b, s, w), BF16),
        grid=(b, pairs, s // blk),
        in_specs=const_specs + [
            pl.BlockSpec((1, blk, LANES), lambda bi, p, i: (bi, i, p)),
            pl.BlockSpec((1, s, LANES), lambda bi, p, i: (bi, 0, p)),
            pl.BlockSpec((1, s, LANES), lambda bi, p, i: (bi, 0, p)),
        ],
        out_specs=pl.BlockSpec((1, blk, LANES), lambda bi, p, i: (bi, i, p)),
        scratch_shapes=[pltpu.VMEM((n_comb * blk, LANES), BF16),
                        pltpu.VMEM((s // blk, n_comb * blk, blk), F32),
                        pltpu.VMEM((n_comb * blk, LANES), F32),
                        pltpu.VMEM((n_comb, blk, LANES), F32)],
        compiler_params=_cparams(("parallel", "parallel", "arbitrary")),
        name=name,
    )(*consts, q, k, v)


def _diff_attn(q, k, v, lam_p, g, lam_init):
    blk = min(ATT_BLOCK, q.shape[1])
    return _attn_call(functools.partial(_diff_attn_kernel, blk=blk, lam_init=lam_init), "diff_attn", 4,
                      [lam_p, g], [_resident((4, LANES)), _resident((1, LANES))], q, k, v, blk)


def _dilated_log2_counts(blk):
    far = max(w for w, _ in DILATED_PATTERNS if w < max(p[0] for p in DILATED_PATTERNS))
    n_off = far // blk + 2
    i = np.arange(blk)[:, None]
    j = np.arange(blk)[None, :]
    cnt = np.zeros((n_off, blk, blk), np.float64)
    for o in range(n_off):
        delta = o * blk + i - j
        for window, dil in DILATED_PATTERNS:
            cnt[o] += ((delta >= 0) & (delta <= window) & (delta % dil == 0))
    return np.where(cnt > 0, np.log2(np.maximum(cnt, 1.0)), NEG_INF).astype(np.float32)


def _dil_attn_kernel(bias_ref, g_ref, q_ref, k_ref, v_ref, o_ref, q2_ref, s_ref, mx_ref, acc_ref,
                     *, blk, n_off):
    qi = pl.program_id(2)
    lane = lax.broadcasted_iota(jnp.int32, (1, LANES), 1)
    low = lane < HEAD_DIM
    q = q_ref[0]
    zero = jnp.zeros_like(q)
    q2_ref[0:blk, :] = jnp.where(low, q, zero)
    q2_ref[blk:2 * blk, :] = jnp.where(low, zero, q)

    pass_a, pass_b = _attn_two_pass(q2_ref, k_ref, v_ref, s_ref, mx_ref, acc_ref, blk, 2)

    def body_a(kb, carry):
        bias = bias_ref[jnp.minimum(qi - kb, n_off - 1)]
        pass_a(kb, jnp.concatenate([bias, bias], axis=0))
        return carry

    lax.fori_loop(0, qi + 1, body_a, 0)
    _finish_max(mx_ref)
    lax.fori_loop(0, qi + 1, pass_b, 0)

    o = jnp.where(low, _normalized(acc_ref[0]), _normalized(acc_ref[1]))
    o_ref[0] = _head_rms(o, g_ref[...]).astype(o_ref.dtype)


def _dil_attn(q, k, v, bias, g):
    blk = bias.shape[1]
    return _attn_call(functools.partial(_dil_attn_kernel, blk=blk, n_off=bias.shape[0]), "dil_attn", 2,
                      [bias, g], [_resident(bias.shape), _resident((1, LANES))], q, k, v, blk)


def _solve_unit_lower(a_list, x_list):
    n = len(x_list)
    c = x_list[0].shape[0]
    nb = c // SOLVE_BLOCK
    sub = SOLVE_BLOCK // SUBLANES
    a_split = [_split2(a[:, 0:c]) for a in a_list]
    xs = [[x[SOLVE_BLOCK * i:SOLVE_BLOCK * (i + 1)] for i in range(nb)] for x in x_list]
    zeros = jnp.zeros((SOLVE_BLOCK, LANES), F32)
    for b in range(nb):
        base = b * SOLVE_BLOCK
        rows = [[xs[u][b][SUBLANES * i:SUBLANES * (i + 1)] for i in range(sub)] for u in range(n)]
        for t in range(SOLVE_BLOCK - 1):
            bi = t // SUBLANES
            for u in range(n):
                xt = rows[u][bi][t % SUBLANES:t % SUBLANES + 1, :]
                for i in range(bi, sub):
                    r0 = base + SUBLANES * i
                    a_col = a_list[u][r0:r0 + SUBLANES, base + t:base + t + 1]
                    rows[u][i] = rows[u][i] - a_col * xt
        for u in range(n):
            xs[u][b] = jnp.concatenate(rows[u], axis=0)
        if b + 1 < nb:
            below = slice(base + SOLVE_BLOCK, c)
            for u in range(n):
                a_hi, a_lo = a_split[u]
                x_hi, x_lo = _split2(jnp.concatenate(
                    [zeros] * b + [xs[u][b]] + [zeros] * (nb - 1 - b), axis=0))
                upd = _dot(a_lo[below], x_hi) + _dot(a_hi[below], x_lo) + _dot(a_hi[below], x_hi)
                for j in range(b + 1, nb):
                    xs[u][j] = xs[u][j] - upd[(j - b - 1) * SOLVE_BLOCK:(j - b) * SOLVE_BLOCK]
    return [jnp.concatenate(x, axis=0) for x in xs]


def _to_slab(col_block, half, lower):
    lane = lax.broadcasted_iota(jnp.int32, (1, LANES), 1)
    want_low = lower
    have_low = half == 0
    src = col_block if want_low == have_low else pltpu.roll(col_block, HEAD_DIM, 1)
    keep = (lane < HEAD_DIM) if want_low else (lane >= HEAD_DIM)
    return jnp.where(keep, src, 0.0)


def _rwkv_kernel(*refs, n_chunks, has_vres):
    if has_vres:
        (cc_ref, vf_ref, mu_ref, w0_ref, w2_ref, a0_ref, a2_ref, g2_ref, kk_ref, ka_ref, rk_ref,
         gng_ref, gnb_ref, v0_ref, v1_ref, v2_ref, bd_ref, tri_ref,
         o_ref, carry_ref, hs_ref) = refs
    else:
        (cc_ref, mu_ref, w0_ref, w2_ref, a0_ref, a2_ref, g2_ref, kk_ref, ka_ref, rk_ref,
         gng_ref, gnb_ref, bd_ref, tri_ref,
         o_ref, vf_out_ref, carry_ref, hs_ref) = refs
    cw = C_WIDTH
    cl = RWKV_CHUNK

    @pl.when(pl.program_id(1) == 0)
    def _():
        carry_ref[...] = jnp.zeros(carry_ref.shape, F32)
        hs_ref[...] = jnp.zeros(hs_ref.shape, F32)

    c_raw = cc_ref[0]
    tb = c_raw.shape[0]
    row = lax.broadcasted_iota(jnp.int32, (tb, 1), 0)
    prev = jnp.where(row == 0, carry_ref[...], pltpu.roll(c_raw, 1, 0))
    carry_ref[...] = c_raw[tb - 1:tb, :]
    c = c_raw + (prev - c_raw) * mu_ref[...]

    r = c[:, 0:cw]
    k = c[:, cw:2 * cw]
    v = c[:, 2 * cw:3 * cw]
    xwa = c[:, 3 * cw:3 * cw + LANES]
    xg = c[:, 3 * cw + LANES:3 * cw + 2 * LANES]
    if has_vres:
        mix = jax.nn.sigmoid(v0_ref[...] + _dot2r(_dot2r(v, v1_ref[...]), v2_ref[...]))
        v = v + (vf_ref[0] - v) * mix
    else:
        vf_out_ref[0] = v
    bd = bd_ref[...]
    lw = -DECAY_RATE * jax.nn.sigmoid(w0_ref[...] + _dot1(jnp.tanh(xwa), w2_ref[...]))
    a = jax.nn.sigmoid(a0_ref[...] + _dot1(xwa, a2_ref[...]))
    gate = _dot2r(jax.nn.sigmoid(xg), g2_ref[...])
    kkv = k * kk_ref[...]
    kappa = kkv / jnp.maximum(jnp.sqrt(_dot_sel_rhs(kkv * kkv, bd)), 1e-12)
    k2 = k * (1.0 + (a - 1.0) * ka_ref[...])
    bonus = _dot_sel_rhs(r * k2 * rk_ref[...], bd)
    beta = a * kappa

    lane = lax.broadcasted_iota(jnp.int32, (1, LANES), 1)
    low = lane < HEAD_DIM
    lane2 = lax.broadcasted_iota(jnp.int32, (1, cw), 1)
    rr = lax.broadcasted_iota(jnp.int32, (2 * cl, LANES), 0)
    cidx = lax.broadcasted_iota(jnp.int32, (2 * cl, LANES), 1) % cl
    tri_mask = cidx < jnp.where(rr < cl, rr, rr - cl + 1)
    eye = (lax.broadcasted_iota(jnp.int32, (cl, LANES), 0)
           == lax.broadcasted_iota(jnp.int32, (cl, LANES), 1))
    tri = tri_ref[...]

    zeros = jnp.zeros((cl, LANES), F32)
    units = []
    for ci in range(n_chunks):
        sl = slice(ci * cl, (ci + 1) * cl)
        lwc = lw[sl]
        cum = _dot_sel_lhs(tri, lwc)
        cum_end = cum[cl - 1:cl, :]
        rt = r[sl] * jnp.exp(cum)
        kt = kappa[sl] * jnp.exp(cum - lwc)
        e_inv = jnp.exp(-cum)
        e_end = jnp.exp(cum_end - cum)
        bh = beta[sl] * e_end
        kh = k2[sl] * e_end
        p_end = jnp.exp(cum_end)
        vc = v[sl]
        right = jnp.concatenate([beta[sl] * e_inv, k2[sl] * e_inv], axis=0)
        for h in range(C_HEADS):
            cb, half = divmod(h, 2)
            cs = slice(cb * LANES, (cb + 1) * LANES)
            hm = (lane2 // HEAD_DIM) == h
            units.append(dict(
                ci=ci, h=h, right=right,
                left=jnp.concatenate([jnp.where(hm, kt, 0.0), jnp.where(hm, rt, 0.0)], axis=0),
                vw=_to_slab(vc[:, cs], half, lower=False),
                kw=_to_slab(kt[:, cs], half, lower=True),
                rw=_to_slab(rt[:, cs], half, lower=True),
                bkw=jnp.concatenate([_to_slab(bh[:, cs], half, True), _to_slab(kh[:, cs], half, True)], axis=0),
                pe=_to_slab(jnp.broadcast_to(p_end[:, cs], (cl, LANES)), half, True)))
    for u in units:
        u["m"] = jnp.where(tri_mask, _dot2l(u["left"], u["right"], _NT), 0.0)
    for u in units:
        u["rhs"] = u["kw"] + _dot1(u["m"][0:cl], jnp.concatenate([zeros, u["vw"]], axis=0))
    xs = _solve_unit_lower([u["m"][0:cl] for u in units], [u["rhs"] for u in units])
    for u, x in zip(units, xs):
        u["st"] = jnp.concatenate([-x, u["vw"]], axis=0)
    for u in units:
        u["qy"] = u["rw"] + _dot2r(u["m"][cl:2 * cl], u["st"])
    for u in units:
        u["gh"] = jnp.where(eye, u["pe"], 0.0) + _dot1(u["bkw"].T[0:cl], u["st"])
    for u in units:
        hs = hs_ref[u["h"]]
        u["y"] = u["qy"] + _dot1(u["qy"], hs)
        hs_ref[u["h"], 0:cl, :] = jnp.where(low, 0.0, u["gh"] + _dot2r(u["gh"], hs))
    y_chunks = []
    for ci in range(n_chunks):
        yh = [u["y"] for u in units if u["ci"] == ci]
        cols = [jnp.where(low, pltpu.roll(yh[2 * cb], HEAD_DIM, 1), yh[2 * cb + 1])
                for cb in range(C_HEADS // 2)]
        y_chunks.append(jnp.concatenate(cols, axis=1))
    y = jnp.concatenate(y_chunks, axis=0) if n_chunks > 1 else y_chunks[0]

    inv = 1.0 / HEAD_DIM
    mean = _dot_sel_rhs(y, bd) * inv
    d = y - mean
    var = _dot_sel_rhs(d * d, bd) * inv
    yn = d * lax.rsqrt(var + C_GN_EPS) * gng_ref[...] + gnb_ref[...]
    out = (yn + bonus * v) * gate
    o_ref[0] = out.astype(o_ref.dtype)


def _rwkv(cc, v_first, p, consts):
    b, s, _ = cc.shape
    n_chunks = RWKV_CHUNKS_PER_STEP
    tb = n_chunks * RWKV_CHUNK
    has_vres = v_first is not None
    blk_cc = pl.BlockSpec((1, tb, C_COLS), lambda bi, j: (bi, j, 0))
    blk_cw = pl.BlockSpec((1, tb, C_WIDTH), lambda bi, j: (bi, j, 0))
    vec = lambda n: _resident((1, n))
    params = [p["mu"], p["w0"], p["w2"], p["a0"], p["a2"], p["g2"], p["k_k"], p["k_a"], p["r_k"],
              p["gn_g"], p["gn_b"]]
    specs = [vec(C_COLS), vec(C_WIDTH), _resident((LANES, C_WIDTH)), vec(C_WIDTH),
             _resident((LANES, C_WIDTH)), _resident((LANES, C_WIDTH)), vec(C_WIDTH), vec(C_WIDTH),
             vec(C_WIDTH), vec(C_WIDTH), vec(C_WIDTH)]
    if has_vres:
        params += [p["v0"], p["v1"], p["v2"]]
        specs += [vec(C_WIDTH), _resident((C_WIDTH, LANES)), _resident((LANES, C_WIDTH))]
    params += [consts["bd"], consts["tri"]]
    specs += [_resident((C_WIDTH, C_WIDTH)), _resident((RWKV_CHUNK, RWKV_CHUNK))]
    o_shape = jax.ShapeDtypeStruct((b, s, C_WIDTH), BF16)
    if has_vres:
        args = [cc, v_first] + params
        in_specs = [blk_cc, blk_cw] + specs
        out_shape, out_specs = o_shape, blk_cw
    else:
        args = [cc] + params
        in_specs = [blk_cc] + specs
        out_shape = (o_shape, jax.ShapeDtypeStruct((b, s, C_WIDTH), F32))
        out_specs = (blk_cw, blk_cw)
    return pl.pallas_call(
        functools.partial(_rwkv_kernel, n_chunks=n_chunks, has_vres=has_vres),
        out_shape=out_shape,
        grid=(b, s // tb),
        in_specs=in_specs,
        out_specs=out_specs,
        scratch_shapes=[pltpu.VMEM((1, C_COLS), F32),
                        pltpu.VMEM((C_HEADS, 2 * RWKV_CHUNK, LANES), F32)],
        compiler_params=_cparams(("parallel", "arbitrary")),
        name="rwkv_vres" if has_vres else "rwkv_first",
    )(*args)


def _out_proj_ln_kernel(x_ref, oa_ref, ob_ref, oc_ref, w_ref, g_ref, b_ref, o_ref, *, alpha):
    mix = (_dot(oa_ref[...], w_ref[0:A_WIDTH, :])
           + _dot(ob_ref[...], w_ref[A_WIDTH:A_WIDTH + B_WIDTH, :])
           + _dot(oc_ref[...], w_ref[A_WIDTH + B_WIDTH:, :]))
    o_ref[...] = _layer_norm(alpha * x_ref[...] + mix, g_ref[...], b_ref[...])


def _out_proj_ln(x, oa, ob, oc, w_out, g, b, alpha, layer):
    t, d = x.shape
    tm = min(ROW_TILE, t)
    row = lambda i: (i, 0)
    return pl.pallas_call(
        functools.partial(_out_proj_ln_kernel, alpha=alpha),
        out_shape=jax.ShapeDtypeStruct((t, d), F32),
        grid=(t // tm,),
        in_specs=[pl.BlockSpec((tm, d), row), pl.BlockSpec((tm, A_WIDTH), row),
                  pl.BlockSpec((tm, B_WIDTH), row), pl.BlockSpec((tm, C_WIDTH), row),
                  _resident_layer(w_out.shape[1:], layer), _resident((1, d)), _resident((1, d))],
        out_specs=pl.BlockSpec((tm, d), row),
        compiler_params=_cparams(("parallel",)),
        name="out_proj_ln",
    )(x, oa, ob, oc, w_out, g, b)


def _pad_rows(w, total, offset):
    return jnp.zeros((total, w.shape[1]), w.dtype).at[offset:offset + w.shape[0]].set(w)


def kernel(x, ffn_a_gate, ffn_a_up, ffn_a_down, ffn_b_gate, ffn_b_up, ffn_b_down, ln_g, ln_b, w_in, w_out, a_lam_q1, a_lam_k1, a_lam_q2, a_lam_k2, a_norm_g, b_norm_g, c_mu, c_w0, c_w2, c_a0, c_a2, c_g2, c_k_k, c_k_a, c_r_k, c_gn_g, c_gn_b, c_v0, c_v1, c_v2):
    bsz, s_len, d_model = x.shape
    depth = w_in.shape[0]
    alpha = (2 * depth) ** 0.25
    t = bsz * s_len

    tabs = _rope_tables(s_len, A_QK_DIM) + _rope_tables(s_len, HEAD_DIM)
    dil_bias = jnp.asarray(_dilated_log2_counts(min(ATT_BLOCK, s_len)))
    head_of = np.arange(C_WIDTH) // HEAD_DIM
    consts = {
        "bd": jnp.asarray(head_of[:, None] == head_of[None, :], BF16),
        "tri": jnp.asarray(np.tril(np.ones((RWKV_CHUNK, RWKV_CHUNK), np.float32)), BF16),
    }
    row = lambda vct: vct.reshape(1, -1)
    tile2 = lambda vct: jnp.tile(vct, LANES // HEAD_DIM).reshape(1, LANES)

    ffn_a = (ffn_a_gate.astype(BF16), ffn_a_up.astype(BF16), ffn_a_down.astype(BF16))
    ffn_b = (ffn_b_gate.astype(BF16), ffn_b_up.astype(BF16), ffn_b_down.astype(BF16))
    w_in_b = w_in.astype(BF16)
    w_out_b = w_out.astype(BF16)

    h = x.reshape(t, d_model)
    v_first = None
    for l in range(depth):
        h = _ffn_ln(h, *ffn_a, row(ln_g[l, 0]), row(ln_b[l, 0]), alpha, l)

        aq, ak, av, bq, bk, bv, cc = _in_proj(h, w_in_b, tabs, s_len, l)
        shp = (bsz, s_len, A_WIDTH)

        lam_init = 0.8 - 0.6 * math.exp(-0.3 * l)
        lam_p = jnp.zeros((4, LANES), F32).at[:, :A_QK_DIM].set(
            jnp.stack([a_lam_q1[l], a_lam_k1[l], a_lam_q2[l], a_lam_k2[l]]))
        o_a = _diff_attn(aq.reshape(shp), ak.reshape(shp), av.reshape(shp), lam_p,
                         tile2(a_norm_g[l]), lam_init)
        o_b = _dil_attn(bq.reshape(shp), bk.reshape(shp), bv.reshape(shp), dil_bias, tile2(b_norm_g[l]))

        p = {"mu": row(c_mu[l]), "w0": row(c_w0[l]), "a0": row(c_a0[l]),
             "w2": _pad_rows(c_w2[l], LANES, 0), "a2": _pad_rows(c_a2[l], LANES, C_W_RANK),
             "g2": c_g2[l], "k_k": row(c_k_k[l]), "k_a": row(c_k_a[l]), "r_k": row(c_r_k[l]),
             "gn_g": row(c_gn_g[l]), "gn_b": row(c_gn_b[l])}
        cc3 = cc.reshape(bsz, s_len, C_COLS)
        if l == 0:
            o_c, v_first = _rwkv(cc3, None, p, consts)
        else:
            p["v0"] = row(c_v0[l - 1])
            p["v1"] = jnp.zeros((C_WIDTH, LANES), F32).at[:, :C_V_RANK].set(c_v1[l - 1])
            p["v2"] = _pad_rows(c_v2[l - 1], LANES, 0)
            o_c = _rwkv(cc3, v_first, p, consts)

        h = _out_proj_ln(h, o_a.reshape(t, A_WIDTH), o_b.reshape(t, B_WIDTH), o_c.reshape(t, C_WIDTH),
                         w_out_b, row(ln_g[l, 1]), row(ln_b[l, 1]), alpha, l)
        h = _ffn_ln(h, *ffn_b, row(ln_g[l, 2]), row(ln_b[l, 2]), alpha, l)
    return h.reshape(bsz, s_len, d_model)
```

```python
import functools
import math

import numpy as np
import jax
import jax.numpy as jnp
from jax import lax
from jax.experimental import pallas as pl
from jax.experimental.pallas import tpu as pltpu

F32 = jnp.float32
BF16 = jnp.bfloat16

HEAD_DIM = 64
A_HEADS = 6
B_HEADS = 6
C_HEADS = 4
A_QK_DIM = 32
A_WIDTH = A_HEADS * HEAD_DIM
B_WIDTH = B_HEADS * HEAD_DIM
C_WIDTH = C_HEADS * HEAD_DIM
C_W_RANK = 64
C_A_RANK = 64
C_V_RANK = 32
C_G_RANK = 128
C_COLS = 3 * C_WIDTH + C_W_RANK + C_A_RANK + C_G_RANK
DILATED_PATTERNS = ((128, 1), (512, 4), (2048, 16))
ROPE_THETA = 10000.0
LN_EPS = 1e-5
RMS_EPS = 1e-5
C_GN_EPS = 64e-5
NEG_INF = -1e30
DECAY_RATE = math.exp(-0.5)
LOG2_E = math.log2(math.e)

LANES = 128
SUBLANES = 8
VMEM_LIMIT_BYTES = 56 * 1024 * 1024

ROW_TILE = 512
FF_CHUNK = 256
ATT_BLOCK = 512
RWKV_CHUNK = 64
RWKV_CHUNKS_PER_STEP = 4
SOLVE_BLOCK = 16

_NN = (((1,), (0,)), ((), ()))
_NT = (((1,), (1,)), ((), ()))


def _dot(a, b, dims=_NN):
    return lax.dot_general(a, b, dims, preferred_element_type=F32)


def _split2(x):
    hi = x.astype(BF16)
    lo = (x - hi.astype(F32)).astype(BF16)
    return hi, lo


def _split3(x):
    hi = x.astype(BF16)
    r = x - hi.astype(F32)
    mid = r.astype(BF16)
    lo = (r - mid.astype(F32)).astype(BF16)
    return hi, mid, lo


def _dot2l(a, b, dims=_NN):
    a_hi, a_lo = _split2(a)
    b_hi = b.astype(BF16)
    return _dot(a_lo, b_hi, dims) + _dot(a_hi, b_hi, dims)


def _dot2r(a, b, dims=_NN):
    a_hi = a.astype(BF16)
    b_hi, b_lo = _split2(b)
    return _dot(a_hi, b_lo, dims) + _dot(a_hi, b_hi, dims)


def _dot1(a, b, dims=_NN):
    return _dot(a.astype(BF16), b.astype(BF16), dims)


def _dot_sel_lhs(sel, x, dims=_NN):
    h, m, l = _split3(x)
    return _dot(sel, l, dims) + _dot(sel, m, dims) + _dot(sel, h, dims)


def _dot_sel_rhs(x, sel):
    h, l = _split2(x)
    return _dot(l, sel) + _dot(h, sel)


def _layer_norm(y, g, b):
    mu = jnp.mean(y, axis=-1, keepdims=True)
    d = y - mu
    var = jnp.mean(d * d, axis=-1, keepdims=True)
    return d * lax.rsqrt(var + LN_EPS) * g + b


def _cparams(sem):
    return pltpu.CompilerParams(dimension_semantics=sem, vmem_limit_bytes=VMEM_LIMIT_BYTES)


def _resident(shape):
    nd = len(shape)
    return pl.BlockSpec(shape, lambda *_: (0,) * nd, pipeline_mode=pl.Buffered(1))


def _resident_layer(shape, layer):
    nd = len(shape)
    return pl.BlockSpec((None,) + tuple(shape), lambda *_: (layer,) + (0,) * nd,
                        pipeline_mode=pl.Buffered(1))


def _ffn_ln_kernel(x_ref, wg_ref, wu_ref, wd_ref, g_ref, b_ref, o_ref, *, alpha, d_ff):
    x = x_ref[...]
    xb = x.astype(BF16)
    acc = jnp.zeros(x.shape, F32)
    for c0 in range(0, d_ff, FF_CHUNK):
        gate = _dot(xb, wg_ref[:, c0:c0 + FF_CHUNK])
        up = _dot(xb, wu_ref[:, c0:c0 + FF_CHUNK])
        h = (gate * jax.nn.sigmoid(gate)) * up
        acc = acc + _dot(h.astype(BF16), wd_ref[c0:c0 + FF_CHUNK, :])
    y = alpha * x + 0.5 * acc
    o_ref[...] = _layer_norm(y, g_ref[...], b_ref[...])


def _ffn_ln(x, wg, wu, wd, g, b, alpha, layer):
    t, d = x.shape
    d_ff = wg.shape[2]
    tm = min(ROW_TILE, t)
    return pl.pallas_call(
        functools.partial(_ffn_ln_kernel, alpha=alpha, d_ff=d_ff),
        out_shape=jax.ShapeDtypeStruct((t, d), F32),
        grid=(t // tm,),
        in_specs=[
            pl.BlockSpec((tm, d), lambda i: (i, 0)),
            _resident_layer((d, d_ff), layer), _resident_layer((d, d_ff), layer),
            _resident_layer((d_ff, d), layer),
            _resident((1, d)), _resident((1, d)),
        ],
        out_specs=pl.BlockSpec((tm, d), lambda i: (i, 0)),
        compiler_params=_cparams(("parallel",)),
        name="ffn_ln",
    )(x, wg, wu, wd, g, b)


def _rope_lanes(y, cos, sin_signed, group):
    half = group // 2
    lane = lax.broadcasted_iota(jnp.int32, (1, LANES), 1)
    first = (lane % group) < half
    swapped = jnp.where(first, pltpu.roll(y, LANES - half, 1), pltpu.roll(y, half, 1))
    return y * cos + swapped * sin_signed


def _in_proj_kernel(x_ref, w_ref, ca_ref, sa_ref, cb_ref, sb_ref,
                    aq_ref, ak_ref, av_ref, bq_ref, bk_ref, bv_ref, cc_ref):
    xb = x_ref[...].astype(BF16)
    ca, sa, cb, sb = ca_ref[...], sa_ref[...], cb_ref[...], sb_ref[...]
    a_scale = A_QK_DIM ** -0.5 * LOG2_E
    b_scale = HEAD_DIM ** -0.5 * LOG2_E
    plan = ((aq_ref, A_WIDTH, (ca, sa, A_QK_DIM), a_scale),
            (ak_ref, A_WIDTH, (ca, sa, A_QK_DIM), None),
            (av_ref, A_WIDTH, None, None),
            (bq_ref, B_WIDTH, (cb, sb, HEAD_DIM), b_scale),
            (bk_ref, B_WIDTH, (cb, sb, HEAD_DIM), None),
            (bv_ref, B_WIDTH, None, None))
    segs = [(ref, s0, rope, scale) for ref, width, rope, scale in plan for s0 in range(0, width, LANES)]
    wide = 2 * LANES
    for c0 in range(0, len(segs) * LANES, wide):
        y = _dot(xb, w_ref[:, c0:c0 + wide])
        for j in range(wide // LANES):
            ref, s0, rope, scale = segs[c0 // LANES + j]
            z = y[:, j * LANES:(j + 1) * LANES]
            if rope is not None:
                z = _rope_lanes(z, *rope)
            if scale is not None:
                z = z * scale
            ref[:, s0:s0 + LANES] = z.astype(ref.dtype)
    col = len(segs) * LANES
    for s0 in range(0, C_COLS, wide):
        cc_ref[:, s0:s0 + wide] = _dot(xb, w_ref[:, col + s0:col + s0 + wide])


def _in_proj(x, w_in, tabs, s_len, layer):
    t, d = x.shape
    n_in = w_in.shape[2]
    tm = min(ROW_TILE, s_len)
    spb = s_len // tm
    row = lambda i: (i, 0)
    tab = pl.BlockSpec((tm, LANES), lambda i: (i % spb, 0))
    qkv = jax.ShapeDtypeStruct((t, A_WIDTH), BF16)
    return pl.pallas_call(
        _in_proj_kernel,
        out_shape=(qkv,) * 6 + (jax.ShapeDtypeStruct((t, C_COLS), F32),),
        grid=(t // tm,),
        in_specs=[pl.BlockSpec((tm, d), row), _resident_layer((d, n_in), layer), tab, tab, tab, tab],
        out_specs=tuple(pl.BlockSpec((tm, A_WIDTH), row) for _ in range(6))
        + (pl.BlockSpec((tm, C_COLS), row),),
        compiler_params=_cparams(("parallel",)),
        name="in_proj",
    )(x, w_in, *tabs)


def _rope_tables(s_len, group):
    half = group // 2
    inv = ROPE_THETA ** (-jnp.arange(0, group, 2, dtype=F32) / group)
    ang = jnp.arange(s_len, dtype=F32)[:, None] * inv[None, :]
    lane = np.arange(LANES)
    idx = lane % half
    sign = np.where((lane % group) < half, -1.0, 1.0).astype(np.float32)
    cos = jnp.cos(ang)[:, idx]
    sin = jnp.sin(ang)[:, idx] * sign[None, :]
    return cos, sin


def _stack_rows(parts, n_comb):
    return jnp.concatenate([p for p in parts for _ in range(n_comb)], axis=0)


def _attn_two_pass(q_stack_ref, k_ref, v_ref, s_ref, mx_ref, acc_ref, qi, blk, n_comb,
                   off_bias, diag_bias):
    half = blk // 2
    rows_half = n_comb * half
    per_head = n_comb // 2
    lane = lax.broadcasted_iota(jnp.int32, (1, LANES), 1)
    low = lane < HEAD_DIM
    mx_ref[...] = jnp.full(mx_ref.shape, NEG_INF, F32)
    acc_ref[...] = jnp.zeros(acc_ref.shape, F32)

    def k_at(kb, lo, size):
        return k_ref[0, pl.ds(pl.multiple_of(kb * blk + lo, half), size), :]

    def vx_at(kb, lo, size):
        v = v_ref[0, pl.ds(pl.multiple_of(kb * blk + lo, half), size), :]
        one = jnp.ones_like(v)
        return (jnp.where(low, v, one), jnp.where(low, one, v))

    def lane_max(m, s):
        for j in range(s.shape[1] // LANES):
            m = jnp.maximum(m, s[:, j * LANES:(j + 1) * LANES])
        return m

    def probs(s, m):
        return jnp.concatenate([jnp.exp2(s[:, j * LANES:(j + 1) * LANES] - m)
                                for j in range(s.shape[1] // LANES)], axis=1).astype(BF16)

    def accumulate(g, p, vx):
        for c in range(n_comb):
            acc_ref[g, c] = acc_ref[g, c] + _dot(p[c * half:(c + 1) * half], vx[c // per_head])

    def pass_a(kb):
        s = _dot(q_stack_ref[...], k_at(kb, 0, blk), _NT)
        bias = off_bias(kb)
        if bias is not None:
            s = s + _stack_rows([bias[0:half], bias[half:blk]], n_comb)
        s_ref[kb] = s
        mx_ref[...] = lane_max(mx_ref[...], s)

    def pass_b(kb):
        p = probs(s_ref[kb], mx_ref[...])
        vx = vx_at(kb, 0, blk)
        for g in range(2):
            accumulate(g, p[g * rows_half:(g + 1) * rows_half], vx)

    def over_off_diagonal(fn):
        def pair(j, carry):
            fn(2 * j)
            fn(2 * j + 1)
            return carry
        lax.fori_loop(0, qi // 2, pair, 0)

        @pl.when(qi % 2 == 1)
        def _():
            fn(qi - 1)

    def biased(s, bias):
        return s if bias is None else s + _stack_rows([bias], n_comb)

    ul, ll, lr = diag_bias
    upper = slice(0, rows_half)
    lower = slice(rows_half, 2 * rows_half)

    over_off_diagonal(pass_a)
    k_lo, k_hi = k_at(qi, 0, half), k_at(qi, half, half)
    s_ul = biased(_dot(q_stack_ref[upper, :], k_lo, _NT), ul)
    s_ll = biased(_dot(q_stack_ref[lower, :], k_lo, _NT), ll)
    s_lr = biased(_dot(q_stack_ref[lower, :], k_hi, _NT), lr)
    s_ref[qi, upper, 0:half] = s_ul
    s_ref[qi, lower, 0:half] = s_ll
    s_ref[qi, lower, half:blk] = s_lr
    mx_ref[upper, :] = lane_max(mx_ref[upper, :], s_ul)
    mx_ref[lower, :] = lane_max(lane_max(mx_ref[lower, :], s_ll), s_lr)
    mx_ref[...] = jnp.broadcast_to(jnp.max(mx_ref[...], axis=1, keepdims=True), mx_ref.shape)

    over_off_diagonal(pass_b)
    accumulate(0, probs(s_ref[qi, upper, 0:half], mx_ref[upper, :]), vx_at(qi, 0, half))
    accumulate(1, probs(s_ref[qi, lower, :], mx_ref[lower, :]), vx_at(qi, 0, blk))


def _normalized(acc_ref, c):
    acc = jnp.concatenate([acc_ref[0, c], acc_ref[1, c]], axis=0)
    return acc / pltpu.roll(acc, HEAD_DIM, 1)


def _store_q_stack(q_stack_ref, q, lane_masks):
    half = q.shape[0] // 2
    zero = jnp.zeros_like(q)
    n_comb = len(lane_masks)
    for g in range(2):
        for c, mask in enumerate(lane_masks):
            r0 = (g * n_comb + c) * half
            q_stack_ref[r0:r0 + half, :] = jnp.where(mask, q, zero)[g * half:(g + 1) * half]


def _head_rms(o, g):
    lane = lax.broadcasted_iota(jnp.int32, (1, LANES), 1)
    low = lane < HEAD_DIM
    sq = o * o
    s0 = jnp.sum(jnp.where(low, sq, 0.0), axis=1, keepdims=True)
    s1 = jnp.sum(jnp.where(low, 0.0, sq), axis=1, keepdims=True)
    ms = jnp.where(low, s0, s1) * (1.0 / HEAD_DIM)
    return o * lax.rsqrt(ms + RMS_EPS) * g


def _diff_attn_kernel(lam_ref, g_ref, q_ref, k_ref, v_ref, o_ref, q4_ref, s_ref, mx_ref, acc_ref,
                      *, blk, lam_init):
    qi = pl.program_id(2)
    half = blk // 2
    lane = lax.broadcasted_iota(jnp.int32, (1, LANES), 1)
    _store_q_stack(q4_ref, q_ref[0], [(lane // A_QK_DIM) == c for c in range(4)])

    r = lax.broadcasted_iota(jnp.int32, (half, half), 0)
    c_ = lax.broadcasted_iota(jnp.int32, (half, half), 1)
    causal = jnp.where(c_ <= r, 0.0, NEG_INF)
    _attn_two_pass(q4_ref, k_ref, v_ref, s_ref, mx_ref, acc_ref, qi, blk, 4,
                   lambda kb: None, (causal, None, causal))

    lp = lam_ref[...]
    lam = (jnp.exp(jnp.sum(lp[0:1] * lp[1:2], axis=1, keepdims=True))
           - jnp.exp(jnp.sum(lp[2:3] * lp[3:4], axis=1, keepdims=True)) + lam_init)
    heads = [_normalized(acc_ref, 2 * h) - lam * _normalized(acc_ref, 2 * h + 1) for h in range(2)]
    o = jnp.where(lane < HEAD_DIM, heads[0], heads[1])
    o_ref[0] = (_head_rms(o, g_ref[...]) * (1.0 - lam_init)).astype(o_ref.dtype)


def _attn_call(kernel_fn, name, n_comb, consts, const_specs, q, k, v, blk):
    b, s, w = q.shape
    pairs = w // LANES
    return pl.pallas_call(
        kernel_fn,
        out_shape=jax.ShapeDtypeStruct((b, s, w), BF16),
        grid=(b, pairs, s // blk),
        in_specs=const_specs + [
            pl.BlockSpec((1, blk, LANES), lambda bi, p, i: (bi, i, p)),
            pl.BlockSpec((1, s, LANES), lambda bi, p, i: (bi, 0, p)),
            pl.BlockSpec((1, s, LANES), lambda bi, p, i: (bi, 0, p)),
        ],
        out_specs=pl.BlockSpec((1, blk, LANES), lambda bi, p, i: (bi, i, p)),
        scratch_shapes=[pltpu.VMEM((n_comb * blk, LANES), BF16),
                        pltpu.VMEM((s // blk, n_comb * blk, blk), F32),
                        pltpu.VMEM((n_comb * blk, LANES), F32),
                        pltpu.VMEM((2, n_comb, blk // 2, LANES), F32)],
        compiler_params=_cparams(("parallel", "parallel", "arbitrary")),
        name=name,
    )(*consts, q, k, v)


def _diff_attn(q, k, v, lam_p, g, lam_init):
    blk = min(ATT_BLOCK, q.shape[1])
    return _attn_call(functools.partial(_diff_attn_kernel, blk=blk, lam_init=lam_init), "diff_attn", 4,
                      [lam_p, g], [_resident((4, LANES)), _resident((1, LANES))], q, k, v, blk)


def _dilated_log2_counts(blk):
    far = max(w for w, _ in DILATED_PATTERNS if w < max(p[0] for p in DILATED_PATTERNS))
    n_off = far // blk + 2
    i = np.arange(blk)[:, None]
    j = np.arange(blk)[None, :]
    cnt = np.zeros((n_off, blk, blk), np.float64)
    for o in range(n_off):
        delta = o * blk + i - j
        for window, dil in DILATED_PATTERNS:
            cnt[o] += ((delta >= 0) & (delta <= window) & (delta % dil == 0))
    return np.where(cnt > 0, np.log2(np.maximum(cnt, 1.0)), NEG_INF).astype(np.float32)


def _dil_attn_kernel(bias_ref, g_ref, q_ref, k_ref, v_ref, o_ref, q2_ref, s_ref, mx_ref, acc_ref,
                     *, blk, n_off):
    qi = pl.program_id(2)
    half = blk // 2
    lane = lax.broadcasted_iota(jnp.int32, (1, LANES), 1)
    low = lane < HEAD_DIM
    _store_q_stack(q2_ref, q_ref[0], [low, jnp.logical_not(low)])

    diag = bias_ref[0]
    _attn_two_pass(q2_ref, k_ref, v_ref, s_ref, mx_ref, acc_ref, qi, blk, 2,
                   lambda kb: bias_ref[jnp.minimum(qi - kb, n_off - 1)],
                   (diag[0:half, 0:half], diag[half:blk, 0:half], diag[half:blk, half:blk]))

    o = jnp.where(low, _normalized(acc_ref, 0), _normalized(acc_ref, 1))
    o_ref[0] = _head_rms(o, g_ref[...]).astype(o_ref.dtype)


def _dil_attn(q, k, v, bias, g):
    blk = bias.shape[1]
    return _attn_call(functools.partial(_dil_attn_kernel, blk=blk, n_off=bias.shape[0]), "dil_attn", 2,
                      [bias, g], [_resident(bias.shape), _resident((1, LANES))], q, k, v, blk)


def _solve_unit_lower(a_list, x_list):
    n = len(x_list)
    c = x_list[0].shape[0]
    nb = c // SOLVE_BLOCK
    sub = SOLVE_BLOCK // SUBLANES
    a_split = [_split2(a[:, 0:c]) for a in a_list]
    xs = [[x[SOLVE_BLOCK * i:SOLVE_BLOCK * (i + 1)] for i in range(nb)] for x in x_list]
    zeros = jnp.zeros((SOLVE_BLOCK, LANES), F32)
    for b in range(nb):
        base = b * SOLVE_BLOCK
        rows = [[xs[u][b][SUBLANES * i:SUBLANES * (i + 1)] for i in range(sub)] for u in range(n)]
        for t in range(SOLVE_BLOCK - 1):
            bi = t // SUBLANES
            for u in range(n):
                xt = rows[u][bi][t % SUBLANES:t % SUBLANES + 1, :]
                for i in range(bi, sub):
                    r0 = base + SUBLANES * i
                    a_col = a_list[u][r0:r0 + SUBLANES, base + t:base + t + 1]
                    rows[u][i] = rows[u][i] - a_col * xt
        for u in range(n):
            xs[u][b] = jnp.concatenate(rows[u], axis=0)
        if b + 1 < nb:
            below = slice(base + SOLVE_BLOCK, c)
            for u in range(n):
                a_hi, a_lo = a_split[u]
                x_hi, x_lo = _split2(jnp.concatenate(
                    [zeros] * b + [xs[u][b]] + [zeros] * (nb - 1 - b), axis=0))
                upd = _dot(a_lo[below], x_hi) + _dot(a_hi[below], x_lo) + _dot(a_hi[below], x_hi)
                for j in range(b + 1, nb):
                    xs[u][j] = xs[u][j] - upd[(j - b - 1) * SOLVE_BLOCK:(j - b) * SOLVE_BLOCK]
    return [jnp.concatenate(x, axis=0) for x in xs]


def _to_slab(col_block, half, lower):
    lane = lax.broadcasted_iota(jnp.int32, (1, LANES), 1)
    want_low = lower
    have_low = half == 0
    src = col_block if want_low == have_low else pltpu.roll(col_block, HEAD_DIM, 1)
    keep = (lane < HEAD_DIM) if want_low else (lane >= HEAD_DIM)
    return jnp.where(keep, src, 0.0)


def _rwkv_kernel(*refs, n_chunks, has_vres):
    if has_vres:
        (cc_ref, vf_ref, mu_ref, w0_ref, w2_ref, a0_ref, a2_ref, g2_ref, kk_ref, ka_ref, rk_ref,
         gng_ref, gnb_ref, v0_ref, v1_ref, v2_ref, bd_ref, tri_ref,
         o_ref, carry_ref, hs_ref) = refs
    else:
        (cc_ref, mu_ref, w0_ref, w2_ref, a0_ref, a2_ref, g2_ref, kk_ref, ka_ref, rk_ref,
         gng_ref, gnb_ref, bd_ref, tri_ref,
         o_ref, vf_out_ref, carry_ref, hs_ref) = refs
    cw = C_WIDTH
    cl = RWKV_CHUNK

    @pl.when(pl.program_id(1) == 0)
    def _():
        carry_ref[...] = jnp.zeros(carry_ref.shape, F32)
        hs_ref[...] = jnp.zeros(hs_ref.shape, F32)

    c_raw = cc_ref[0]
    tb = c_raw.shape[0]
    row = lax.broadcasted_iota(jnp.int32, (tb, 1), 0)
    prev = jnp.where(row == 0, carry_ref[...], pltpu.roll(c_raw, 1, 0))
    carry_ref[...] = c_raw[tb - 1:tb, :]
    c = c_raw + (prev - c_raw) * mu_ref[...]

    r = c[:, 0:cw]
    k = c[:, cw:2 * cw]
    v = c[:, 2 * cw:3 * cw]
    xwa = c[:, 3 * cw:3 * cw + LANES]
    xg = c[:, 3 * cw + LANES:3 * cw + 2 * LANES]
    if has_vres:
        mix = jax.nn.sigmoid(v0_ref[...] + _dot2r(_dot2r(v, v1_ref[...]), v2_ref[...]))
        v = v + (vf_ref[0] - v) * mix
    else:
        vf_out_ref[0] = v
    bd = bd_ref[...]
    lw = -DECAY_RATE * jax.nn.sigmoid(w0_ref[...] + _dot1(jnp.tanh(xwa), w2_ref[...]))
    a = jax.nn.sigmoid(a0_ref[...] + _dot1(xwa, a2_ref[...]))
    gate = _dot2r(jax.nn.sigmoid(xg), g2_ref[...])
    kkv = k * kk_ref[...]
    kappa = kkv / jnp.maximum(jnp.sqrt(_dot_sel_rhs(kkv * kkv, bd)), 1e-12)
    k2 = k * (1.0 + (a - 1.0) * ka_ref[...])
    bonus = _dot_sel_rhs(r * k2 * rk_ref[...], bd)
    beta = a * kappa

    lane = lax.broadcasted_iota(jnp.int32, (1, LANES), 1)
    low = lane < HEAD_DIM
    lane2 = lax.broadcasted_iota(jnp.int32, (1, cw), 1)
    rr = lax.broadcasted_iota(jnp.int32, (2 * cl, LANES), 0)
    cidx = lax.broadcasted_iota(jnp.int32, (2 * cl, LANES), 1) % cl
    tri_mask = cidx < jnp.where(rr < cl, rr, rr - cl + 1)
    eye = (lax.broadcasted_iota(jnp.int32, (cl, LANES), 0)
           == lax.broadcasted_iota(jnp.int32, (cl, LANES), 1))
    tri = tri_ref[...]

    zeros = jnp.zeros((cl, LANES), F32)
    units = []
    for ci in range(n_chunks):
        sl = slice(ci * cl, (ci + 1) * cl)
        lwc = lw[sl]
        cum = _dot_sel_lhs(tri, lwc)
        cum_end = cum[cl - 1:cl, :]
        rt = r[sl] * jnp.exp(cum)
        kt = kappa[sl] * jnp.exp(cum - lwc)
        e_inv = jnp.exp(-cum)
        e_end = jnp.exp(cum_end - cum)
        bh = beta[sl] * e_end
        kh = k2[sl] * e_end
        p_end = jnp.exp(cum_end)
        vc = v[sl]
        right = jnp.concatenate([beta[sl] * e_inv, k2[sl] * e_inv], axis=0)
        for h in range(C_HEADS):
            cb, half = divmod(h, 2)
            cs = slice(cb * LANES, (cb + 1) * LANES)
            hm = (lane2 // HEAD_DIM) == h
            units.append(dict(
                ci=ci, h=h, right=right,
                left=jnp.concatenate([jnp.where(hm, kt, 0.0), jnp.where(hm, rt, 0.0)], axis=0),
                vw=_to_slab(vc[:, cs], half, lower=False),
                kw=_to_slab(kt[:, cs], half, lower=True),
                rw=_to_slab(rt[:, cs], half, lower=True),
                bkw=jnp.concatenate([_to_slab(bh[:, cs], half, True), _to_slab(kh[:, cs], half, True)], axis=0),
                pe=_to_slab(jnp.broadcast_to(p_end[:, cs], (cl, LANES)), half, True)))
    for u in units:
        u["m"] = jnp.where(tri_mask, _dot2l(u["left"], u["right"], _NT), 0.0)
    for u in units:
        u["rhs"] = u["kw"] + _dot1(u["m"][0:cl], jnp.concatenate([zeros, u["vw"]], axis=0))
    xs = _solve_unit_lower([u["m"][0:cl] for u in units], [u["rhs"] for u in units])
    for u, x in zip(units, xs):
        u["st"] = jnp.concatenate([-x, u["vw"]], axis=0)
    for u in units:
        u["qy"] = u["rw"] + _dot2r(u["m"][cl:2 * cl], u["st"])
    for u in units:
        u["gh"] = jnp.where(eye, u["pe"], 0.0) + _dot1(u["bkw"].T[0:cl], u["st"])
    for u in units:
        hs = hs_ref[u["h"]]
        u["y"] = u["qy"] + _dot1(u["qy"], hs)
        hs_ref[u["h"], 0:cl, :] = jnp.where(low, 0.0, u["gh"] + _dot2r(u["gh"], hs))
    y_chunks = []
    for ci in range(n_chunks):
        yh = [u["y"] for u in units if u["ci"] == ci]
        cols = [jnp.where(low, pltpu.roll(yh[2 * cb], HEAD_DIM, 1), yh[2 * cb + 1])
                for cb in range(C_HEADS // 2)]
        y_chunks.append(jnp.concatenate(cols, axis=1))
    y = jnp.concatenate(y_chunks, axis=0) if n_chunks > 1 else y_chunks[0]

    inv = 1.0 / HEAD_DIM
    mean = _dot_sel_rhs(y, bd) * inv
    d = y - mean
    var = _dot_sel_rhs(d * d, bd) * inv
    yn = d * lax.rsqrt(var + C_GN_EPS) * gng_ref[...] + gnb_ref[...]
    out = (yn + bonus * v) * gate
    o_ref[0] = out.astype(o_ref.dtype)


def _rwkv(cc, v_first, p, consts):
    b, s, _ = cc.shape
    n_chunks = RWKV_CHUNKS_PER_STEP
    tb = n_chunks * RWKV_CHUNK
    has_vres = v_first is not None
    blk_cc = pl.BlockSpec((1, tb, C_COLS), lambda bi, j: (bi, j, 0))
    blk_cw = pl.BlockSpec((1, tb, C_WIDTH), lambda bi, j: (bi, j, 0))
    vec = lambda n: _resident((1, n))
    params = [p["mu"], p["w0"], p["w2"], p["a0"], p["a2"], p["g2"], p["k_k"], p["k_a"], p["r_k"],
              p["gn_g"], p["gn_b"]]
    specs = [vec(C_COLS), vec(C_WIDTH), _resident((LANES, C_WIDTH)), vec(C_WIDTH),
             _resident((LANES, C_WIDTH)), _resident((LANES, C_WIDTH)), vec(C_WIDTH), vec(C_WIDTH),
             vec(C_WIDTH), vec(C_WIDTH), vec(C_WIDTH)]
    if has_vres:
        params += [p["v0"], p["v1"], p["v2"]]
        specs += [vec(C_WIDTH), _resident((C_WIDTH, LANES)), _resident((LANES, C_WIDTH))]
    params += [consts["bd"], consts["tri"]]
    specs += [_resident((C_WIDTH, C_WIDTH)), _resident((RWKV_CHUNK, RWKV_CHUNK))]
    o_shape = jax.ShapeDtypeStruct((b, s, C_WIDTH), BF16)
    if has_vres:
        args = [cc, v_first] + params
        in_specs = [blk_cc, blk_cw] + specs
        out_shape, out_specs = o_shape, blk_cw
    else:
        args = [cc] + params
        in_specs = [blk_cc] + specs
        out_shape = (o_shape, jax.ShapeDtypeStruct((b, s, C_WIDTH), F32))
        out_specs = (blk_cw, blk_cw)
    return pl.pallas_call(
        functools.partial(_rwkv_kernel, n_chunks=n_chunks, has_vres=has_vres),
        out_shape=out_shape,
        grid=(b, s // tb),
        in_specs=in_specs,
        out_specs=out_specs,
        scratch_shapes=[pltpu.VMEM((1, C_COLS), F32),
                        pltpu.VMEM((C_HEADS, 2 * RWKV_CHUNK, LANES), F32)],
        compiler_params=_cparams(("parallel", "arbitrary")),
        name="rwkv_vres" if has_vres else "rwkv_first",
    )(*args)


def _out_proj_ln_kernel(x_ref, oa_ref, ob_ref, oc_ref, w_ref, g_ref, b_ref, o_ref, *, alpha):
    mix = (_dot(oa_ref[...], w_ref[0:A_WIDTH, :])
           + _dot(ob_ref[...], w_ref[A_WIDTH:A_WIDTH + B_WIDTH, :])
           + _dot(oc_ref[...], w_ref[A_WIDTH + B_WIDTH:, :]))
    o_ref[...] = _layer_norm(alpha * x_ref[...] + mix, g_ref[...], b_ref[...])


def _out_proj_ln(x, oa, ob, oc, w_out, g, b, alpha, layer):
    t, d = x.shape
    tm = min(ROW_TILE, t)
    row = lambda i: (i, 0)
    return pl.pallas_call(
        functools.partial(_out_proj_ln_kernel, alpha=alpha),
        out_shape=jax.ShapeDtypeStruct((t, d), F32),
        grid=(t // tm,),
        in_specs=[pl.BlockSpec((tm, d), row), pl.BlockSpec((tm, A_WIDTH), row),
                  pl.BlockSpec((tm, B_WIDTH), row), pl.BlockSpec((tm, C_WIDTH), row),
                  _resident_layer(w_out.shape[1:], layer), _resident((1, d)), _resident((1, d))],
        out_specs=pl.BlockSpec((tm, d), row),
        compiler_params=_cparams(("parallel",)),
        name="out_proj_ln",
    )(x, oa, ob, oc, w_out, g, b)


def _pad_rows(w, total, offset):
    return jnp.zeros((total, w.shape[1]), w.dtype).at[offset:offset + w.shape[0]].set(w)


def kernel(x, ffn_a_gate, ffn_a_up, ffn_a_down, ffn_b_gate, ffn_b_up, ffn_b_down, ln_g, ln_b, w_in, w_out, a_lam_q1, a_lam_k1, a_lam_q2, a_lam_k2, a_norm_g, b_norm_g, c_mu, c_w0, c_w2, c_a0, c_a2, c_g2, c_k_k, c_k_a, c_r_k, c_gn_g, c_gn_b, c_v0, c_v1, c_v2):
    bsz, s_len, d_model = x.shape
    depth = w_in.shape[0]
    alpha = (2 * depth) ** 0.25
    t = bsz * s_len

    tabs = _rope_tables(s_len, A_QK_DIM) + _rope_tables(s_len, HEAD_DIM)
    dil_bias = jnp.asarray(_dilated_log2_counts(min(ATT_BLOCK, s_len)))
    head_of = np.arange(C_WIDTH) // HEAD_DIM
    consts = {
        "bd": jnp.asarray(head_of[:, None] == head_of[None, :], BF16),
        "tri": jnp.asarray(np.tril(np.ones((RWKV_CHUNK, RWKV_CHUNK), np.float32)), BF16),
    }
    row = lambda vct: vct.reshape(1, -1)
    tile2 = lambda vct: jnp.tile(vct, LANES // HEAD_DIM).reshape(1, LANES)

    ffn_a = (ffn_a_gate.astype(BF16), ffn_a_up.astype(BF16), ffn_a_down.astype(BF16))
    ffn_b = (ffn_b_gate.astype(BF16), ffn_b_up.astype(BF16), ffn_b_down.astype(BF16))
    w_in_b = w_in.astype(BF16)
    w_out_b = w_out.astype(BF16)

    h = x.reshape(t, d_model)
    v_first = None
    for l in range(depth):
        h = _ffn_ln(h, *ffn_a, row(ln_g[l, 0]), row(ln_b[l, 0]), alpha, l)

        aq, ak, av, bq, bk, bv, cc = _in_proj(h, w_in_b, tabs, s_len, l)
        shp = (bsz, s_len, A_WIDTH)

        lam_init = 0.8 - 0.6 * math.exp(-0.3 * l)
        lam_p = jnp.zeros((4, LANES), F32).at[:, :A_QK_DIM].set(
            jnp.stack([a_lam_q1[l], a_lam_k1[l], a_lam_q2[l], a_lam_k2[l]]))
        o_a = _diff_attn(aq.reshape(shp), ak.reshape(shp), av.reshape(shp), lam_p,
                         tile2(a_norm_g[l]), lam_init)
        o_b = _dil_attn(bq.reshape(shp), bk.reshape(shp), bv.reshape(shp), dil_bias, tile2(b_norm_g[l]))

        p = {"mu": row(c_mu[l]), "w0": row(c_w0[l]), "a0": row(c_a0[l]),
             "w2": _pad_rows(c_w2[l], LANES, 0), "a2": _pad_rows(c_a2[l], LANES, C_W_RANK),
             "g2": c_g2[l], "k_k": row(c_k_k[l]), "k_a": row(c_k_a[l]), "r_k": row(c_r_k[l]),
             "gn_g": row(c_gn_g[l]), "gn_b": row(c_gn_b[l])}
        cc3 = cc.reshape(bsz, s_len, C_COLS)
        if l == 0:
            o_c, v_first = _rwkv(cc3, None, p, consts)
        else:
            p["v0"] = row(c_v0[l - 1])
            p["v1"] = jnp.zeros((C_WIDTH, LANES), F32).at[:, :C_V_RANK].set(c_v1[l - 1])
            p["v2"] = _pad_rows(c_v2[l - 1], LANES, 0)
            o_c = _rwkv(cc3, v_first, p, consts)

        h = _out_proj_ln(h, o_a.reshape(t, A_WIDTH), o_b.reshape(t, B_WIDTH), o_c.reshape(t, C_WIDTH),
                         w_out_b, row(ln_g[l, 1]), row(ln_b[l, 1]), alpha, l)
        h = _ffn_ln(h, *ffn_b, row(ln_g[l, 2]), row(ln_b[l, 2]), alpha, l)
    return h.reshape(bsz, s_len, d_model)
```

```python
import functools
import math

import numpy as np
import jax
import jax.numpy as jnp
from jax import lax
from jax.experimental import pallas as pl
from jax.experimental.pallas import tpu as pltpu

F32 = jnp.float32
BF16 = jnp.bfloat16

HEAD_DIM = 64
A_HEADS = 6
B_HEADS = 6
C_HEADS = 4
A_QK_DIM = 32
A_WIDTH = A_HEADS * HEAD_DIM
B_WIDTH = B_HEADS * HEAD_DIM
C_WIDTH = C_HEADS * HEAD_DIM
C_W_RANK = 64
C_A_RANK = 64
C_V_RANK = 32
C_G_RANK = 128
C_COLS = 3 * C_WIDTH + C_W_RANK + C_A_RANK + C_G_RANK
DILATED_PATTERNS = ((128, 1), (512, 4), (2048, 16))
ROPE_THETA = 10000.0
LN_EPS = 1e-5
RMS_EPS = 1e-5
C_GN_EPS = 64e-5
NEG_INF = -1e30
DECAY_RATE = math.exp(-0.5)
LOG2_E = math.log2(math.e)

LANES = 128
SUBLANES = 8
VMEM_LIMIT_BYTES = 56 * 1024 * 1024

ROW_TILE = 512
FF_CHUNK = 256
ATT_BLOCK = 512
RWKV_CHUNK = 64
RWKV_CHUNKS_PER_STEP = 4
SOLVE_BLOCK = 16

_NN = (((1,), (0,)), ((), ()))
_NT = (((1,), (1,)), ((), ()))


def _dot(a, b, dims=_NN):
    return lax.dot_general(a, b, dims, preferred_element_type=F32)


def _split2(x):
    hi = x.astype(BF16)
    lo = (x - hi.astype(F32)).astype(BF16)
    return hi, lo


def _split3(x):
    hi = x.astype(BF16)
    r = x - hi.astype(F32)
    mid = r.astype(BF16)
    lo = (r - mid.astype(F32)).astype(BF16)
    return hi, mid, lo


def _dot2l(a, b, dims=_NN):
    a_hi, a_lo = _split2(a)
    b_hi = b.astype(BF16)
    return _dot(a_lo, b_hi, dims) + _dot(a_hi, b_hi, dims)


def _dot2r(a, b, dims=_NN):
    a_hi = a.astype(BF16)
    b_hi, b_lo = _split2(b)
    return _dot(a_hi, b_lo, dims) + _dot(a_hi, b_hi, dims)


def _dot1(a, b, dims=_NN):
    return _dot(a.astype(BF16), b.astype(BF16), dims)


def _dot_sel_lhs(sel, x, dims=_NN):
    h, m, l = _split3(x)
    return _dot(sel, l, dims) + _dot(sel, m, dims) + _dot(sel, h, dims)


def _dot_sel_rhs(x, sel):
    h, l = _split2(x)
    return _dot(l, sel) + _dot(h, sel)


def _layer_norm(y, g, b):
    mu = jnp.mean(y, axis=-1, keepdims=True)
    d = y - mu
    var = jnp.mean(d * d, axis=-1, keepdims=True)
    return d * lax.rsqrt(var + LN_EPS) * g + b


def _cparams(sem):
    return pltpu.CompilerParams(dimension_semantics=sem, vmem_limit_bytes=VMEM_LIMIT_BYTES)


def _resident(shape):
    nd = len(shape)
    return pl.BlockSpec(shape, lambda *_: (0,) * nd, pipeline_mode=pl.Buffered(1))


def _resident_layer(shape, layer):
    nd = len(shape)
    return pl.BlockSpec((None,) + tuple(shape), lambda *_: (layer,) + (0,) * nd,
                        pipeline_mode=pl.Buffered(1))


def _ffn_ln_rows(x, wg_ref, wu_ref, wd_ref, g, b, alpha):
    d_ff = wg_ref.shape[1]
    xb = x.astype(BF16)
    acc = jnp.zeros(x.shape, F32)
    for c0 in range(0, d_ff, FF_CHUNK):
        gate = _dot(xb, wg_ref[:, c0:c0 + FF_CHUNK])
        up = _dot(xb, wu_ref[:, c0:c0 + FF_CHUNK])
        h = (gate * jax.nn.sigmoid(gate)) * up
        acc = acc + _dot(h.astype(BF16), wd_ref[c0:c0 + FF_CHUNK, :])
    return _layer_norm(alpha * x + 0.5 * acc, g, b)


def _ffn_specs(d, d_ff, layer):
    return [_resident_layer((d, d_ff), layer), _resident_layer((d, d_ff), layer),
            _resident_layer((d_ff, d), layer), _resident((1, d)), _resident((1, d))]


def _rope_lanes(y, cos, sin_signed, group):
    half = group // 2
    lane = lax.broadcasted_iota(jnp.int32, (1, LANES), 1)
    first = (lane % group) < half
    swapped = jnp.where(first, pltpu.roll(y, LANES - half, 1), pltpu.roll(y, half, 1))
    return y * cos + swapped * sin_signed


def _in_proj_rows(x, w_ref, ca_ref, sa_ref, cb_ref, sb_ref,
                  aq_ref, ak_ref, av_ref, bq_ref, bk_ref, bv_ref, cc_ref):
    xb = x.astype(BF16)
    ca, sa, cb, sb = ca_ref[...], sa_ref[...], cb_ref[...], sb_ref[...]
    a_scale = A_QK_DIM ** -0.5 * LOG2_E
    b_scale = HEAD_DIM ** -0.5 * LOG2_E
    plan = ((aq_ref, A_WIDTH, (ca, sa, A_QK_DIM), a_scale),
            (ak_ref, A_WIDTH, (ca, sa, A_QK_DIM), None),
            (av_ref, A_WIDTH, None, None),
            (bq_ref, B_WIDTH, (cb, sb, HEAD_DIM), b_scale),
            (bk_ref, B_WIDTH, (cb, sb, HEAD_DIM), None),
            (bv_ref, B_WIDTH, None, None))
    segs = [(ref, s0, rope, scale) for ref, width, rope, scale in plan for s0 in range(0, width, LANES)]
    wide = 2 * LANES
    for c0 in range(0, len(segs) * LANES, wide):
        y = _dot(xb, w_ref[:, c0:c0 + wide])
        for j in range(wide // LANES):
            ref, s0, rope, scale = segs[c0 // LANES + j]
            z = y[:, j * LANES:(j + 1) * LANES]
            if rope is not None:
                z = _rope_lanes(z, *rope)
            if scale is not None:
                z = z * scale
            ref[:, s0:s0 + LANES] = z.astype(ref.dtype)
    col = len(segs) * LANES
    for s0 in range(0, C_COLS, wide):
        cc_ref[:, s0:s0 + wide] = _dot(xb, w_ref[:, col + s0:col + s0 + wide])


def _ffn_in_proj_kernel(x_ref, wg_ref, wu_ref, wd_ref, g_ref, b_ref, w_ref, ca_ref, sa_ref, cb_ref, sb_ref,
                        h_ref, *out_refs, alpha):
    h = _ffn_ln_rows(x_ref[...], wg_ref, wu_ref, wd_ref, g_ref[...], b_ref[...], alpha)
    h_ref[...] = h
    _in_proj_rows(h, w_ref, ca_ref, sa_ref, cb_ref, sb_ref, *out_refs)


def _ffn_in_proj(x, ffn_w, g, b, w_in, tabs, s_len, alpha, layer):
    t, d = x.shape
    d_ff = ffn_w[0].shape[2]
    n_in = w_in.shape[2]
    tm = min(ROW_TILE, s_len)
    spb = s_len // tm
    row = lambda i: (i, 0)
    tab = pl.BlockSpec((tm, LANES), lambda i: (i % spb, 0))
    qkv = jax.ShapeDtypeStruct((t, A_WIDTH), BF16)
    return pl.pallas_call(
        functools.partial(_ffn_in_proj_kernel, alpha=alpha),
        out_shape=(jax.ShapeDtypeStruct((t, d), F32),) + (qkv,) * 6
        + (jax.ShapeDtypeStruct((t, C_COLS), F32),),
        grid=(t // tm,),
        in_specs=[pl.BlockSpec((tm, d), row)] + _ffn_specs(d, d_ff, layer)
        + [_resident_layer((d, n_in), layer), tab, tab, tab, tab],
        out_specs=(pl.BlockSpec((tm, d), row),) + tuple(pl.BlockSpec((tm, A_WIDTH), row) for _ in range(6))
        + (pl.BlockSpec((tm, C_COLS), row),),
        compiler_params=_cparams(("parallel",)),
        name="ffn_in_proj",
    )(x, *ffn_w, g, b, w_in, *tabs)


def _rope_tables(s_len, group):
    half = group // 2
    inv = ROPE_THETA ** (-jnp.arange(0, group, 2, dtype=F32) / group)
    ang = jnp.arange(s_len, dtype=F32)[:, None] * inv[None, :]
    lane = np.arange(LANES)
    idx = lane % half
    sign = np.where((lane % group) < half, -1.0, 1.0).astype(np.float32)
    cos = jnp.cos(ang)[:, idx]
    sin = jnp.sin(ang)[:, idx] * sign[None, :]
    return cos, sin


def _stack_rows(parts, n_comb):
    return jnp.concatenate([p for p in parts for _ in range(n_comb)], axis=0)


def _attn_two_pass(q_stack_ref, k_ref, v_ref, s_ref, mx_ref, acc_ref, qi, blk, n_comb,
                   off_bias, diag_bias):
    half = blk // 2
    rows_half = n_comb * half
    per_head = n_comb // 2
    lane = lax.broadcasted_iota(jnp.int32, (1, LANES), 1)
    low = lane < HEAD_DIM
    mx_ref[...] = jnp.full(mx_ref.shape, NEG_INF, F32)
    acc_ref[...] = jnp.zeros(acc_ref.shape, F32)

    def k_at(kb, lo, size):
        return k_ref[0, pl.ds(pl.multiple_of(kb * blk + lo, half), size), :]

    def vx_at(kb, lo, size):
        v = v_ref[0, pl.ds(pl.multiple_of(kb * blk + lo, half), size), :]
        one = jnp.ones_like(v)
        return (jnp.where(low, v, one), jnp.where(low, one, v))

    def lane_max(m, s):
        for j in range(s.shape[1] // LANES):
            m = jnp.maximum(m, s[:, j * LANES:(j + 1) * LANES])
        return m

    def probs(s, m):
        return jnp.concatenate([jnp.exp2(s[:, j * LANES:(j + 1) * LANES] - m)
                                for j in range(s.shape[1] // LANES)], axis=1).astype(BF16)

    def accumulate(g, p, vx):
        for c in range(n_comb):
            acc_ref[g, c] = acc_ref[g, c] + _dot(p[c * half:(c + 1) * half], vx[c // per_head])

    def pass_a(kb):
        s = _dot(q_stack_ref[...], k_at(kb, 0, blk), _NT)
        bias = off_bias(kb)
        if bias is not None:
            s = s + _stack_rows([bias[0:half], bias[half:blk]], n_comb)
        s_ref[kb] = s
        mx_ref[...] = lane_max(mx_ref[...], s)

    def pass_b(kb):
        p = probs(s_ref[kb], mx_ref[...])
        vx = vx_at(kb, 0, blk)
        for g in range(2):
            accumulate(g, p[g * rows_half:(g + 1) * rows_half], vx)

    def over_off_diagonal(fn):
        def pair(j, carry):
            fn(2 * j)
            fn(2 * j + 1)
            return carry
        lax.fori_loop(0, qi // 2, pair, 0)

        @pl.when(qi % 2 == 1)
        def _():
            fn(qi - 1)

    def biased(s, bias):
        return s if bias is None else s + _stack_rows([bias], n_comb)

    ul, ll, lr = diag_bias
    upper = slice(0, rows_half)
    lower = slice(rows_half, 2 * rows_half)

    over_off_diagonal(pass_a)
    k_lo, k_hi = k_at(qi, 0, half), k_at(qi, half, half)
    s_ul = biased(_dot(q_stack_ref[upper, :], k_lo, _NT), ul)
    s_ll = biased(_dot(q_stack_ref[lower, :], k_lo, _NT), ll)
    s_lr = biased(_dot(q_stack_ref[lower, :], k_hi, _NT), lr)
    s_ref[qi, upper, 0:half] = s_ul
    s_ref[qi, lower, 0:half] = s_ll
    s_ref[qi, lower, half:blk] = s_lr
    mx_ref[upper, :] = lane_max(mx_ref[upper, :], s_ul)
    mx_ref[lower, :] = lane_max(lane_max(mx_ref[lower, :], s_ll), s_lr)
    mx_ref[...] = jnp.broadcast_to(jnp.max(mx_ref[...], axis=1, keepdims=True), mx_ref.shape)

    over_off_diagonal(pass_b)
    accumulate(0, probs(s_ref[qi, upper, 0:half], mx_ref[upper, :]), vx_at(qi, 0, half))
    accumulate(1, probs(s_ref[qi, lower, :], mx_ref[lower, :]), vx_at(qi, 0, blk))


def _normalized(acc_ref, c):
    acc = jnp.concatenate([acc_ref[0, c], acc_ref[1, c]], axis=0)
    return acc / pltpu.roll(acc, HEAD_DIM, 1)


def _store_q_stack(q_stack_ref, q, lane_masks):
    half = q.shape[0] // 2
    zero = jnp.zeros_like(q)
    n_comb = len(lane_masks)
    for g in range(2):
        for c, mask in enumerate(lane_masks):
            r0 = (g * n_comb + c) * half
            q_stack_ref[r0:r0 + half, :] = jnp.where(mask, q, zero)[g * half:(g + 1) * half]


def _head_rms(o, g):
    lane = lax.broadcasted_iota(jnp.int32, (1, LANES), 1)
    low = lane < HEAD_DIM
    sq = o * o
    s0 = jnp.sum(jnp.where(low, sq, 0.0), axis=1, keepdims=True)
    s1 = jnp.sum(jnp.where(low, 0.0, sq), axis=1, keepdims=True)
    ms = jnp.where(low, s0, s1) * (1.0 / HEAD_DIM)
    return o * lax.rsqrt(ms + RMS_EPS) * g


def _diff_attn_kernel(lam_ref, g_ref, q_ref, k_ref, v_ref, o_ref, q4_ref, s_ref, mx_ref, acc_ref,
                      *, blk, lam_init):
    qi = pl.program_id(2)
    half = blk // 2
    lane = lax.broadcasted_iota(jnp.int32, (1, LANES), 1)
    _store_q_stack(q4_ref, q_ref[0], [(lane // A_QK_DIM) == c for c in range(4)])

    r = lax.broadcasted_iota(jnp.int32, (half, half), 0)
    c_ = lax.broadcasted_iota(jnp.int32, (half, half), 1)
    causal = jnp.where(c_ <= r, 0.0, NEG_INF)
    _attn_two_pass(q4_ref, k_ref, v_ref, s_ref, mx_ref, acc_ref, qi, blk, 4,
                   lambda kb: None, (causal, None, causal))

    lp = lam_ref[...]
    lam = (jnp.exp(jnp.sum(lp[0:1] * lp[1:2], axis=1, keepdims=True))
           - jnp.exp(jnp.sum(lp[2:3] * lp[3:4], axis=1, keepdims=True)) + lam_init)
    heads = [_normalized(acc_ref, 2 * h) - lam * _normalized(acc_ref, 2 * h + 1) for h in range(2)]
    o = jnp.where(lane < HEAD_DIM, heads[0], heads[1])
    o_ref[0] = (_head_rms(o, g_ref[...]) * (1.0 - lam_init)).astype(o_ref.dtype)


def _attn_call(kernel_fn, name, n_comb, consts, const_specs, q, k, v, blk):
    b, s, w = q.shape
    pairs = w // LANES
    return pl.pallas_call(
        kernel_fn,
        out_shape=jax.ShapeDtypeStruct((b, s, w), BF16),
        grid=(b, pairs, s // blk),
        in_specs=const_specs + [
            pl.BlockSpec((1, blk, LANES), lambda bi, p, i: (bi, i, p)),
            pl.BlockSpec((1, s, LANES), lambda bi, p, i: (bi, 0, p)),
            pl.BlockSpec((1, s, LANES), lambda bi, p, i: (bi, 0, p)),
        ],
        out_specs=pl.BlockSpec((1, blk, LANES), lambda bi, p, i: (bi, i, p)),
        scratch_shapes=[pltpu.VMEM((n_comb * blk, LANES), BF16),
                        pltpu.VMEM((s // blk, n_comb * blk, blk), F32),
                        pltpu.VMEM((n_comb * blk, LANES), F32),
                        pltpu.VMEM((2, n_comb, blk // 2, LANES), F32)],
        compiler_params=_cparams(("parallel", "parallel", "arbitrary")),
        name=name,
    )(*consts, q, k, v)


def _diff_attn(q, k, v, lam_p, g, lam_init):
    blk = min(ATT_BLOCK, q.shape[1])
    return _attn_call(functools.partial(_diff_attn_kernel, blk=blk, lam_init=lam_init), "diff_attn", 4,
                      [lam_p, g], [_resident((4, LANES)), _resident((1, LANES))], q, k, v, blk)


def _dilated_log2_counts(blk):
    far = max(w for w, _ in DILATED_PATTERNS if w < max(p[0] for p in DILATED_PATTERNS))
    n_off = far // blk + 2
    i = np.arange(blk)[:, None]
    j = np.arange(blk)[None, :]
    cnt = np.zeros((n_off, blk, blk), np.float64)
    for o in range(n_off):
        delta = o * blk + i - j
        for window, dil in DILATED_PATTERNS:
            cnt[o] += ((delta >= 0) & (delta <= window) & (delta % dil == 0))
    return np.where(cnt > 0, np.log2(np.maximum(cnt, 1.0)), NEG_INF).astype(np.float32)


def _dil_attn_kernel(bias_ref, g_ref, q_ref, k_ref, v_ref, o_ref, q2_ref, s_ref, mx_ref, acc_ref,
                     *, blk, n_off):
    qi = pl.program_id(2)
    half = blk // 2
    lane = lax.broadcasted_iota(jnp.int32, (1, LANES), 1)
    low = lane < HEAD_DIM
    _store_q_stack(q2_ref, q_ref[0], [low, jnp.logical_not(low)])

    diag = bias_ref[0]
    _attn_two_pass(q2_ref, k_ref, v_ref, s_ref, mx_ref, acc_ref, qi, blk, 2,
                   lambda kb: bias_ref[jnp.minimum(qi - kb, n_off - 1)],
                   (diag[0:half, 0:half], diag[half:blk, 0:half], diag[half:blk, half:blk]))

    o = jnp.where(low, _normalized(acc_ref, 0), _normalized(acc_ref, 1))
    o_ref[0] = _head_rms(o, g_ref[...]).astype(o_ref.dtype)


def _dil_attn(q, k, v, bias, g):
    blk = bias.shape[1]
    return _attn_call(functools.partial(_dil_attn_kernel, blk=blk, n_off=bias.shape[0]), "dil_attn", 2,
                      [bias, g], [_resident(bias.shape), _resident((1, LANES))], q, k, v, blk)


def _solve_unit_lower(a_list, x_list):
    n = len(x_list)
    c = x_list[0].shape[0]
    nb = c // SOLVE_BLOCK
    sub = SOLVE_BLOCK // SUBLANES
    a_split = [_split2(a[:, 0:c]) for a in a_list]
    xs = [[x[SOLVE_BLOCK * i:SOLVE_BLOCK * (i + 1)] for i in range(nb)] for x in x_list]
    zeros = jnp.zeros((SOLVE_BLOCK, LANES), F32)
    for b in range(nb):
        base = b * SOLVE_BLOCK
        rows = [[xs[u][b][SUBLANES * i:SUBLANES * (i + 1)] for i in range(sub)] for u in range(n)]
        for t in range(SOLVE_BLOCK - 1):
            bi = t // SUBLANES
            for u in range(n):
                xt = rows[u][bi][t % SUBLANES:t % SUBLANES + 1, :]
                for i in range(bi, sub):
                    r0 = base + SUBLANES * i
                    a_col = a_list[u][r0:r0 + SUBLANES, base + t:base + t + 1]
                    rows[u][i] = rows[u][i] - a_col * xt
        for u in range(n):
            xs[u][b] = jnp.concatenate(rows[u], axis=0)
        if b + 1 < nb:
            below = slice(base + SOLVE_BLOCK, c)
            for u in range(n):
                a_hi, a_lo = a_split[u]
                x_hi, x_lo = _split2(jnp.concatenate(
                    [zeros] * b + [xs[u][b]] + [zeros] * (nb - 1 - b), axis=0))
                upd = _dot(a_lo[below], x_hi) + _dot(a_hi[below], x_lo) + _dot(a_hi[below], x_hi)
                for j in range(b + 1, nb):
                    xs[u][j] = xs[u][j] - upd[(j - b - 1) * SOLVE_BLOCK:(j - b) * SOLVE_BLOCK]
    return [jnp.concatenate(x, axis=0) for x in xs]


def _to_slab(col_block, half, lower):
    lane = lax.broadcasted_iota(jnp.int32, (1, LANES), 1)
    want_low = lower
    have_low = half == 0
    src = col_block if want_low == have_low else pltpu.roll(col_block, HEAD_DIM, 1)
    keep = (lane < HEAD_DIM) if want_low else (lane >= HEAD_DIM)
    return jnp.where(keep, src, 0.0)


def _rwkv_kernel(*refs, n_chunks, has_vres):
    if has_vres:
        (cc_ref, vf_ref, mu_ref, w0_ref, w2_ref, a0_ref, a2_ref, g2_ref, kk_ref, ka_ref, rk_ref,
         gng_ref, gnb_ref, v0_ref, v1_ref, v2_ref, bd_ref, tri_ref,
         o_ref, carry_ref, hs_ref) = refs
    else:
        (cc_ref, mu_ref, w0_ref, w2_ref, a0_ref, a2_ref, g2_ref, kk_ref, ka_ref, rk_ref,
         gng_ref, gnb_ref, bd_ref, tri_ref,
         o_ref, vf_out_ref, carry_ref, hs_ref) = refs
    cw = C_WIDTH
    cl = RWKV_CHUNK

    @pl.when(pl.program_id(1) == 0)
    def _():
        carry_ref[...] = jnp.zeros(carry_ref.shape, F32)
        hs_ref[...] = jnp.zeros(hs_ref.shape, F32)

    c_raw = cc_ref[0]
    tb = c_raw.shape[0]
    row = lax.broadcasted_iota(jnp.int32, (tb, 1), 0)
    prev = jnp.where(row == 0, carry_ref[...], pltpu.roll(c_raw, 1, 0))
    carry_ref[...] = c_raw[tb - 1:tb, :]
    c = c_raw + (prev - c_raw) * mu_ref[...]

    r = c[:, 0:cw]
    k = c[:, cw:2 * cw]
    v = c[:, 2 * cw:3 * cw]
    xwa = c[:, 3 * cw:3 * cw + LANES]
    xg = c[:, 3 * cw + LANES:3 * cw + 2 * LANES]
    if has_vres:
        mix = jax.nn.sigmoid(v0_ref[...] + _dot2r(_dot2r(v, v1_ref[...]), v2_ref[...]))
        v = v + (vf_ref[0] - v) * mix
    else:
        vf_out_ref[0] = v
    bd = bd_ref[...]
    lw = -DECAY_RATE * jax.nn.sigmoid(w0_ref[...] + _dot1(jnp.tanh(xwa), w2_ref[...]))
    a = jax.nn.sigmoid(a0_ref[...] + _dot1(xwa, a2_ref[...]))
    gate = _dot2r(jax.nn.sigmoid(xg), g2_ref[...])
    kkv = k * kk_ref[...]
    kappa = kkv / jnp.maximum(jnp.sqrt(_dot_sel_rhs(kkv * kkv, bd)), 1e-12)
    k2 = k * (1.0 + (a - 1.0) * ka_ref[...])
    bonus = _dot_sel_rhs(r * k2 * rk_ref[...], bd)
    beta = a * kappa

    lane = lax.broadcasted_iota(jnp.int32, (1, LANES), 1)
    low = lane < HEAD_DIM
    lane2 = lax.broadcasted_iota(jnp.int32, (1, cw), 1)
    rr = lax.broadcasted_iota(jnp.int32, (2 * cl, LANES), 0)
    cidx = lax.broadcasted_iota(jnp.int32, (2 * cl, LANES), 1) % cl
    tri_mask = cidx < jnp.where(rr < cl, rr, rr - cl + 1)
    eye = (lax.broadcasted_iota(jnp.int32, (cl, LANES), 0)
           == lax.broadcasted_iota(jnp.int32, (cl, LANES), 1))
    tri = tri_ref[...]

    zeros = jnp.zeros((cl, LANES), F32)
    units = []
    for ci in range(n_chunks):
        sl = slice(ci * cl, (ci + 1) * cl)
        lwc = lw[sl]
        cum = _dot_sel_lhs(tri, lwc)
        cum_end = cum[cl - 1:cl, :]
        rt = r[sl] * jnp.exp(cum)
        kt = kappa[sl] * jnp.exp(cum - lwc)
        e_inv = jnp.exp(-cum)
        e_end = jnp.exp(cum_end - cum)
        bh = beta[sl] * e_end
        kh = k2[sl] * e_end
        p_end = jnp.exp(cum_end)
        vc = v[sl]
        right = jnp.concatenate([beta[sl] * e_inv, k2[sl] * e_inv], axis=0)
        for h in range(C_HEADS):
            cb, half = divmod(h, 2)
            cs = slice(cb * LANES, (cb + 1) * LANES)
            hm = (lane2 // HEAD_DIM) == h
            units.append(dict(
                ci=ci, h=h, right=right,
                left=jnp.concatenate([jnp.where(hm, kt, 0.0), jnp.where(hm, rt, 0.0)], axis=0),
                vw=_to_slab(vc[:, cs], half, lower=False),
                kw=_to_slab(kt[:, cs], half, lower=True),
                rw=_to_slab(rt[:, cs], half, lower=True),
                bkw=jnp.concatenate([_to_slab(bh[:, cs], half, True), _to_slab(kh[:, cs], half, True)], axis=0),
                pe=_to_slab(jnp.broadcast_to(p_end[:, cs], (cl, LANES)), half, True)))
    for u in units:
        u["m"] = jnp.where(tri_mask, _dot2l(u["left"], u["right"], _NT), 0.0)
    for u in units:
        u["rhs"] = u["kw"] + _dot1(u["m"][0:cl], jnp.concatenate([zeros, u["vw"]], axis=0))
    xs = _solve_unit_lower([u["m"][0:cl] for u in units], [u["rhs"] for u in units])
    for u, x in zip(units, xs):
        u["st"] = jnp.concatenate([-x, u["vw"]], axis=0)
    for u in units:
        u["qy"] = u["rw"] + _dot2r(u["m"][cl:2 * cl], u["st"])
    for u in units:
        u["gh"] = jnp.where(eye, u["pe"], 0.0) + _dot1(u["bkw"].T[0:cl], u["st"])
    for u in units:
        hs = hs_ref[u["h"]]
        u["y"] = u["qy"] + _dot1(u["qy"], hs)
        hs_ref[u["h"], 0:cl, :] = jnp.where(low, 0.0, u["gh"] + _dot2r(u["gh"], hs))
    y_chunks = []
    for ci in range(n_chunks):
        yh = [u["y"] for u in units if u["ci"] == ci]
        cols = [jnp.where(low, pltpu.roll(yh[2 * cb], HEAD_DIM, 1), yh[2 * cb + 1])
                for cb in range(C_HEADS // 2)]
        y_chunks.append(jnp.concatenate(cols, axis=1))
    y = jnp.concatenate(y_chunks, axis=0) if n_chunks > 1 else y_chunks[0]

    inv = 1.0 / HEAD_DIM
    mean = _dot_sel_rhs(y, bd) * inv
    d = y - mean
    var = _dot_sel_rhs(d * d, bd) * inv
    yn = d * lax.rsqrt(var + C_GN_EPS) * gng_ref[...] + gnb_ref[...]
    out = (yn + bonus * v) * gate
    o_ref[0] = out.astype(o_ref.dtype)


def _rwkv(cc, v_first, p, consts):
    b, s, _ = cc.shape
    n_chunks = RWKV_CHUNKS_PER_STEP
    tb = n_chunks * RWKV_CHUNK
    has_vres = v_first is not None
    blk_cc = pl.BlockSpec((1, tb, C_COLS), lambda bi, j: (bi, j, 0))
    blk_cw = pl.BlockSpec((1, tb, C_WIDTH), lambda bi, j: (bi, j, 0))
    vec = lambda n: _resident((1, n))
    params = [p["mu"], p["w0"], p["w2"], p["a0"], p["a2"], p["g2"], p["k_k"], p["k_a"], p["r_k"],
              p["gn_g"], p["gn_b"]]
    specs = [vec(C_COLS), vec(C_WIDTH), _resident((LANES, C_WIDTH)), vec(C_WIDTH),
             _resident((LANES, C_WIDTH)), _resident((LANES, C_WIDTH)), vec(C_WIDTH), vec(C_WIDTH),
             vec(C_WIDTH), vec(C_WIDTH), vec(C_WIDTH)]
    if has_vres:
        params += [p["v0"], p["v1"], p["v2"]]
        specs += [vec(C_WIDTH), _resident((C_WIDTH, LANES)), _resident((LANES, C_WIDTH))]
    params += [consts["bd"], consts["tri"]]
    specs += [_resident((C_WIDTH, C_WIDTH)), _resident((RWKV_CHUNK, RWKV_CHUNK))]
    o_shape = jax.ShapeDtypeStruct((b, s, C_WIDTH), BF16)
    if has_vres:
        args = [cc, v_first] + params
        in_specs = [blk_cc, blk_cw] + specs
        out_shape, out_specs = o_shape, blk_cw
    else:
        args = [cc] + params
        in_specs = [blk_cc] + specs
        out_shape = (o_shape, jax.ShapeDtypeStruct((b, s, C_WIDTH), F32))
        out_specs = (blk_cw, blk_cw)
    return pl.pallas_call(
        functools.partial(_rwkv_kernel, n_chunks=n_chunks, has_vres=has_vres),
        out_shape=out_shape,
        grid=(b, s // tb),
        in_specs=in_specs,
        out_specs=out_specs,
        scratch_shapes=[pltpu.VMEM((1, C_COLS), F32),
                        pltpu.VMEM((C_HEADS, 2 * RWKV_CHUNK, LANES), F32)],
        compiler_params=_cparams(("parallel", "arbitrary")),
        name="rwkv_vres" if has_vres else "rwkv_first",
    )(*args)


def _out_proj_ffn_kernel(x_ref, oa_ref, ob_ref, oc_ref, w_ref, g1_ref, b1_ref,
                         wg_ref, wu_ref, wd_ref, g2_ref, b2_ref, o_ref, *, alpha):
    mix = (_dot(oa_ref[...], w_ref[0:A_WIDTH, :])
           + _dot(ob_ref[...], w_ref[A_WIDTH:A_WIDTH + B_WIDTH, :])
           + _dot(oc_ref[...], w_ref[A_WIDTH + B_WIDTH:, :]))
    h = _layer_norm(alpha * x_ref[...] + mix, g1_ref[...], b1_ref[...])
    o_ref[...] = _ffn_ln_rows(h, wg_ref, wu_ref, wd_ref, g2_ref[...], b2_ref[...], alpha)


def _out_proj_ffn(x, oa, ob, oc, w_out, g1, b1, ffn_w, g2, b2, alpha, layer):
    t, d = x.shape
    d_ff = ffn_w[0].shape[2]
    tm = min(ROW_TILE, t)
    row = lambda i: (i, 0)
    return pl.pallas_call(
        functools.partial(_out_proj_ffn_kernel, alpha=alpha),
        out_shape=jax.ShapeDtypeStruct((t, d), F32),
        grid=(t // tm,),
        in_specs=[pl.BlockSpec((tm, d), row), pl.BlockSpec((tm, A_WIDTH), row),
                  pl.BlockSpec((tm, B_WIDTH), row), pl.BlockSpec((tm, C_WIDTH), row),
                  _resident_layer(w_out.shape[1:], layer), _resident((1, d)), _resident((1, d))]
        + _ffn_specs(d, d_ff, layer),
        out_specs=pl.BlockSpec((tm, d), row),
        compiler_params=_cparams(("parallel",)),
        name="out_proj_ffn",
    )(x, oa, ob, oc, w_out, g1, b1, *ffn_w, g2, b2)


def _pad_rows(w, total, offset):
    return jnp.zeros((total, w.shape[1]), w.dtype).at[offset:offset + w.shape[0]].set(w)


def kernel(x, ffn_a_gate, ffn_a_up, ffn_a_down, ffn_b_gate, ffn_b_up, ffn_b_down, ln_g, ln_b, w_in, w_out, a_lam_q1, a_lam_k1, a_lam_q2, a_lam_k2, a_norm_g, b_norm_g, c_mu, c_w0, c_w2, c_a0, c_a2, c_g2, c_k_k, c_k_a, c_r_k, c_gn_g, c_gn_b, c_v0, c_v1, c_v2):
    bsz, s_len, d_model = x.shape
    depth = w_in.shape[0]
    alpha = (2 * depth) ** 0.25
    t = bsz * s_len

    tabs = _rope_tables(s_len, A_QK_DIM) + _rope_tables(s_len, HEAD_DIM)
    dil_bias = jnp.asarray(_dilated_log2_counts(min(ATT_BLOCK, s_len)))
    head_of = np.arange(C_WIDTH) // HEAD_DIM
    consts = {
        "bd": jnp.asarray(head_of[:, None] == head_of[None, :], BF16),
        "tri": jnp.asarray(np.tril(np.ones((RWKV_CHUNK, RWKV_CHUNK), np.float32)), BF16),
    }
    row = lambda vct: vct.reshape(1, -1)
    tile2 = lambda vct: jnp.tile(vct, LANES // HEAD_DIM).reshape(1, LANES)

    ffn_a = (ffn_a_gate.astype(BF16), ffn_a_up.astype(BF16), ffn_a_down.astype(BF16))
    ffn_b = (ffn_b_gate.astype(BF16), ffn_b_up.astype(BF16), ffn_b_down.astype(BF16))
    w_in_b = w_in.astype(BF16)
    w_out_b = w_out.astype(BF16)

    h = x.reshape(t, d_model)
    v_first = None
    for l in range(depth):
        h, aq, ak, av, bq, bk, bv, cc = _ffn_in_proj(h, ffn_a, row(ln_g[l, 0]), row(ln_b[l, 0]),
                                                     w_in_b, tabs, s_len, alpha, l)
        shp = (bsz, s_len, A_WIDTH)

        lam_init = 0.8 - 0.6 * math.exp(-0.3 * l)
        lam_p = jnp.zeros((4, LANES), F32).at[:, :A_QK_DIM].set(
            jnp.stack([a_lam_q1[l], a_lam_k1[l], a_lam_q2[l], a_lam_k2[l]]))
        o_a = _diff_attn(aq.reshape(shp), ak.reshape(shp), av.reshape(shp), lam_p,
                         tile2(a_norm_g[l]), lam_init)
        o_b = _dil_attn(bq.reshape(shp), bk.reshape(shp), bv.reshape(shp), dil_bias, tile2(b_norm_g[l]))

        p = {"mu": row(c_mu[l]), "w0": row(c_w0[l]), "a0": row(c_a0[l]),
             "w2": _pad_rows(c_w2[l], LANES, 0), "a2": _pad_rows(c_a2[l], LANES, C_W_RANK),
             "g2": c_g2[l], "k_k": row(c_k_k[l]), "k_a": row(c_k_a[l]), "r_k": row(c_r_k[l]),
             "gn_g": row(c_gn_g[l]), "gn_b": row(c_gn_b[l])}
        cc3 = cc.reshape(bsz, s_len, C_COLS)
        if l == 0:
            o_c, v_first = _rwkv(cc3, None, p, consts)
        else:
            p["v0"] = row(c_v0[l - 1])
            p["v1"] = jnp.zeros((C_WIDTH, LANES), F32).at[:, :C_V_RANK].set(c_v1[l - 1])
            p["v2"] = _pad_rows(c_v2[l - 1], LANES, 0)
            o_c = _rwkv(cc3, v_first, p, consts)

        h = _out_proj_ffn(h, o_a.reshape(t, A_WIDTH), o_b.reshape(t, B_WIDTH), o_c.reshape(t, C_WIDTH),
                          w_out_b, row(ln_g[l, 1]), row(ln_b[l, 1]),
                          ffn_b, row(ln_g[l, 2]), row(ln_b[l, 2]), alpha, l)
    return h.reshape(bsz, s_len, d_model)
```

```python
import functools
import math

import numpy as np
import jax
import jax.numpy as jnp
from jax import lax
from jax.experimental import pallas as pl
from jax.experimental.pallas import tpu as pltpu

F32 = jnp.float32
BF16 = jnp.bfloat16

HEAD_DIM = 64
A_HEADS = 6
B_HEADS = 6
C_HEADS = 4
A_QK_DIM = 32
A_WIDTH = A_HEADS * HEAD_DIM
B_WIDTH = B_HEADS * HEAD_DIM
C_WIDTH = C_HEADS * HEAD_DIM
C_W_RANK = 64
C_A_RANK = 64
C_V_RANK = 32
C_G_RANK = 128
C_COLS = 3 * C_WIDTH + C_W_RANK + C_A_RANK + C_G_RANK
DILATED_PATTERNS = ((128, 1), (512, 4), (2048, 16))
ROPE_THETA = 10000.0
LN_EPS = 1e-5
RMS_EPS = 1e-5
C_GN_EPS = 64e-5
NEG_INF = -1e30
DECAY_RATE = math.exp(-0.5)
LOG2_E = math.log2(math.e)

LANES = 128
SUBLANES = 8
VMEM_LIMIT_BYTES = 56 * 1024 * 1024

ROW_TILE = 512
FF_CHUNK = 256
ATT_BLOCK = 512
RWKV_CHUNK = 64
RWKV_CHUNKS_PER_STEP = 4
SOLVE_BLOCK = 16

_NN = (((1,), (0,)), ((), ()))
_NT = (((1,), (1,)), ((), ()))


def _dot(a, b, dims=_NN):
    return lax.dot_general(a, b, dims, preferred_element_type=F32)


def _split2(x):
    hi = x.astype(BF16)
    lo = (x - hi.astype(F32)).astype(BF16)
    return hi, lo


def _split3(x):
    hi = x.astype(BF16)
    r = x - hi.astype(F32)
    mid = r.astype(BF16)
    lo = (r - mid.astype(F32)).astype(BF16)
    return hi, mid, lo


def _dot2l(a, b, dims=_NN):
    a_hi, a_lo = _split2(a)
    b_hi = b.astype(BF16)
    return _dot(a_lo, b_hi, dims) + _dot(a_hi, b_hi, dims)


def _dot2r(a, b, dims=_NN):
    a_hi = a.astype(BF16)
    b_hi, b_lo = _split2(b)
    return _dot(a_hi, b_lo, dims) + _dot(a_hi, b_hi, dims)


def _dot1(a, b, dims=_NN):
    return _dot(a.astype(BF16), b.astype(BF16), dims)


def _dot_sel_lhs(sel, x, dims=_NN):
    h, m, l = _split3(x)
    return _dot(sel, l, dims) + _dot(sel, m, dims) + _dot(sel, h, dims)


def _dot_sel_rhs(x, sel):
    h, l = _split2(x)
    return _dot(l, sel) + _dot(h, sel)


def _layer_norm(y, g, b):
    mu = jnp.mean(y, axis=-1, keepdims=True)
    d = y - mu
    var = jnp.mean(d * d, axis=-1, keepdims=True)
    return d * lax.rsqrt(var + LN_EPS) * g + b


def _cparams(sem):
    return pltpu.CompilerParams(dimension_semantics=sem, vmem_limit_bytes=VMEM_LIMIT_BYTES)


def _resident(shape):
    nd = len(shape)
    return pl.BlockSpec(shape, lambda *_: (0,) * nd, pipeline_mode=pl.Buffered(1))


def _resident_layer(shape, layer):
    nd = len(shape)
    return pl.BlockSpec((None,) + tuple(shape), lambda *_: (layer,) + (0,) * nd,
                        pipeline_mode=pl.Buffered(1))


def _ffn_ln_rows(x, wg_ref, wu_ref, wd_ref, g, b, alpha):
    d_ff = wg_ref.shape[1]
    xb = x.astype(BF16)
    acc = jnp.zeros(x.shape, F32)
    for c0 in range(0, d_ff, FF_CHUNK):
        gate = _dot(xb, wg_ref[:, c0:c0 + FF_CHUNK])
        up = _dot(xb, wu_ref[:, c0:c0 + FF_CHUNK])
        h = (gate * jax.nn.sigmoid(gate)) * up
        acc = acc + _dot(h.astype(BF16), wd_ref[c0:c0 + FF_CHUNK, :])
    return _layer_norm(alpha * x + 0.5 * acc, g, b)


def _ffn_specs(d, d_ff, layer):
    return [_resident_layer((d, d_ff), layer), _resident_layer((d, d_ff), layer),
            _resident_layer((d_ff, d), layer), _resident((1, d)), _resident((1, d))]


def _rope_lanes(y, cos, sin_signed, group):
    half = group // 2
    lane = lax.broadcasted_iota(jnp.int32, (1, LANES), 1)
    first = (lane % group) < half
    swapped = jnp.where(first, pltpu.roll(y, LANES - half, 1), pltpu.roll(y, half, 1))
    return y * cos + swapped * sin_signed


def _in_proj_rows(x, w_ref, ca_ref, sa_ref, cb_ref, sb_ref,
                  aq_ref, ak_ref, av_ref, bq_ref, bk_ref, bv_ref, cc_ref):
    xb = x.astype(BF16)
    ca, sa, cb, sb = ca_ref[...], sa_ref[...], cb_ref[...], sb_ref[...]
    a_scale = A_QK_DIM ** -0.5 * LOG2_E
    b_scale = HEAD_DIM ** -0.5 * LOG2_E
    plan = ((aq_ref, A_WIDTH, (ca, sa, A_QK_DIM), a_scale),
            (ak_ref, A_WIDTH, (ca, sa, A_QK_DIM), None),
            (av_ref, A_WIDTH, None, None),
            (bq_ref, B_WIDTH, (cb, sb, HEAD_DIM), b_scale),
            (bk_ref, B_WIDTH, (cb, sb, HEAD_DIM), None),
            (bv_ref, B_WIDTH, None, None))
    segs = [(ref, s0, rope, scale) for ref, width, rope, scale in plan for s0 in range(0, width, LANES)]
    wide = 2 * LANES
    for c0 in range(0, len(segs) * LANES, wide):
        y = _dot(xb, w_ref[:, c0:c0 + wide])
        for j in range(wide // LANES):
            ref, s0, rope, scale = segs[c0 // LANES + j]
            z = y[:, j * LANES:(j + 1) * LANES]
            if rope is not None:
                z = _rope_lanes(z, *rope)
            if scale is not None:
                z = z * scale
            ref[:, s0:s0 + LANES] = z.astype(ref.dtype)
    col = len(segs) * LANES
    for s0 in range(0, C_COLS, wide):
        cc_ref[:, s0:s0 + wide] = _dot(xb, w_ref[:, col + s0:col + s0 + wide])


def _ffn_in_proj_kernel(x_ref, wg_ref, wu_ref, wd_ref, g_ref, b_ref, w_ref, ca_ref, sa_ref, cb_ref, sb_ref,
                        h_ref, *out_refs, alpha):
    h = _ffn_ln_rows(x_ref[...], wg_ref, wu_ref, wd_ref, g_ref[...], b_ref[...], alpha)
    h_ref[...] = h
    _in_proj_rows(h, w_ref, ca_ref, sa_ref, cb_ref, sb_ref, *out_refs)


def _ffn_in_proj(x, ffn_w, g, b, w_in, tabs, s_len, alpha, layer):
    t, d = x.shape
    d_ff = ffn_w[0].shape[2]
    n_in = w_in.shape[2]
    tm = min(ROW_TILE, s_len)
    spb = s_len // tm
    row = lambda i: (i, 0)
    tab = pl.BlockSpec((tm, LANES), lambda i: (i % spb, 0))
    qkv = jax.ShapeDtypeStruct((t, A_WIDTH), BF16)
    return pl.pallas_call(
        functools.partial(_ffn_in_proj_kernel, alpha=alpha),
        out_shape=(jax.ShapeDtypeStruct((t, d), F32),) + (qkv,) * 6
        + (jax.ShapeDtypeStruct((t, C_COLS), F32),),
        grid=(t // tm,),
        in_specs=[pl.BlockSpec((tm, d), row)] + _ffn_specs(d, d_ff, layer)
        + [_resident_layer((d, n_in), layer), tab, tab, tab, tab],
        out_specs=(pl.BlockSpec((tm, d), row),) + tuple(pl.BlockSpec((tm, A_WIDTH), row) for _ in range(6))
        + (pl.BlockSpec((tm, C_COLS), row),),
        compiler_params=_cparams(("parallel",)),
        name="ffn_in_proj",
    )(x, *ffn_w, g, b, w_in, *tabs)


def _rope_tables(s_len, group):
    half = group // 2
    inv = ROPE_THETA ** (-jnp.arange(0, group, 2, dtype=F32) / group)
    ang = jnp.arange(s_len, dtype=F32)[:, None] * inv[None, :]
    lane = np.arange(LANES)
    idx = lane % half
    sign = np.where((lane % group) < half, -1.0, 1.0).astype(np.float32)
    cos = jnp.cos(ang)[:, idx]
    sin = jnp.sin(ang)[:, idx] * sign[None, :]
    return cos, sin


def _stack_rows(parts, n_comb):
    return jnp.concatenate([p for p in parts for _ in range(n_comb)], axis=0)


def _attn_two_pass(q_stack_ref, k_ref, v_ref, s_ref, mx_ref, acc_ref, qi, blk, n_comb,
                   off_bias, diag_bias):
    half = blk // 2
    rows_half = n_comb * half
    per_head = n_comb // 2
    lane = lax.broadcasted_iota(jnp.int32, (1, LANES), 1)
    low = lane < HEAD_DIM
    upper = slice(0, rows_half)
    lower = slice(rows_half, 2 * rows_half)

    def k_at(kb, lo, size):
        return k_ref[0, kb * blk + lo:kb * blk + lo + size, :]

    def vx_at(kb, lo, size):
        v = v_ref[0, kb * blk + lo:kb * blk + lo + size, :]
        one = jnp.ones_like(v)
        return (jnp.where(low, v, one), jnp.where(low, one, v))

    def lane_max(m, s):
        for j in range(s.shape[1] // LANES):
            m = jnp.maximum(m, s[:, j * LANES:(j + 1) * LANES])
        return m

    def probs(s, m):
        return jnp.concatenate([jnp.exp2(s[:, j * LANES:(j + 1) * LANES] - m)
                                for j in range(s.shape[1] // LANES)], axis=1).astype(BF16)

    def biased(s, bias):
        return s if bias is None else s + _stack_rows([bias], n_comb)

    mx_ref[...] = jnp.full(mx_ref.shape, NEG_INF, F32)
    for kb in range(qi):
        s = _dot(q_stack_ref[...], k_at(kb, 0, blk), _NT)
        bias = off_bias(kb)
        if bias is not None:
            s = s + _stack_rows([bias[0:half], bias[half:blk]], n_comb)
        s_ref[kb] = s
        mx_ref[...] = lane_max(mx_ref[...], s)
    ul, ll, lr = diag_bias
    k_lo, k_hi = k_at(qi, 0, half), k_at(qi, half, half)
    s_ul = biased(_dot(q_stack_ref[upper, :], k_lo, _NT), ul)
    s_ll = biased(_dot(q_stack_ref[lower, :], k_lo, _NT), ll)
    s_lr = biased(_dot(q_stack_ref[lower, :], k_hi, _NT), lr)
    s_ref[qi, upper, 0:half] = s_ul
    s_ref[qi, lower, 0:half] = s_ll
    s_ref[qi, lower, half:blk] = s_lr
    mx_ref[upper, :] = lane_max(mx_ref[upper, :], s_ul)
    mx_ref[lower, :] = lane_max(lane_max(mx_ref[lower, :], s_ll), s_lr)
    mx_ref[...] = jnp.broadcast_to(jnp.max(mx_ref[...], axis=1, keepdims=True), mx_ref.shape)

    acc = [[None] * n_comb for _ in range(2)]

    def accumulate(g, p, vx):
        for c in range(n_comb):
            d = _dot(p[c * half:(c + 1) * half], vx[c // per_head])
            acc[g][c] = d if acc[g][c] is None else acc[g][c] + d

    for kb in range(qi):
        p = probs(s_ref[kb], mx_ref[...])
        vx = vx_at(kb, 0, blk)
        for g in range(2):
            accumulate(g, p[g * rows_half:(g + 1) * rows_half], vx)
    accumulate(0, probs(s_ref[qi, upper, 0:half], mx_ref[upper, :]), vx_at(qi, 0, half))
    accumulate(1, probs(s_ref[qi, lower, :], mx_ref[lower, :]), vx_at(qi, 0, blk))
    for g in range(2):
        for c in range(n_comb):
            acc_ref[g, c] = acc[g][c]


def _for_each_query_block(qi, n_q, fn):
    for n in range(n_q):
        @pl.when(qi == n)
        def _(n=n):
            fn(n)


def _normalized(acc_ref, c):
    acc = jnp.concatenate([acc_ref[0, c], acc_ref[1, c]], axis=0)
    return acc / pltpu.roll(acc, HEAD_DIM, 1)


def _store_q_stack(q_stack_ref, q, lane_masks):
    half = q.shape[0] // 2
    zero = jnp.zeros_like(q)
    n_comb = len(lane_masks)
    for g in range(2):
        for c, mask in enumerate(lane_masks):
            r0 = (g * n_comb + c) * half
            q_stack_ref[r0:r0 + half, :] = jnp.where(mask, q, zero)[g * half:(g + 1) * half]


def _head_rms(o, g):
    lane = lax.broadcasted_iota(jnp.int32, (1, LANES), 1)
    low = lane < HEAD_DIM
    sq = o * o
    s0 = jnp.sum(jnp.where(low, sq, 0.0), axis=1, keepdims=True)
    s1 = jnp.sum(jnp.where(low, 0.0, sq), axis=1, keepdims=True)
    ms = jnp.where(low, s0, s1) * (1.0 / HEAD_DIM)
    return o * lax.rsqrt(ms + RMS_EPS) * g


def _diff_attn_kernel(lam_ref, g_ref, q_ref, k_ref, v_ref, o_ref, q4_ref, s_ref, mx_ref, acc_ref,
                      *, blk, lam_init):
    qi = pl.program_id(2)
    half = blk // 2
    lane = lax.broadcasted_iota(jnp.int32, (1, LANES), 1)
    _store_q_stack(q4_ref, q_ref[0], [(lane // A_QK_DIM) == c for c in range(4)])

    r = lax.broadcasted_iota(jnp.int32, (half, half), 0)
    c_ = lax.broadcasted_iota(jnp.int32, (half, half), 1)
    causal = jnp.where(c_ <= r, 0.0, NEG_INF)
    _for_each_query_block(qi, k_ref.shape[1] // blk, lambda n: _attn_two_pass(
        q4_ref, k_ref, v_ref, s_ref, mx_ref, acc_ref, n, blk, 4, lambda kb: None, (causal, None, causal)))

    lp = lam_ref[...]
    lam = (jnp.exp(jnp.sum(lp[0:1] * lp[1:2], axis=1, keepdims=True))
           - jnp.exp(jnp.sum(lp[2:3] * lp[3:4], axis=1, keepdims=True)) + lam_init)
    heads = [_normalized(acc_ref, 2 * h) - lam * _normalized(acc_ref, 2 * h + 1) for h in range(2)]
    o = jnp.where(lane < HEAD_DIM, heads[0], heads[1])
    o_ref[0] = (_head_rms(o, g_ref[...]) * (1.0 - lam_init)).astype(o_ref.dtype)


def _attn_call(kernel_fn, name, n_comb, consts, const_specs, q, k, v, blk):
    b, s, w = q.shape
    pairs = w // LANES
    return pl.pallas_call(
        kernel_fn,
        out_shape=jax.ShapeDtypeStruct((b, s, w), BF16),
        grid=(b, pairs, s // blk),
        in_specs=const_specs + [
            pl.BlockSpec((1, blk, LANES), lambda bi, p, i: (bi, i, p)),
            pl.BlockSpec((1, s, LANES), lambda bi, p, i: (bi, 0, p)),
            pl.BlockSpec((1, s, LANES), lambda bi, p, i: (bi, 0, p)),
        ],
        out_specs=pl.BlockSpec((1, blk, LANES), lambda bi, p, i: (bi, i, p)),
        scratch_shapes=[pltpu.VMEM((n_comb * blk, LANES), BF16),
                        pltpu.VMEM((s // blk, n_comb * blk, blk), F32),
                        pltpu.VMEM((n_comb * blk, LANES), F32),
                        pltpu.VMEM((2, n_comb, blk // 2, LANES), F32)],
        compiler_params=_cparams(("parallel", "parallel", "arbitrary")),
        name=name,
    )(*consts, q, k, v)


def _diff_attn(q, k, v, lam_p, g, lam_init):
    blk = min(ATT_BLOCK, q.shape[1])
    return _attn_call(functools.partial(_diff_attn_kernel, blk=blk, lam_init=lam_init), "diff_attn", 4,
                      [lam_p, g], [_resident((4, LANES)), _resident((1, LANES))], q, k, v, blk)


def _dilated_log2_counts(blk):
    far = max(w for w, _ in DILATED_PATTERNS if w < max(p[0] for p in DILATED_PATTERNS))
    n_off = far // blk + 2
    i = np.arange(blk)[:, None]
    j = np.arange(blk)[None, :]
    cnt = np.zeros((n_off, blk, blk), np.float64)
    for o in range(n_off):
        delta = o * blk + i - j
        for window, dil in DILATED_PATTERNS:
            cnt[o] += ((delta >= 0) & (delta <= window) & (delta % dil == 0))
    return np.where(cnt > 0, np.log2(np.maximum(cnt, 1.0)), NEG_INF).astype(np.float32)


def _dil_attn_kernel(bias_ref, g_ref, q_ref, k_ref, v_ref, o_ref, q2_ref, s_ref, mx_ref, acc_ref,
                     *, blk, n_off):
    qi = pl.program_id(2)
    half = blk // 2
    lane = lax.broadcasted_iota(jnp.int32, (1, LANES), 1)
    low = lane < HEAD_DIM
    _store_q_stack(q2_ref, q_ref[0], [low, jnp.logical_not(low)])

    diag = bias_ref[0]
    _for_each_query_block(qi, k_ref.shape[1] // blk, lambda n: _attn_two_pass(
        q2_ref, k_ref, v_ref, s_ref, mx_ref, acc_ref, n, blk, 2,
        lambda kb: bias_ref[min(n - kb, n_off - 1)],
        (diag[0:half, 0:half], diag[half:blk, 0:half], diag[half:blk, half:blk])))

    o = jnp.where(low, _normalized(acc_ref, 0), _normalized(acc_ref, 1))
    o_ref[0] = _head_rms(o, g_ref[...]).astype(o_ref.dtype)


def _dil_attn(q, k, v, bias, g):
    blk = bias.shape[1]
    return _attn_call(functools.partial(_dil_attn_kernel, blk=blk, n_off=bias.shape[0]), "dil_attn", 2,
                      [bias, g], [_resident(bias.shape), _resident((1, LANES))], q, k, v, blk)


def _solve_unit_lower(a_list, x_list):
    n = len(x_list)
    c = x_list[0].shape[0]
    nb = c // SOLVE_BLOCK
    sub = SOLVE_BLOCK // SUBLANES
    a_split = [_split2(a[:, 0:c]) for a in a_list]
    xs = [[x[SOLVE_BLOCK * i:SOLVE_BLOCK * (i + 1)] for i in range(nb)] for x in x_list]
    zeros = jnp.zeros((SOLVE_BLOCK, LANES), F32)
    for b in range(nb):
        base = b * SOLVE_BLOCK
        rows = [[xs[u][b][SUBLANES * i:SUBLANES * (i + 1)] for i in range(sub)] for u in range(n)]
        for t in range(SOLVE_BLOCK - 1):
            bi = t // SUBLANES
            for u in range(n):
                xt = rows[u][bi][t % SUBLANES:t % SUBLANES + 1, :]
                for i in range(bi, sub):
                    r0 = base + SUBLANES * i
                    a_col = a_list[u][r0:r0 + SUBLANES, base + t:base + t + 1]
                    rows[u][i] = rows[u][i] - a_col * xt
        for u in range(n):
            xs[u][b] = jnp.concatenate(rows[u], axis=0)
        if b + 1 < nb:
            below = slice(base + SOLVE_BLOCK, c)
            for u in range(n):
                a_hi, a_lo = a_split[u]
                x_hi, x_lo = _split2(jnp.concatenate(
                    [zeros] * b + [xs[u][b]] + [zeros] * (nb - 1 - b), axis=0))
                upd = _dot(a_lo[below], x_hi) + _dot(a_hi[below], x_lo) + _dot(a_hi[below], x_hi)
                for j in range(b + 1, nb):
                    xs[u][j] = xs[u][j] - upd[(j - b - 1) * SOLVE_BLOCK:(j - b) * SOLVE_BLOCK]
    return [jnp.concatenate(x, axis=0) for x in xs]


def _to_slab(col_block, half, lower):
    lane = lax.broadcasted_iota(jnp.int32, (1, LANES), 1)
    want_low = lower
    have_low = half == 0
    src = col_block if want_low == have_low else pltpu.roll(col_block, HEAD_DIM, 1)
    keep = (lane < HEAD_DIM) if want_low else (lane >= HEAD_DIM)
    return jnp.where(keep, src, 0.0)


def _rwkv_kernel(*refs, n_chunks, has_vres):
    if has_vres:
        (cc_ref, vf_ref, mu_ref, w0_ref, w2_ref, a0_ref, a2_ref, g2_ref, kk_ref, ka_ref, rk_ref,
         gng_ref, gnb_ref, v0_ref, v1_ref, v2_ref, bd_ref, tri_ref,
         o_ref, carry_ref, hs_ref) = refs
    else:
        (cc_ref, mu_ref, w0_ref, w2_ref, a0_ref, a2_ref, g2_ref, kk_ref, ka_ref, rk_ref,
         gng_ref, gnb_ref, bd_ref, tri_ref,
         o_ref, vf_out_ref, carry_ref, hs_ref) = refs
    cw = C_WIDTH
    cl = RWKV_CHUNK

    @pl.when(pl.program_id(1) == 0)
    def _():
        carry_ref[...] = jnp.zeros(carry_ref.shape, F32)
        hs_ref[...] = jnp.zeros(hs_ref.shape, F32)

    c_raw = cc_ref[0]
    tb = c_raw.shape[0]
    row = lax.broadcasted_iota(jnp.int32, (tb, 1), 0)
    prev = jnp.where(row == 0, carry_ref[...], pltpu.roll(c_raw, 1, 0))
    carry_ref[...] = c_raw[tb - 1:tb, :]
    c = c_raw + (prev - c_raw) * mu_ref[...]

    r = c[:, 0:cw]
    k = c[:, cw:2 * cw]
    v = c[:, 2 * cw:3 * cw]
    xwa = c[:, 3 * cw:3 * cw + LANES]
    xg = c[:, 3 * cw + LANES:3 * cw + 2 * LANES]
    if has_vres:
        mix = jax.nn.sigmoid(v0_ref[...] + _dot2r(_dot2r(v, v1_ref[...]), v2_ref[...]))
        v = v + (vf_ref[0] - v) * mix
    else:
        vf_out_ref[0] = v
    bd = bd_ref[...]
    lw = -DECAY_RATE * jax.nn.sigmoid(w0_ref[...] + _dot1(jnp.tanh(xwa), w2_ref[...]))
    a = jax.nn.sigmoid(a0_ref[...] + _dot1(xwa, a2_ref[...]))
    gate = _dot2r(jax.nn.sigmoid(xg), g2_ref[...])
    kkv = k * kk_ref[...]
    kappa = kkv / jnp.maximum(jnp.sqrt(_dot_sel_rhs(kkv * kkv, bd)), 1e-12)
    k2 = k * (1.0 + (a - 1.0) * ka_ref[...])
    bonus = _dot_sel_rhs(r * k2 * rk_ref[...], bd)
    beta = a * kappa

    lane = lax.broadcasted_iota(jnp.int32, (1, LANES), 1)
    low = lane < HEAD_DIM
    lane2 = lax.broadcasted_iota(jnp.int32, (1, cw), 1)
    rr = lax.broadcasted_iota(jnp.int32, (2 * cl, LANES), 0)
    cidx = lax.broadcasted_iota(jnp.int32, (2 * cl, LANES), 1) % cl
    tri_mask = cidx < jnp.where(rr < cl, rr, rr - cl + 1)
    eye = (lax.broadcasted_iota(jnp.int32, (cl, LANES), 0)
           == lax.broadcasted_iota(jnp.int32, (cl, LANES), 1))
    tri = tri_ref[...]

    zeros = jnp.zeros((cl, LANES), F32)
    units = []
    for ci in range(n_chunks):
        sl = slice(ci * cl, (ci + 1) * cl)
        lwc = lw[sl]
        cum = _dot_sel_lhs(tri, lwc)
        cum_end = cum[cl - 1:cl, :]
        rt = r[sl] * jnp.exp(cum)
        kt = kappa[sl] * jnp.exp(cum - lwc)
        e_inv = jnp.exp(-cum)
        e_end = jnp.exp(cum_end - cum)
        bh = beta[sl] * e_end
        kh = k2[sl] * e_end
        p_end = jnp.exp(cum_end)
        vc = v[sl]
        right = jnp.concatenate([beta[sl] * e_inv, k2[sl] * e_inv], axis=0)
        for h in range(C_HEADS):
            cb, half = divmod(h, 2)
            cs = slice(cb * LANES, (cb + 1) * LANES)
            hm = (lane2 // HEAD_DIM) == h
            units.append(dict(
                ci=ci, h=h, right=right,
                left=jnp.concatenate([jnp.where(hm, kt, 0.0), jnp.where(hm, rt, 0.0)], axis=0),
                vw=_to_slab(vc[:, cs], half, lower=False),
                kw=_to_slab(kt[:, cs], half, lower=True),
                rw=_to_slab(rt[:, cs], half, lower=True),
                bkw=jnp.concatenate([_to_slab(bh[:, cs], half, True), _to_slab(kh[:, cs], half, True)], axis=0),
                pe=_to_slab(jnp.broadcast_to(p_end[:, cs], (cl, LANES)), half, True)))
    for u in units:
        u["m"] = jnp.where(tri_mask, _dot2l(u["left"], u["right"], _NT), 0.0)
    for u in units:
        u["rhs"] = u["kw"] + _dot1(u["m"][0:cl], jnp.concatenate([zeros, u["vw"]], axis=0))
    xs = _solve_unit_lower([u["m"][0:cl] for u in units], [u["rhs"] for u in units])
    for u, x in zip(units, xs):
        u["st"] = jnp.concatenate([-x, u["vw"]], axis=0)
    for u in units:
        u["qy"] = u["rw"] + _dot2r(u["m"][cl:2 * cl], u["st"])
    for u in units:
        u["gh"] = jnp.where(eye, u["pe"], 0.0) + _dot1(u["bkw"].T[0:cl], u["st"])
    for u in units:
        hs = hs_ref[u["h"]]
        u["y"] = u["qy"] + _dot1(u["qy"], hs)
        hs_ref[u["h"], 0:cl, :] = jnp.where(low, 0.0, u["gh"] + _dot2r(u["gh"], hs))
    y_chunks = []
    for ci in range(n_chunks):
        yh = [u["y"] for u in units if u["ci"] == ci]
        cols = [jnp.where(low, pltpu.roll(yh[2 * cb], HEAD_DIM, 1), yh[2 * cb + 1])
                for cb in range(C_HEADS // 2)]
        y_chunks.append(jnp.concatenate(cols, axis=1))
    y = jnp.concatenate(y_chunks, axis=0) if n_chunks > 1 else y_chunks[0]

    inv = 1.0 / HEAD_DIM
    mean = _dot_sel_rhs(y, bd) * inv
    d = y - mean
    var = _dot_sel_rhs(d * d, bd) * inv
    yn = d * lax.rsqrt(var + C_GN_EPS) * gng_ref[...] + gnb_ref[...]
    out = (yn + bonus * v) * gate
    o_ref[0] = out.astype(o_ref.dtype)


def _rwkv(cc, v_first, p, consts):
    b, s, _ = cc.shape
    n_chunks = RWKV_CHUNKS_PER_STEP
    tb = n_chunks * RWKV_CHUNK
    has_vres = v_first is not None
    blk_cc = pl.BlockSpec((1, tb, C_COLS), lambda bi, j: (bi, j, 0))
    blk_cw = pl.BlockSpec((1, tb, C_WIDTH), lambda bi, j: (bi, j, 0))
    vec = lambda n: _resident((1, n))
    params = [p["mu"], p["w0"], p["w2"], p["a0"], p["a2"], p["g2"], p["k_k"], p["k_a"], p["r_k"],
              p["gn_g"], p["gn_b"]]
    specs = [vec(C_COLS), vec(C_WIDTH), _resident((LANES, C_WIDTH)), vec(C_WIDTH),
             _resident((LANES, C_WIDTH)), _resident((LANES, C_WIDTH)), vec(C_WIDTH), vec(C_WIDTH),
             vec(C_WIDTH), vec(C_WIDTH), vec(C_WIDTH)]
    if has_vres:
        params += [p["v0"], p["v1"], p["v2"]]
        specs += [vec(C_WIDTH), _resident((C_WIDTH, LANES)), _resident((LANES, C_WIDTH))]
    params += [consts["bd"], consts["tri"]]
    specs += [_resident((C_WIDTH, C_WIDTH)), _resident((RWKV_CHUNK, RWKV_CHUNK))]
    o_shape = jax.ShapeDtypeStruct((b, s, C_WIDTH), BF16)
    if has_vres:
        args = [cc, v_first] + params
        in_specs = [blk_cc, blk_cw] + specs
        out_shape, out_specs = o_shape, blk_cw
    else:
        args = [cc] + params
        in_specs = [blk_cc] + specs
        out_shape = (o_shape, jax.ShapeDtypeStruct((b, s, C_WIDTH), F32))
        out_specs = (blk_cw, blk_cw)
    return pl.pallas_call(
        functools.partial(_rwkv_kernel, n_chunks=n_chunks, has_vres=has_vres),
        out_shape=out_shape,
        grid=(b, s // tb),
        in_specs=in_specs,
        out_specs=out_specs,
        scratch_shapes=[pltpu.VMEM((1, C_COLS), F32),
                        pltpu.VMEM((C_HEADS, 2 * RWKV_CHUNK, LANES), F32)],
        compiler_params=_cparams(("parallel", "arbitrary")),
        name="rwkv_vres" if has_vres else "rwkv_first",
    )(*args)


def _out_proj_ffn_kernel(x_ref, oa_ref, ob_ref, oc_ref, w_ref, g1_ref, b1_ref,
                         wg_ref, wu_ref, wd_ref, g2_ref, b2_ref, o_ref, *, alpha):
    mix = (_dot(oa_ref[...], w_ref[0:A_WIDTH, :])
           + _dot(ob_ref[...], w_ref[A_WIDTH:A_WIDTH + B_WIDTH, :])
           + _dot(oc_ref[...], w_ref[A_WIDTH + B_WIDTH:, :]))
    h = _layer_norm(alpha * x_ref[...] + mix, g1_ref[...], b1_ref[...])
    o_ref[...] = _ffn_ln_rows(h, wg_ref, wu_ref, wd_ref, g2_ref[...], b2_ref[...], alpha)


def _out_proj_ffn(x, oa, ob, oc, w_out, g1, b1, ffn_w, g2, b2, alpha, layer):
    t, d = x.shape
    d_ff = ffn_w[0].shape[2]
    tm = min(ROW_TILE, t)
    row = lambda i: (i, 0)
    return pl.pallas_call(
        functools.partial(_out_proj_ffn_kernel, alpha=alpha),
        out_shape=jax.ShapeDtypeStruct((t, d), F32),
        grid=(t // tm,),
        in_specs=[pl.BlockSpec((tm, d), row), pl.BlockSpec((tm, A_WIDTH), row),
                  pl.BlockSpec((tm, B_WIDTH), row), pl.BlockSpec((tm, C_WIDTH), row),
                  _resident_layer(w_out.shape[1:], layer), _resident((1, d)), _resident((1, d))]
        + _ffn_specs(d, d_ff, layer),
        out_specs=pl.BlockSpec((tm, d), row),
        compiler_params=_cparams(("parallel",)),
        name="out_proj_ffn",
    )(x, oa, ob, oc, w_out, g1, b1, *ffn_w, g2, b2)


def _pad_rows(w, total, offset):
    return jnp.zeros((total, w.shape[1]), w.dtype).at[offset:offset + w.shape[0]].set(w)


def kernel(x, ffn_a_gate, ffn_a_up, ffn_a_down, ffn_b_gate, ffn_b_up, ffn_b_down, ln_g, ln_b, w_in, w_out, a_lam_q1, a_lam_k1, a_lam_q2, a_lam_k2, a_norm_g, b_norm_g, c_mu, c_w0, c_w2, c_a0, c_a2, c_g2, c_k_k, c_k_a, c_r_k, c_gn_g, c_gn_b, c_v0, c_v1, c_v2):
    bsz, s_len, d_model = x.shape
    depth = w_in.shape[0]
    alpha = (2 * depth) ** 0.25
    t = bsz * s_len

    tabs = _rope_tables(s_len, A_QK_DIM) + _rope_tables(s_len, HEAD_DIM)
    dil_bias = jnp.asarray(_dilated_log2_counts(min(ATT_BLOCK, s_len)))
    head_of = np.arange(C_WIDTH) // HEAD_DIM
    consts = {
        "bd": jnp.asarray(head_of[:, None] == head_of[None, :], BF16),
        "tri": jnp.asarray(np.tril(np.ones((RWKV_CHUNK, RWKV_CHUNK), np.float32)), BF16),
    }
    row = lambda vct: vct.reshape(1, -1)
    tile2 = lambda vct: jnp.tile(vct, LANES // HEAD_DIM).reshape(1, LANES)

    ffn_a = (ffn_a_gate.astype(BF16), ffn_a_up.astype(BF16), ffn_a_down.astype(BF16))
    ffn_b = (ffn_b_gate.astype(BF16), ffn_b_up.astype(BF16), ffn_b_down.astype(BF16))
    w_in_b = w_in.astype(BF16)
    w_out_b = w_out.astype(BF16)

    h = x.reshape(t, d_model)
    v_first = None
    for l in range(depth):
        h, aq, ak, av, bq, bk, bv, cc = _ffn_in_proj(h, ffn_a, row(ln_g[l, 0]), row(ln_b[l, 0]),
                                                     w_in_b, tabs, s_len, alpha, l)
        shp = (bsz, s_len, A_WIDTH)

        lam_init = 0.8 - 0.6 * math.exp(-0.3 * l)
        lam_p = jnp.zeros((4, LANES), F32).at[:, :A_QK_DIM].set(
            jnp.stack([a_lam_q1[l], a_lam_k1[l], a_lam_q2[l], a_lam_k2[l]]))
        o_a = _diff_attn(aq.reshape(shp), ak.reshape(shp), av.reshape(shp), lam_p,
                         tile2(a_norm_g[l]), lam_init)
        o_b = _dil_attn(bq.reshape(shp), bk.reshape(shp), bv.reshape(shp), dil_bias, tile2(b_norm_g[l]))

        p = {"mu": row(c_mu[l]), "w0": row(c_w0[l]), "a0": row(c_a0[l]),
             "w2": _pad_rows(c_w2[l], LANES, 0), "a2": _pad_rows(c_a2[l], LANES, C_W_RANK),
             "g2": c_g2[l], "k_k": row(c_k_k[l]), "k_a": row(c_k_a[l]), "r_k": row(c_r_k[l]),
             "gn_g": row(c_gn_g[l]), "gn_b": row(c_gn_b[l])}
        cc3 = cc.reshape(bsz, s_len, C_COLS)
        if l == 0:
            o_c, v_first = _rwkv(cc3, None, p, consts)
        else:
            p["v0"] = row(c_v0[l - 1])
            p["v1"] = jnp.zeros((C_WIDTH, LANES), F32).at[:, :C_V_RANK].set(c_v1[l - 1])
            p["v2"] = _pad_rows(c_v2[l - 1], LANES, 0)
            o_c = _rwkv(cc3, v_first, p, consts)

        h = _out_proj_ffn(h, o_a.reshape(t, A_WIDTH), o_b.reshape(t, B_WIDTH), o_c.reshape(t, C_WIDTH),
                          w_out_b, row(ln_g[l, 1]), row(ln_b[l, 1]),
                          ffn_b, row(ln_g[l, 2]), row(ln_b[l, 2]), alpha, l)
    return h.reshape(bsz, s_len, d_model)
```

```python
import functools
import math

import numpy as np
import jax
import jax.numpy as jnp
from jax import lax
from jax.experimental import pallas as pl
from jax.experimental.pallas import tpu as pltpu

F32 = jnp.float32
BF16 = jnp.bfloat16

HEAD_DIM = 64
A_HEADS = 6
B_HEADS = 6
C_HEADS = 4
A_QK_DIM = 32
A_WIDTH = A_HEADS * HEAD_DIM
B_WIDTH = B_HEADS * HEAD_DIM
C_WIDTH = C_HEADS * HEAD_DIM
C_W_RANK = 64
C_A_RANK = 64
C_V_RANK = 32
C_G_RANK = 128
C_COLS = 3 * C_WIDTH + C_W_RANK + C_A_RANK + C_G_RANK
DILATED_PATTERNS = ((128, 1), (512, 4), (2048, 16))
ROPE_THETA = 10000.0
LN_EPS = 1e-5
RMS_EPS = 1e-5
C_GN_EPS = 64e-5
NEG_INF = -1e30
DECAY_RATE = math.exp(-0.5)
LOG2_E = math.log2(math.e)

LANES = 128
SUBLANES = 8
VMEM_LIMIT_BYTES = 56 * 1024 * 1024

ROW_TILE = 512
FF_CHUNK = 256
ATT_BLOCK = 512
RWKV_CHUNK = 64
RWKV_CHUNKS_PER_STEP = 4
SOLVE_BLOCK = 16

_NN = (((1,), (0,)), ((), ()))
_NT = (((1,), (1,)), ((), ()))


def _dot(a, b, dims=_NN):
    return lax.dot_general(a, b, dims, preferred_element_type=F32)


def _split2(x):
    hi = x.astype(BF16)
    lo = (x - hi.astype(F32)).astype(BF16)
    return hi, lo


def _split3(x):
    hi = x.astype(BF16)
    r = x - hi.astype(F32)
    mid = r.astype(BF16)
    lo = (r - mid.astype(F32)).astype(BF16)
    return hi, mid, lo


def _dot2l(a, b, dims=_NN):
    a_hi, a_lo = _split2(a)
    b_hi = b.astype(BF16)
    return _dot(a_lo, b_hi, dims) + _dot(a_hi, b_hi, dims)


def _dot2r(a, b, dims=_NN):
    a_hi = a.astype(BF16)
    b_hi, b_lo = _split2(b)
    return _dot(a_hi, b_lo, dims) + _dot(a_hi, b_hi, dims)


def _dot1(a, b, dims=_NN):
    return _dot(a.astype(BF16), b.astype(BF16), dims)


def _dot_sel_lhs(sel, x, dims=_NN):
    h, m, l = _split3(x)
    return _dot(sel, l, dims) + _dot(sel, m, dims) + _dot(sel, h, dims)


def _dot_sel_rhs(x, sel):
    h, l = _split2(x)
    return _dot(l, sel) + _dot(h, sel)


def _layer_norm(y, g, b):
    mu = jnp.mean(y, axis=-1, keepdims=True)
    d = y - mu
    var = jnp.mean(d * d, axis=-1, keepdims=True)
    return d * lax.rsqrt(var + LN_EPS) * g + b


def _cparams(sem):
    return pltpu.CompilerParams(dimension_semantics=sem, vmem_limit_bytes=VMEM_LIMIT_BYTES)


def _resident(shape):
    nd = len(shape)
    return pl.BlockSpec(shape, lambda *_: (0,) * nd, pipeline_mode=pl.Buffered(1))


def _resident_layer(shape, layer):
    nd = len(shape)
    return pl.BlockSpec((None,) + tuple(shape), lambda *_: (layer,) + (0,) * nd,
                        pipeline_mode=pl.Buffered(1))


def _ffn_ln_rows(x, wg_ref, wu_ref, wd_ref, g, b, alpha):
    d_ff = wg_ref.shape[1]
    xb = x.astype(BF16)
    acc = jnp.zeros(x.shape, F32)
    for c0 in range(0, d_ff, FF_CHUNK):
        gate = _dot(xb, wg_ref[:, c0:c0 + FF_CHUNK])
        up = _dot(xb, wu_ref[:, c0:c0 + FF_CHUNK])
        h = (gate * jax.nn.sigmoid(gate)) * up
        acc = acc + _dot(h.astype(BF16), wd_ref[c0:c0 + FF_CHUNK, :])
    return _layer_norm(alpha * x + 0.5 * acc, g, b)


def _ffn_specs(d, d_ff, layer):
    return [_resident_layer((d, d_ff), layer), _resident_layer((d, d_ff), layer),
            _resident_layer((d_ff, d), layer), _resident((1, d)), _resident((1, d))]


def _rope_lanes(y, cos, sin_signed, group):
    half = group // 2
    lane = lax.broadcasted_iota(jnp.int32, (1, LANES), 1)
    first = (lane % group) < half
    swapped = jnp.where(first, pltpu.roll(y, LANES - half, 1), pltpu.roll(y, half, 1))
    return y * cos + swapped * sin_signed


def _in_proj_rows(x, w_ref, wvt_ref, ca_ref, sa_ref, cb_ref, sb_ref,
                  aq_ref, ak_ref, avt_ref, bq_ref, bk_ref, bvt_ref, cc_ref):
    xb = x.astype(BF16)
    ca, sa, cb, sb = ca_ref[...], sa_ref[...], cb_ref[...], sb_ref[...]
    a_scale = A_QK_DIM ** -0.5 * LOG2_E
    b_scale = HEAD_DIM ** -0.5 * LOG2_E
    plan = ((aq_ref, A_WIDTH, (ca, sa, A_QK_DIM), a_scale),
            (ak_ref, A_WIDTH, (ca, sa, A_QK_DIM), None),
            (bq_ref, B_WIDTH, (cb, sb, HEAD_DIM), b_scale),
            (bk_ref, B_WIDTH, (cb, sb, HEAD_DIM), None))
    segs = [(ref, s0, rope, scale) for ref, width, rope, scale in plan for s0 in range(0, width, LANES)]
    wide = 2 * LANES
    for c0 in range(0, len(segs) * LANES, wide):
        y = _dot(xb, w_ref[:, c0:c0 + wide])
        for j in range(wide // LANES):
            ref, s0, rope, scale = segs[c0 // LANES + j]
            z = _rope_lanes(y[:, j * LANES:(j + 1) * LANES], *rope)
            if scale is not None:
                z = z * scale
            ref[:, s0:s0 + LANES] = z.astype(ref.dtype)
    col = len(segs) * LANES
    for s0 in range(0, C_COLS, wide):
        cc_ref[:, s0:s0 + wide] = _dot(xb, w_ref[:, col + s0:col + s0 + wide])
    avt_ref[...] = _dot(wvt_ref[0:A_WIDTH, :], xb, _NT).astype(avt_ref.dtype)
    bvt_ref[...] = _dot(wvt_ref[A_WIDTH:A_WIDTH + B_WIDTH, :], xb, _NT).astype(bvt_ref.dtype)


def _ffn_in_proj_kernel(x_ref, wg_ref, wu_ref, wd_ref, g_ref, b_ref, w_ref, wvt_ref,
                        ca_ref, sa_ref, cb_ref, sb_ref, h_ref, *out_refs, alpha):
    h = _ffn_ln_rows(x_ref[...], wg_ref, wu_ref, wd_ref, g_ref[...], b_ref[...], alpha)
    h_ref[...] = h
    _in_proj_rows(h, w_ref, wvt_ref, ca_ref, sa_ref, cb_ref, sb_ref, *out_refs)


def _ffn_in_proj(x, ffn_w, g, b, w_in, w_vt, tabs, s_len, alpha, layer):
    t, d = x.shape
    d_ff = ffn_w[0].shape[2]
    tm = min(ROW_TILE, s_len)
    spb = s_len // tm
    row = lambda i: (i, 0)
    tab = pl.BlockSpec((tm, LANES), lambda i: (i % spb, 0))
    qk = jax.ShapeDtypeStruct((t, A_WIDTH), BF16)
    vt = jax.ShapeDtypeStruct((A_WIDTH, t), BF16)
    qk_spec = pl.BlockSpec((tm, A_WIDTH), row)
    vt_spec = pl.BlockSpec((A_WIDTH, tm), lambda i: (0, i))
    return pl.pallas_call(
        functools.partial(_ffn_in_proj_kernel, alpha=alpha),
        out_shape=(jax.ShapeDtypeStruct((t, d), F32), qk, qk, vt, qk, qk, vt,
                   jax.ShapeDtypeStruct((t, C_COLS), F32)),
        grid=(t // tm,),
        in_specs=[pl.BlockSpec((tm, d), row)] + _ffn_specs(d, d_ff, layer)
        + [_resident_layer(w_in.shape[1:], layer), _resident_layer(w_vt.shape[1:], layer), tab, tab, tab, tab],
        out_specs=(pl.BlockSpec((tm, d), row), qk_spec, qk_spec, vt_spec, qk_spec, qk_spec, vt_spec,
                   pl.BlockSpec((tm, C_COLS), row)),
        compiler_params=_cparams(("parallel",)),
        name="ffn_in_proj",
    )(x, *ffn_w, g, b, w_in, w_vt, *tabs)


def _rope_tables(s_len, group):
    half = group // 2
    inv = ROPE_THETA ** (-jnp.arange(0, group, 2, dtype=F32) / group)
    ang = jnp.arange(s_len, dtype=F32)[:, None] * inv[None, :]
    lane = np.arange(LANES)
    idx = lane % half
    sign = np.where((lane % group) < half, -1.0, 1.0).astype(np.float32)
    cos = jnp.cos(ang)[:, idx]
    sin = jnp.sin(ang)[:, idx] * sign[None, :]
    return cos, sin


V_ROWS = HEAD_DIM + 16
DIFF_GROUPS = 4
DIL_GROUPS = 2


def _attn_two_pass(q_stack_ref, k_ref, vt_ref, s_refs, acc_ref, qi, blk, n_comb, n_groups,
                   off_bias, diag_bias):
    gs = blk // n_groups
    per_head = n_comb // 2
    gcols = n_comb * gs

    def cols(g):
        return slice(g * gcols, (g + 1) * gcols)

    def tiled(bias):
        return jnp.concatenate([bias] * n_comb, axis=1)

    def k_at(kb, size):
        return k_ref[0, kb * blk:kb * blk + size, :]

    def vx_at(h, kb, size):
        v = vt_ref[h * HEAD_DIM:(h + 1) * HEAD_DIM, kb * blk:kb * blk + size]
        return jnp.concatenate([v, jnp.ones((V_ROWS - HEAD_DIM, size), v.dtype)], axis=0)

    def col_max(m, s):
        part = jnp.max(s.reshape(s.shape[0] // SUBLANES, SUBLANES, s.shape[1]), axis=0)
        return part if m is None else jnp.maximum(m, part)

    mx = [None] * n_groups
    m = [None] * n_groups
    acc = [[None, None] for _ in range(n_groups)]

    def pass_a(g, kb):
        s = _dot(k_at(kb, blk), q_stack_ref[cols(g), :], _NT)
        bias = off_bias(kb)
        if bias is not None:
            s = s + tiled(bias[:, g * gs:(g + 1) * gs])
        s_refs[g][kb] = s
        mx[g] = col_max(mx[g], s)

    def pass_a_diag(g):
        nk = (g + 1) * gs
        s = _dot(k_at(qi, nk), q_stack_ref[cols(g), :], _NT)
        bias = diag_bias(g)
        if bias is not None:
            s = s + tiled(bias)
        s_refs[g][qi, 0:nk, :] = s
        m[g] = jnp.max(col_max(mx[g], s), axis=0, keepdims=True)

    def pass_b(g, kb, nk):
        p = jnp.exp2(s_refs[g][kb, 0:nk, :] - m[g]).astype(BF16)
        for h in range(2):
            d = _dot(vx_at(h, kb, nk), p[:, h * per_head * gs:(h + 1) * per_head * gs])
            acc[g][h] = d if acc[g][h] is None else acc[g][h] + d

    for kb in range(qi):
        pass_a(0, kb)
    pass_a_diag(0)
    for g in range(1, n_groups):
        for kb in range(qi):
            pass_a(g, kb)
            pass_b(g - 1, kb, blk)
        pass_a_diag(g)
        pass_b(g - 1, qi, g * gs)
    for kb in range(qi):
        pass_b(n_groups - 1, kb, blk)
    pass_b(n_groups - 1, qi, blk)
    for h in range(2):
        acc_ref[h] = jnp.concatenate([acc[g][h] for g in range(n_groups)], axis=1)


def _for_each_query_block(qi, n_q, fn):
    for n in range(n_q):
        @pl.when(qi == n)
        def _(n=n):
            fn(n)


def _normalized(acc_ref, c, n_comb, n_groups):
    per_head = n_comb // 2
    h, j = divmod(c, per_head)
    blk = acc_ref.shape[2] // per_head
    gs = blk // n_groups
    acc = acc_ref[h]
    acc = jnp.concatenate([acc[:, (g * per_head + j) * gs:(g * per_head + j + 1) * gs]
                           for g in range(n_groups)], axis=1)
    return acc[0:HEAD_DIM] / acc[HEAD_DIM:HEAD_DIM + 1]


def _store_q_stack(q_stack_ref, q, lane_masks, n_groups):
    gs = q.shape[0] // n_groups
    zero = jnp.zeros_like(q)
    n_comb = len(lane_masks)
    for g in range(n_groups):
        for c, mask in enumerate(lane_masks):
            r0 = (g * n_comb + c) * gs
            q_stack_ref[r0:r0 + gs, :] = jnp.where(mask, q, zero)[g * gs:(g + 1) * gs]


def _head_rms_t(o_t, g_t):
    sq = o_t * o_t
    ms = [jnp.sum(sq[h * HEAD_DIM:(h + 1) * HEAD_DIM], axis=0, keepdims=True) * (1.0 / HEAD_DIM)
          for h in range(2)]
    scale = jnp.concatenate([jnp.broadcast_to(lax.rsqrt(m + RMS_EPS), (HEAD_DIM, o_t.shape[1])) for m in ms],
                            axis=0)
    return o_t * scale * g_t


def _diff_attn_kernel(lam_ref, gt_ref, q_ref, k_ref, vt_ref, o_ref, q4_ref, acc_ref, *s_refs,
                      blk, lam_init):
    qi = pl.program_id(2)
    n_groups = DIFF_GROUPS
    gs = blk // n_groups
    lane = lax.broadcasted_iota(jnp.int32, (1, LANES), 1)
    _store_q_stack(q4_ref, q_ref[0], [(lane // A_QK_DIM) == c for c in range(4)], n_groups)

    def causal(g):
        key = lax.broadcasted_iota(jnp.int32, ((g + 1) * gs, gs), 0)
        qry = lax.broadcasted_iota(jnp.int32, ((g + 1) * gs, gs), 1) + g * gs
        return jnp.where(key <= qry, 0.0, NEG_INF)

    _for_each_query_block(qi, k_ref.shape[1] // blk, lambda n: _attn_two_pass(
        q4_ref, k_ref, vt_ref, s_refs, acc_ref, n, blk, 4, n_groups, lambda kb: None, causal))

    lp = lam_ref[...]
    lam = (jnp.exp(jnp.sum(lp[0:1] * lp[1:2], axis=1, keepdims=True))
           - jnp.exp(jnp.sum(lp[2:3] * lp[3:4], axis=1, keepdims=True)) + lam_init)
    o_t = jnp.concatenate([_normalized(acc_ref, 2 * h, 4, n_groups)
                           - lam * _normalized(acc_ref, 2 * h + 1, 4, n_groups)
                           for h in range(2)], axis=0)
    o_t = _head_rms_t(o_t, gt_ref[...]) * (1.0 - lam_init)
    o_ref[0] = o_t.T.astype(o_ref.dtype)


def _attn_call(kernel_fn, name, n_comb, n_groups, consts, const_specs, q, k, vt, blk):
    b, s, w = q.shape
    pairs = w // LANES
    return pl.pallas_call(
        kernel_fn,
        out_shape=jax.ShapeDtypeStruct((b, s, w), BF16),
        grid=(b, pairs, s // blk),
        in_specs=const_specs + [
            pl.BlockSpec((1, blk, LANES), lambda bi, p, i: (bi, i, p)),
            pl.BlockSpec((1, s, LANES), lambda bi, p, i: (bi, 0, p)),
            pl.BlockSpec((LANES, s), lambda bi, p, i: (p, bi)),
        ],
        out_specs=pl.BlockSpec((1, blk, LANES), lambda bi, p, i: (bi, i, p)),
        scratch_shapes=[pltpu.VMEM((n_comb * blk, LANES), BF16),
                        pltpu.VMEM((2, V_ROWS, n_comb // 2 * blk), F32)]
        + [pltpu.VMEM((s // blk, blk, n_comb * blk // n_groups), F32) for _ in range(n_groups)],
        compiler_params=_cparams(("parallel", "parallel", "arbitrary")),
        name=name,
    )(*consts, q, k, vt)


def _diff_attn(q, k, vt, lam_p, g_t, lam_init):
    blk = g_t.shape[1]
    return _attn_call(functools.partial(_diff_attn_kernel, blk=blk, lam_init=lam_init), "diff_attn", 4, DIFF_GROUPS,
                      [lam_p, g_t], [_resident((4, LANES)), _resident(g_t.shape)], q, k, vt, blk)


def _dilated_log2_counts(blk):
    far = max(w for w, _ in DILATED_PATTERNS if w < max(p[0] for p in DILATED_PATTERNS))
    n_off = far // blk + 2
    j = np.arange(blk)[:, None]
    i = np.arange(blk)[None, :]
    cnt = np.zeros((n_off, blk, blk), np.float64)
    for o in range(n_off):
        delta = o * blk + i - j
        for window, dil in DILATED_PATTERNS:
            cnt[o] += ((delta >= 0) & (delta <= window) & (delta % dil == 0))
    return np.where(cnt > 0, np.log2(np.maximum(cnt, 1.0)), NEG_INF).astype(np.float32)


def _dil_attn_kernel(bias_ref, gt_ref, q_ref, k_ref, vt_ref, o_ref, q2_ref, acc_ref, *s_refs,
                     blk, n_off):
    qi = pl.program_id(2)
    n_groups = DIL_GROUPS
    gs = blk // n_groups
    lane = lax.broadcasted_iota(jnp.int32, (1, LANES), 1)
    low = lane < HEAD_DIM
    _store_q_stack(q2_ref, q_ref[0], [low, jnp.logical_not(low)], n_groups)

    _for_each_query_block(qi, k_ref.shape[1] // blk, lambda n: _attn_two_pass(
        q2_ref, k_ref, vt_ref, s_refs, acc_ref, n, blk, 2, n_groups,
        lambda kb: bias_ref[min(n - kb, n_off - 1)],
        lambda g: bias_ref[0, 0:(g + 1) * gs, g * gs:(g + 1) * gs]))

    o_t = jnp.concatenate([_normalized(acc_ref, c, 2, n_groups) for c in range(2)], axis=0)
    o_ref[0] = _head_rms_t(o_t, gt_ref[...]).T.astype(o_ref.dtype)


def _dil_attn(q, k, vt, bias, g_t):
    blk = bias.shape[1]
    return _attn_call(functools.partial(_dil_attn_kernel, blk=blk, n_off=bias.shape[0]), "dil_attn", 2, DIL_GROUPS,
                      [bias, g_t], [_resident(bias.shape), _resident(g_t.shape)], q, k, vt, blk)


def _solve_unit_lower(a_list, x_list):
    n = len(x_list)
    c = x_list[0].shape[0]
    nb = c // SOLVE_BLOCK
    sub = SOLVE_BLOCK // SUBLANES
    a_split = [_split2(a[:, 0:c]) for a in a_list]
    xs = [[x[SOLVE_BLOCK * i:SOLVE_BLOCK * (i + 1)] for i in range(nb)] for x in x_list]
    zeros = jnp.zeros((SOLVE_BLOCK, LANES), F32)
    for b in range(nb):
        base = b * SOLVE_BLOCK
        rows = [[xs[u][b][SUBLANES * i:SUBLANES * (i + 1)] for i in range(sub)] for u in range(n)]
        for t in range(SOLVE_BLOCK - 1):
            bi = t // SUBLANES
            for u in range(n):
                xt = rows[u][bi][t % SUBLANES:t % SUBLANES + 1, :]
                for i in range(bi, sub):
                    r0 = base + SUBLANES * i
                    a_col = a_list[u][r0:r0 + SUBLANES, base + t:base + t + 1]
                    rows[u][i] = rows[u][i] - a_col * xt
        for u in range(n):
            xs[u][b] = jnp.concatenate(rows[u], axis=0)
        if b + 1 < nb:
            below = slice(base + SOLVE_BLOCK, c)
            for u in range(n):
                a_hi, a_lo = a_split[u]
                x_hi, x_lo = _split2(jnp.concatenate(
                    [zeros] * b + [xs[u][b]] + [zeros] * (nb - 1 - b), axis=0))
                upd = _dot(a_lo[below], x_hi) + _dot(a_hi[below], x_lo) + _dot(a_hi[below], x_hi)
                for j in range(b + 1, nb):
                    xs[u][j] = xs[u][j] - upd[(j - b - 1) * SOLVE_BLOCK:(j - b) * SOLVE_BLOCK]
    return [jnp.concatenate(x, axis=0) for x in xs]


def _to_slab(col_block, half, lower):
    lane = lax.broadcasted_iota(jnp.int32, (1, LANES), 1)
    want_low = lower
    have_low = half == 0
    src = col_block if want_low == have_low else pltpu.roll(col_block, HEAD_DIM, 1)
    keep = (lane < HEAD_DIM) if want_low else (lane >= HEAD_DIM)
    return jnp.where(keep, src, 0.0)


def _rwkv_kernel(*refs, n_chunks, has_vres):
    if has_vres:
        (cc_ref, vf_ref, mu_ref, w0_ref, w2_ref, a0_ref, a2_ref, g2_ref, kk_ref, ka_ref, rk_ref,
         gng_ref, gnb_ref, v0_ref, v1_ref, v2_ref, bd_ref, tri_ref,
         o_ref, carry_ref, hs_ref) = refs
    else:
        (cc_ref, mu_ref, w0_ref, w2_ref, a0_ref, a2_ref, g2_ref, kk_ref, ka_ref, rk_ref,
         gng_ref, gnb_ref, bd_ref, tri_ref,
         o_ref, vf_out_ref, carry_ref, hs_ref) = refs
    cw = C_WIDTH
    cl = RWKV_CHUNK

    @pl.when(pl.program_id(1) == 0)
    def _():
        carry_ref[...] = jnp.zeros(carry_ref.shape, F32)
        hs_ref[...] = jnp.zeros(hs_ref.shape, F32)

    c_raw = cc_ref[0]
    tb = c_raw.shape[0]
    row = lax.broadcasted_iota(jnp.int32, (tb, 1), 0)
    prev = jnp.where(row == 0, carry_ref[...], pltpu.roll(c_raw, 1, 0))
    carry_ref[...] = c_raw[tb - 1:tb, :]
    c = c_raw + (prev - c_raw) * mu_ref[...]

    r = c[:, 0:cw]
    k = c[:, cw:2 * cw]
    v = c[:, 2 * cw:3 * cw]
    xwa = c[:, 3 * cw:3 * cw + LANES]
    xg = c[:, 3 * cw + LANES:3 * cw + 2 * LANES]
    if has_vres:
        mix = jax.nn.sigmoid(v0_ref[...] + _dot2r(_dot2r(v, v1_ref[...]), v2_ref[...]))
        v = v + (vf_ref[0] - v) * mix
    else:
        vf_out_ref[0] = v
    bd = bd_ref[...]
    lw = -DECAY_RATE * jax.nn.sigmoid(w0_ref[...] + _dot1(jnp.tanh(xwa), w2_ref[...]))
    a = jax.nn.sigmoid(a0_ref[...] + _dot1(xwa, a2_ref[...]))
    gate = _dot2r(jax.nn.sigmoid(xg), g2_ref[...])
    kkv = k * kk_ref[...]
    kappa = kkv / jnp.maximum(jnp.sqrt(_dot_sel_rhs(kkv * kkv, bd)), 1e-12)
    k2 = k * (1.0 + (a - 1.0) * ka_ref[...])
    bonus = _dot_sel_rhs(r * k2 * rk_ref[...], bd)
    beta = a * kappa

    lane = lax.broadcasted_iota(jnp.int32, (1, LANES), 1)
    low = lane < HEAD_DIM
    lane2 = lax.broadcasted_iota(jnp.int32, (1, cw), 1)
    rr = lax.broadcasted_iota(jnp.int32, (2 * cl, LANES), 0)
    cidx = lax.broadcasted_iota(jnp.int32, (2 * cl, LANES), 1) % cl
    tri_mask = cidx < jnp.where(rr < cl, rr, rr - cl + 1)
    eye = (lax.broadcasted_iota(jnp.int32, (cl, LANES), 0)
           == lax.broadcasted_iota(jnp.int32, (cl, LANES), 1))
    tri = tri_ref[...]

    zeros = jnp.zeros((cl, LANES), F32)
    units = []
    for ci in range(n_chunks):
        sl = slice(ci * cl, (ci + 1) * cl)
        lwc = lw[sl]
        cum = _dot_sel_lhs(tri, lwc)
        cum_end = cum[cl - 1:cl, :]
        rt = r[sl] * jnp.exp(cum)
        kt = kappa[sl] * jnp.exp(cum - lwc)
        e_inv = jnp.exp(-cum)
        e_end = jnp.exp(cum_end - cum)
        bh = beta[sl] * e_end
        kh = k2[sl] * e_end
        p_end = jnp.exp(cum_end)
        vc = v[sl]
        right = jnp.concatenate([beta[sl] * e_inv, k2[sl] * e_inv], axis=0)
        for h in range(C_HEADS):
            cb, half = divmod(h, 2)
            cs = slice(cb * LANES, (cb + 1) * LANES)
            hm = (lane2 // HEAD_DIM) == h
            units.append(dict(
                ci=ci, h=h, right=right,
                left=jnp.concatenate([jnp.where(hm, kt, 0.0), jnp.where(hm, rt, 0.0)], axis=0),
                vw=_to_slab(vc[:, cs], half, lower=False),
                kw=_to_slab(kt[:, cs], half, lower=True),
                rw=_to_slab(rt[:, cs], half, lower=True),
                bkw=jnp.concatenate([_to_slab(bh[:, cs], half, True), _to_slab(kh[:, cs], half, True)], axis=0),
                pe=_to_slab(jnp.broadcast_to(p_end[:, cs], (cl, LANES)), half, True)))
    for u in units:
        u["m"] = jnp.where(tri_mask, _dot2l(u["left"], u["right"], _NT), 0.0)
    for u in units:
        u["rhs"] = u["kw"] + _dot1(u["m"][0:cl], jnp.concatenate([zeros, u["vw"]], axis=0))
    xs = _solve_unit_lower([u["m"][0:cl] for u in units], [u["rhs"] for u in units])
    for u, x in zip(units, xs):
        u["st"] = jnp.concatenate([-x, u["vw"]], axis=0)
    for u in units:
        u["qy"] = u["rw"] + _dot2r(u["m"][cl:2 * cl], u["st"])
    for u in units:
        u["gh"] = jnp.where(eye, u["pe"], 0.0) + _dot1(u["bkw"].T[0:cl], u["st"])
    for u in units:
        hs = hs_ref[u["h"]]
        u["y"] = u["qy"] + _dot1(u["qy"], hs)
        hs_ref[u["h"], 0:cl, :] = jnp.where(low, 0.0, u["gh"] + _dot2r(u["gh"], hs))
    y_chunks = []
    for ci in range(n_chunks):
        yh = [u["y"] for u in units if u["ci"] == ci]
        cols = [jnp.where(low, pltpu.roll(yh[2 * cb], HEAD_DIM, 1), yh[2 * cb + 1])
                for cb in range(C_HEADS // 2)]
        y_chunks.append(jnp.concatenate(cols, axis=1))
    y = jnp.concatenate(y_chunks, axis=0) if n_chunks > 1 else y_chunks[0]

    inv = 1.0 / HEAD_DIM
    mean = _dot_sel_rhs(y, bd) * inv
    d = y - mean
    var = _dot_sel_rhs(d * d, bd) * inv
    yn = d * lax.rsqrt(var + C_GN_EPS) * gng_ref[...] + gnb_ref[...]
    out = (yn + bonus * v) * gate
    o_ref[0] = out.astype(o_ref.dtype)


def _rwkv(cc, v_first, p, consts):
    b, s, _ = cc.shape
    n_chunks = RWKV_CHUNKS_PER_STEP
    tb = n_chunks * RWKV_CHUNK
    has_vres = v_first is not None
    blk_cc = pl.BlockSpec((1, tb, C_COLS), lambda bi, j: (bi, j, 0))
    blk_cw = pl.BlockSpec((1, tb, C_WIDTH), lambda bi, j: (bi, j, 0))
    vec = lambda n: _resident((1, n))
    params = [p["mu"], p["w0"], p["w2"], p["a0"], p["a2"], p["g2"], p["k_k"], p["k_a"], p["r_k"],
              p["gn_g"], p["gn_b"]]
    specs = [vec(C_COLS), vec(C_WIDTH), _resident((LANES, C_WIDTH)), vec(C_WIDTH),
             _resident((LANES, C_WIDTH)), _resident((LANES, C_WIDTH)), vec(C_WIDTH), vec(C_WIDTH),
             vec(C_WIDTH), vec(C_WIDTH), vec(C_WIDTH)]
    if has_vres:
        params += [p["v0"], p["v1"], p["v2"]]
        specs += [vec(C_WIDTH), _resident((C_WIDTH, LANES)), _resident((LANES, C_WIDTH))]
    params += [consts["bd"], consts["tri"]]
    specs += [_resident((C_WIDTH, C_WIDTH)), _resident((RWKV_CHUNK, RWKV_CHUNK))]
    o_shape = jax.ShapeDtypeStruct((b, s, C_WIDTH), BF16)
    if has_vres:
        args = [cc, v_first] + params
        in_specs = [blk_cc, blk_cw] + specs
        out_shape, out_specs = o_shape, blk_cw
    else:
        args = [cc] + params
        in_specs = [blk_cc] + specs
        out_shape = (o_shape, jax.ShapeDtypeStruct((b, s, C_WIDTH), F32))
        out_specs = (blk_cw, blk_cw)
    return pl.pallas_call(
        functools.partial(_rwkv_kernel, n_chunks=n_chunks, has_vres=has_vres),
        out_shape=out_shape,
        grid=(b, s // tb),
        in_specs=in_specs,
        out_specs=out_specs,
        scratch_shapes=[pltpu.VMEM((1, C_COLS), F32),
                        pltpu.VMEM((C_HEADS, 2 * RWKV_CHUNK, LANES), F32)],
        compiler_params=_cparams(("parallel", "arbitrary")),
        name="rwkv_vres" if has_vres else "rwkv_first",
    )(*args)


def _out_proj_ffn_kernel(x_ref, oa_ref, ob_ref, oc_ref, w_ref, g1_ref, b1_ref,
                         wg_ref, wu_ref, wd_ref, g2_ref, b2_ref, o_ref, *, alpha):
    mix = (_dot(oa_ref[...], w_ref[0:A_WIDTH, :])
           + _dot(ob_ref[...], w_ref[A_WIDTH:A_WIDTH + B_WIDTH, :])
           + _dot(oc_ref[...], w_ref[A_WIDTH + B_WIDTH:, :]))
    h = _layer_norm(alpha * x_ref[...] + mix, g1_ref[...], b1_ref[...])
    o_ref[...] = _ffn_ln_rows(h, wg_ref, wu_ref, wd_ref, g2_ref[...], b2_ref[...], alpha)


def _out_proj_ffn(x, oa, ob, oc, w_out, g1, b1, ffn_w, g2, b2, alpha, layer):
    t, d = x.shape
    d_ff = ffn_w[0].shape[2]
    tm = min(ROW_TILE, t)
    row = lambda i: (i, 0)
    return pl.pallas_call(
        functools.partial(_out_proj_ffn_kernel, alpha=alpha),
        out_shape=jax.ShapeDtypeStruct((t, d), F32),
        grid=(t // tm,),
        in_specs=[pl.BlockSpec((tm, d), row), pl.BlockSpec((tm, A_WIDTH), row),
                  pl.BlockSpec((tm, B_WIDTH), row), pl.BlockSpec((tm, C_WIDTH), row),
                  _resident_layer(w_out.shape[1:], layer), _resident((1, d)), _resident((1, d))]
        + _ffn_specs(d, d_ff, layer),
        out_specs=pl.BlockSpec((tm, d), row),
        compiler_params=_cparams(("parallel",)),
        name="out_proj_ffn",
    )(x, oa, ob, oc, w_out, g1, b1, *ffn_w, g2, b2)


def _pad_rows(w, total, offset):
    return jnp.zeros((total, w.shape[1]), w.dtype).at[offset:offset + w.shape[0]].set(w)


def kernel(x, ffn_a_gate, ffn_a_up, ffn_a_down, ffn_b_gate, ffn_b_up, ffn_b_down, ln_g, ln_b, w_in, w_out, a_lam_q1, a_lam_k1, a_lam_q2, a_lam_k2, a_norm_g, b_norm_g, c_mu, c_w0, c_w2, c_a0, c_a2, c_g2, c_k_k, c_k_a, c_r_k, c_gn_g, c_gn_b, c_v0, c_v1, c_v2):
    bsz, s_len, d_model = x.shape
    depth = w_in.shape[0]
    alpha = (2 * depth) ** 0.25
    t = bsz * s_len

    tabs = _rope_tables(s_len, A_QK_DIM) + _rope_tables(s_len, HEAD_DIM)
    dil_bias = jnp.asarray(_dilated_log2_counts(min(ATT_BLOCK, s_len)))
    head_of = np.arange(C_WIDTH) // HEAD_DIM
    consts = {
        "bd": jnp.asarray(head_of[:, None] == head_of[None, :], BF16),
        "tri": jnp.asarray(np.tril(np.ones((RWKV_CHUNK, RWKV_CHUNK), np.float32)), BF16),
    }
    row = lambda vct: vct.reshape(1, -1)
    att_blk = min(ATT_BLOCK, s_len)
    gain_t = lambda vct: jnp.broadcast_to(jnp.tile(vct, LANES // HEAD_DIM)[:, None], (LANES, att_blk))

    ffn_a = (ffn_a_gate.astype(BF16), ffn_a_up.astype(BF16), ffn_a_down.astype(BF16))
    ffn_b = (ffn_b_gate.astype(BF16), ffn_b_up.astype(BF16), ffn_b_down.astype(BF16))
    v_cols = np.r_[2 * A_WIDTH:3 * A_WIDTH, 3 * A_WIDTH + 2 * B_WIDTH:3 * A_WIDTH + 3 * B_WIDTH]
    w_in_b = jnp.delete(w_in, v_cols, axis=2, assume_unique_indices=True).astype(BF16)
    w_vt_b = jnp.swapaxes(w_in[:, :, v_cols], 1, 2).astype(BF16)
    w_out_b = w_out.astype(BF16)

    h = x.reshape(t, d_model)
    v_first = None
    for l in range(depth):
        h, aq, ak, avt, bq, bk, bvt, cc = _ffn_in_proj(h, ffn_a, row(ln_g[l, 0]), row(ln_b[l, 0]),
                                                       w_in_b, w_vt_b, tabs, s_len, alpha, l)
        shp = (bsz, s_len, A_WIDTH)

        lam_init = 0.8 - 0.6 * math.exp(-0.3 * l)
        lam_p = jnp.zeros((4, LANES), F32).at[:, :A_QK_DIM].set(
            jnp.stack([a_lam_q1[l], a_lam_k1[l], a_lam_q2[l], a_lam_k2[l]]))
        o_a = _diff_attn(aq.reshape(shp), ak.reshape(shp), avt, lam_p, gain_t(a_norm_g[l]), lam_init)
        o_b = _dil_attn(bq.reshape(shp), bk.reshape(shp), bvt, dil_bias, gain_t(b_norm_g[l]))

        p = {"mu": row(c_mu[l]), "w0": row(c_w0[l]), "a0": row(c_a0[l]),
             "w2": _pad_rows(c_w2[l], LANES, 0), "a2": _pad_rows(c_a2[l], LANES, C_W_RANK),
             "g2": c_g2[l], "k_k": row(c_k_k[l]), "k_a": row(c_k_a[l]), "r_k": row(c_r_k[l]),
             "gn_g": row(c_gn_g[l]), "gn_b": row(c_gn_b[l])}
        cc3 = cc.reshape(bsz, s_len, C_COLS)
        if l == 0:
            o_c, v_first = _rwkv(cc3, None, p, consts)
        else:
            p["v0"] = row(c_v0[l - 1])
            p["v1"] = jnp.zeros((C_WIDTH, LANES), F32).at[:, :C_V_RANK].set(c_v1[l - 1])
            p["v2"] = _pad_rows(c_v2[l - 1], LANES, 0)
            o_c = _rwkv(cc3, v_first, p, consts)

        h = _out_proj_ffn(h, o_a.reshape(t, A_WIDTH), o_b.reshape(t, B_WIDTH), o_c.reshape(t, C_WIDTH),
                          w_out_b, row(ln_g[l, 1]), row(ln_b[l, 1]),
                          ffn_b, row(ln_g[l, 2]), row(ln_b[l, 2]), alpha, l)
    return h.reshape(bsz, s_len, d_model)
```

```python
import functools
import math

import numpy as np
import jax
import jax.numpy as jnp
from jax import lax
from jax.experimental import pallas as pl
from jax.experimental.pallas import tpu as pltpu

F32 = jnp.float32
BF16 = jnp.bfloat16

HEAD_DIM = 64
A_HEADS = 6
B_HEADS = 6
C_HEADS = 4
A_QK_DIM = 32
A_WIDTH = A_HEADS * HEAD_DIM
B_WIDTH = B_HEADS * HEAD_DIM
C_WIDTH = C_HEADS * HEAD_DIM
C_W_RANK = 64
C_A_RANK = 64
C_V_RANK = 32
C_G_RANK = 128
C_COLS = 3 * C_WIDTH + C_W_RANK + C_A_RANK + C_G_RANK
DILATED_PATTERNS = ((128, 1), (512, 4), (2048, 16))
ROPE_THETA = 10000.0
LN_EPS = 1e-5
RMS_EPS = 1e-5
C_GN_EPS = 64e-5
NEG_INF = -1e30
DECAY_RATE = math.exp(-0.5)
LOG2_E = math.log2(math.e)

LANES = 128
SUBLANES = 8
VMEM_LIMIT_BYTES = 56 * 1024 * 1024

ROW_TILE = 512
FF_CHUNK = 256
ATT_BLOCK = 512
RWKV_CHUNK = 64
RWKV_CHUNKS_PER_STEP = 4
SOLVE_BLOCK = 16

_NN = (((1,), (0,)), ((), ()))
_NT = (((1,), (1,)), ((), ()))


def _dot(a, b, dims=_NN):
    return lax.dot_general(a, b, dims, preferred_element_type=F32)


def _split2(x):
    hi = x.astype(BF16)
    lo = (x - hi.astype(F32)).astype(BF16)
    return hi, lo


def _split3(x):
    hi = x.astype(BF16)
    r = x - hi.astype(F32)
    mid = r.astype(BF16)
    lo = (r - mid.astype(F32)).astype(BF16)
    return hi, mid, lo


def _dot2l(a, b, dims=_NN):
    a_hi, a_lo = _split2(a)
    b_hi = b.astype(BF16)
    return _dot(a_lo, b_hi, dims) + _dot(a_hi, b_hi, dims)


def _dot2r(a, b, dims=_NN):
    a_hi = a.astype(BF16)
    b_hi, b_lo = _split2(b)
    return _dot(a_hi, b_lo, dims) + _dot(a_hi, b_hi, dims)


def _dot1(a, b, dims=_NN):
    return _dot(a.astype(BF16), b.astype(BF16), dims)


def _dot_sel_lhs(sel, x, dims=_NN):
    h, m, l = _split3(x)
    return _dot(sel, l, dims) + _dot(sel, m, dims) + _dot(sel, h, dims)


def _dot_sel_rhs(x, sel):
    h, l = _split2(x)
    return _dot(l, sel) + _dot(h, sel)


def _layer_norm(y, g, b):
    mu = jnp.mean(y, axis=-1, keepdims=True)
    d = y - mu
    var = jnp.mean(d * d, axis=-1, keepdims=True)
    return d * lax.rsqrt(var + LN_EPS) * g + b


def _cparams(sem):
    return pltpu.CompilerParams(dimension_semantics=sem, vmem_limit_bytes=VMEM_LIMIT_BYTES)


def _resident(shape):
    nd = len(shape)
    return pl.BlockSpec(shape, lambda *_: (0,) * nd, pipeline_mode=pl.Buffered(1))


def _resident_layer(shape, layer):
    nd = len(shape)
    return pl.BlockSpec((None,) + tuple(shape), lambda *_: (layer,) + (0,) * nd,
                        pipeline_mode=pl.Buffered(1))


def _ffn_ln_rows(x, wg_ref, wu_ref, wd_ref, g, b, alpha):
    d_ff = wg_ref.shape[1]
    xb = x.astype(BF16)
    acc = jnp.zeros(x.shape, F32)
    for c0 in range(0, d_ff, FF_CHUNK):
        gate = _dot(xb, wg_ref[:, c0:c0 + FF_CHUNK])
        up = _dot(xb, wu_ref[:, c0:c0 + FF_CHUNK])
        h = (gate * jax.nn.sigmoid(gate)) * up
        acc = acc + _dot(h.astype(BF16), wd_ref[c0:c0 + FF_CHUNK, :])
    return _layer_norm(alpha * x + 0.5 * acc, g, b)


def _ffn_specs(d, d_ff, layer):
    return [_resident_layer((d, d_ff), layer), _resident_layer((d, d_ff), layer),
            _resident_layer((d_ff, d), layer), _resident((1, d)), _resident((1, d))]


def _rope_lanes(y, cos, sin_signed, group):
    half = group // 2
    lane = lax.broadcasted_iota(jnp.int32, (1, LANES), 1)
    first = (lane % group) < half
    swapped = jnp.where(first, pltpu.roll(y, LANES - half, 1), pltpu.roll(y, half, 1))
    return y * cos + swapped * sin_signed


def _in_proj_rows(x, w_ref, wvt_ref, ca_ref, sa_ref, cb_ref, sb_ref,
                  aq_ref, ak_ref, avt_ref, bq_ref, bk_ref, bvt_ref, cc_ref):
    xb = x.astype(BF16)
    ca, sa, cb, sb = ca_ref[...], sa_ref[...], cb_ref[...], sb_ref[...]
    a_scale = A_QK_DIM ** -0.5 * LOG2_E
    b_scale = HEAD_DIM ** -0.5 * LOG2_E
    b_col = 3 * A_WIDTH
    plan = ((aq_ref, 0, (ca, sa, A_QK_DIM), a_scale),
            (ak_ref, A_WIDTH, (ca, sa, A_QK_DIM), None),
            (bq_ref, b_col, (cb, sb, HEAD_DIM), b_scale),
            (bk_ref, b_col + B_WIDTH, (cb, sb, HEAD_DIM), None))
    segs = [(ref, s0, col + s0, rope, scale) for ref, col, rope, scale in plan
            for s0 in range(0, A_WIDTH, LANES)]
    wide = 2 * LANES
    for i in range(0, len(segs), wide // LANES):
        c0 = segs[i][2]
        y = _dot(xb, w_ref[:, c0:c0 + wide])
        for j in range(wide // LANES):
            ref, s0, _, rope, scale = segs[i + j]
            z = _rope_lanes(y[:, j * LANES:(j + 1) * LANES], *rope)
            if scale is not None:
                z = z * scale
            ref[:, s0:s0 + LANES] = z.astype(ref.dtype)
    col = b_col + 3 * B_WIDTH
    for s0 in range(0, C_COLS, wide):
        cc_ref[:, s0:s0 + wide] = _dot(xb, w_ref[:, col + s0:col + s0 + wide])
    avt_ref[...] = _dot(wvt_ref[0:A_WIDTH, :], xb, _NT).astype(avt_ref.dtype)
    bvt_ref[...] = _dot(wvt_ref[A_WIDTH:A_WIDTH + B_WIDTH, :], xb, _NT).astype(bvt_ref.dtype)


def _ffn_in_proj_kernel(x_ref, wg_ref, wu_ref, wd_ref, g_ref, b_ref, w_ref, wvt_ref,
                        ca_ref, sa_ref, cb_ref, sb_ref, h_ref, *out_refs, alpha):
    h = _ffn_ln_rows(x_ref[...], wg_ref, wu_ref, wd_ref, g_ref[...], b_ref[...], alpha)
    h_ref[...] = h
    _in_proj_rows(h, w_ref, wvt_ref, ca_ref, sa_ref, cb_ref, sb_ref, *out_refs)


def _ffn_in_proj(x, ffn_w, g, b, w_in, w_vt, tabs, s_len, alpha, layer):
    t, d = x.shape
    d_ff = ffn_w[0].shape[2]
    tm = min(ROW_TILE, s_len)
    spb = s_len // tm
    row = lambda i: (i, 0)
    tab = pl.BlockSpec((tm, LANES), lambda i: (i % spb, 0))
    qk = jax.ShapeDtypeStruct((t, A_WIDTH), BF16)
    vt = jax.ShapeDtypeStruct((A_WIDTH, t), BF16)
    qk_spec = pl.BlockSpec((tm, A_WIDTH), row)
    vt_spec = pl.BlockSpec((A_WIDTH, tm), lambda i: (0, i))
    return pl.pallas_call(
        functools.partial(_ffn_in_proj_kernel, alpha=alpha),
        out_shape=(jax.ShapeDtypeStruct((t, d), F32), qk, qk, vt, qk, qk, vt,
                   jax.ShapeDtypeStruct((t, C_COLS), F32)),
        grid=(t // tm,),
        in_specs=[pl.BlockSpec((tm, d), row)] + _ffn_specs(d, d_ff, layer)
        + [_resident_layer(w_in.shape[1:], layer), _resident_layer(w_vt.shape[1:], layer), tab, tab, tab, tab],
        out_specs=(pl.BlockSpec((tm, d), row), qk_spec, qk_spec, vt_spec, qk_spec, qk_spec, vt_spec,
                   pl.BlockSpec((tm, C_COLS), row)),
        compiler_params=_cparams(("parallel",)),
        name="ffn_in_proj",
    )(x, *ffn_w, g, b, w_in, w_vt, *tabs)


def _rope_tables(s_len, group):
    half = group // 2
    inv = ROPE_THETA ** (-jnp.arange(0, group, 2, dtype=F32) / group)
    ang = jnp.arange(s_len, dtype=F32)[:, None] * inv[None, :]
    lane = np.arange(LANES)
    idx = lane % half
    sign = np.where((lane % group) < half, -1.0, 1.0).astype(np.float32)
    cos = jnp.cos(ang)[:, idx]
    sin = jnp.sin(ang)[:, idx] * sign[None, :]
    return cos, sin


V_ROWS = HEAD_DIM + 16
DIFF_GROUPS = 4
DIL_GROUPS = 2


def _attn_two_pass(q_stack_ref, k_ref, vt_ref, s_refs, acc_ref, qi, blk, n_comb, n_groups,
                   off_bias, diag_bias):
    gs = blk // n_groups
    per_head = n_comb // 2
    gcols = n_comb * gs

    def cols(g):
        return slice(g * gcols, (g + 1) * gcols)

    def tiled(bias):
        return jnp.concatenate([bias] * n_comb, axis=1)

    def k_at(kb, size):
        return k_ref[0, kb * blk:kb * blk + size, :]

    vx_cache = {}

    def vx_at(h, kb, size):
        if (h, kb, size) not in vx_cache:
            v = vt_ref[h * HEAD_DIM:(h + 1) * HEAD_DIM, kb * blk:kb * blk + size]
            vx_cache[h, kb, size] = jnp.concatenate(
                [v, jnp.ones((V_ROWS - HEAD_DIM, size), v.dtype)], axis=0)
        return vx_cache[h, kb, size]

    def col_max(m, s):
        part = jnp.max(s.reshape(s.shape[0] // SUBLANES, SUBLANES, s.shape[1]), axis=0)
        return part if m is None else jnp.maximum(m, part)

    mx = [None] * n_groups
    m = [None] * n_groups
    acc = [[None, None] for _ in range(n_groups)]

    def pass_a(g, kb):
        s = _dot(k_at(kb, blk), q_stack_ref[cols(g), :], _NT)
        bias = off_bias(kb)
        if bias is not None:
            s = s + tiled(bias[:, g * gs:(g + 1) * gs])
        s_refs[g][kb] = s
        mx[g] = col_max(mx[g], s)

    def pass_a_diag(g):
        nk = (g + 1) * gs
        s = _dot(k_at(qi, nk), q_stack_ref[cols(g), :], _NT)
        bias = diag_bias(g)
        if bias is not None:
            s = s + tiled(bias)
        s_refs[g][qi, 0:nk, :] = s
        m[g] = jnp.max(col_max(mx[g], s), axis=0, keepdims=True)

    def keys_of(g, kb):
        return blk if kb < qi else (g + 1) * gs

    def probs(g, kb):
        return jnp.exp2(s_refs[g][kb, 0:keys_of(g, kb), :] - m[g]).astype(BF16)

    def pass_b(g, kb, p):
        for h in range(2):
            d = _dot(vx_at(h, kb, keys_of(g, kb)), p[:, h * per_head * gs:(h + 1) * per_head * gs])
            acc[g][h] = d if acc[g][h] is None else acc[g][h] + d

    for kb in range(qi):
        pass_a(0, kb)
    pass_a_diag(0)
    for g in range(1, n_groups + 1):
        p = probs(g - 1, 0)
        for kb in range(qi + 1):
            if g < n_groups and kb < qi:
                pass_a(g, kb)
            elif g < n_groups:
                pass_a_diag(g)
            p_next = probs(g - 1, kb + 1) if kb < qi else None
            pass_b(g - 1, kb, p)
            p = p_next
    for h in range(2):
        acc_ref[h] = jnp.concatenate([acc[g][h] for g in range(n_groups)], axis=1)


def _for_each_query_block(qi, n_q, fn):
    for n in range(n_q):
        @pl.when(qi == n)
        def _(n=n):
            fn(n)


def _normalized(acc_ref, c, n_comb, n_groups):
    per_head = n_comb // 2
    h, j = divmod(c, per_head)
    blk = acc_ref.shape[2] // per_head
    gs = blk // n_groups
    acc = acc_ref[h]
    acc = jnp.concatenate([acc[:, (g * per_head + j) * gs:(g * per_head + j + 1) * gs]
                           for g in range(n_groups)], axis=1)
    return acc[0:HEAD_DIM] / acc[HEAD_DIM:HEAD_DIM + 1]


def _store_q_stack(q_stack_ref, q, lane_masks, n_groups):
    gs = q.shape[0] // n_groups
    zero = jnp.zeros_like(q)
    n_comb = len(lane_masks)
    for g in range(n_groups):
        for c, mask in enumerate(lane_masks):
            r0 = (g * n_comb + c) * gs
            q_stack_ref[r0:r0 + gs, :] = jnp.where(mask, q, zero)[g * gs:(g + 1) * gs]


def _head_rms_t(o_t, g_t):
    sq = o_t * o_t
    ms = [jnp.sum(sq[h * HEAD_DIM:(h + 1) * HEAD_DIM], axis=0, keepdims=True) * (1.0 / HEAD_DIM)
          for h in range(2)]
    scale = jnp.concatenate([jnp.broadcast_to(lax.rsqrt(m + RMS_EPS), (HEAD_DIM, o_t.shape[1])) for m in ms],
                            axis=0)
    return o_t * scale * g_t


def _diff_attn_kernel(lam_ref, gt_ref, q_ref, k_ref, vt_ref, o_ref, q4_ref, acc_ref, *s_refs,
                      blk, lam_init):
    qi = pl.program_id(2)
    n_groups = DIFF_GROUPS
    gs = blk // n_groups
    lane = lax.broadcasted_iota(jnp.int32, (1, LANES), 1)
    _store_q_stack(q4_ref, q_ref[0], [(lane // A_QK_DIM) == c for c in range(4)], n_groups)

    def causal(g):
        key = lax.broadcasted_iota(jnp.int32, ((g + 1) * gs, gs), 0)
        qry = lax.broadcasted_iota(jnp.int32, ((g + 1) * gs, gs), 1) + g * gs
        return jnp.where(key <= qry, 0.0, NEG_INF)

    _for_each_query_block(qi, k_ref.shape[1] // blk, lambda n: _attn_two_pass(
        q4_ref, k_ref, vt_ref, s_refs, acc_ref, n, blk, 4, n_groups, lambda kb: None, causal))

    lp = lam_ref[...]
    lam = (jnp.exp(jnp.sum(lp[0:1] * lp[1:2], axis=1, keepdims=True))
           - jnp.exp(jnp.sum(lp[2:3] * lp[3:4], axis=1, keepdims=True)) + lam_init)
    o_t = jnp.concatenate([_normalized(acc_ref, 2 * h, 4, n_groups)
                           - lam * _normalized(acc_ref, 2 * h + 1, 4, n_groups)
                           for h in range(2)], axis=0)
    o_t = _head_rms_t(o_t, gt_ref[...]) * (1.0 - lam_init)
    o_ref[0] = o_t.T.astype(o_ref.dtype)


def _attn_call(kernel_fn, name, n_comb, n_groups, consts, const_specs, q, k, vt, blk):
    b, s, w = q.shape
    pairs = w // LANES
    return pl.pallas_call(
        kernel_fn,
        out_shape=jax.ShapeDtypeStruct((b, s, w), BF16),
        grid=(b, pairs, s // blk),
        in_specs=const_specs + [
            pl.BlockSpec((1, blk, LANES), lambda bi, p, i: (bi, i, p)),
            pl.BlockSpec((1, s, LANES), lambda bi, p, i: (bi, 0, p)),
            pl.BlockSpec((LANES, s), lambda bi, p, i: (p, bi)),
        ],
        out_specs=pl.BlockSpec((1, blk, LANES), lambda bi, p, i: (bi, i, p)),
        scratch_shapes=[pltpu.VMEM((n_comb * blk, LANES), BF16),
                        pltpu.VMEM((2, V_ROWS, n_comb // 2 * blk), F32)]
        + [pltpu.VMEM((s // blk, blk, n_comb * blk // n_groups), F32) for _ in range(n_groups)],
        compiler_params=_cparams(("parallel", "parallel", "arbitrary")),
        name=name,
    )(*consts, q, k, vt)


def _diff_attn(q, k, vt, lam_p, g_t, lam_init):
    blk = g_t.shape[1]
    return _attn_call(functools.partial(_diff_attn_kernel, blk=blk, lam_init=lam_init), "diff_attn", 4, DIFF_GROUPS,
                      [lam_p, g_t], [_resident((4, LANES)), _resident(g_t.shape)], q, k, vt, blk)


def _dilated_log2_counts(blk):
    far = max(w for w, _ in DILATED_PATTERNS if w < max(p[0] for p in DILATED_PATTERNS))
    n_off = far // blk + 2
    j = np.arange(blk)[:, None]
    i = np.arange(blk)[None, :]
    cnt = np.zeros((n_off, blk, blk), np.float64)
    for o in range(n_off):
        delta = o * blk + i - j
        for window, dil in DILATED_PATTERNS:
            cnt[o] += ((delta >= 0) & (delta <= window) & (delta % dil == 0))
    return np.where(cnt > 0, np.log2(np.maximum(cnt, 1.0)), NEG_INF).astype(np.float32)


def _dil_attn_kernel(bias_ref, gt_ref, q_ref, k_ref, vt_ref, o_ref, q2_ref, acc_ref, *s_refs,
                     blk, n_off):
    qi = pl.program_id(2)
    n_groups = DIL_GROUPS
    gs = blk // n_groups
    lane = lax.broadcasted_iota(jnp.int32, (1, LANES), 1)
    low = lane < HEAD_DIM
    _store_q_stack(q2_ref, q_ref[0], [low, jnp.logical_not(low)], n_groups)

    _for_each_query_block(qi, k_ref.shape[1] // blk, lambda n: _attn_two_pass(
        q2_ref, k_ref, vt_ref, s_refs, acc_ref, n, blk, 2, n_groups,
        lambda kb: bias_ref[min(n - kb, n_off - 1)],
        lambda g: bias_ref[0, 0:(g + 1) * gs, g * gs:(g + 1) * gs]))

    o_t = jnp.concatenate([_normalized(acc_ref, c, 2, n_groups) for c in range(2)], axis=0)
    o_ref[0] = _head_rms_t(o_t, gt_ref[...]).T.astype(o_ref.dtype)


def _dil_attn(q, k, vt, bias, g_t):
    blk = bias.shape[1]
    return _attn_call(functools.partial(_dil_attn_kernel, blk=blk, n_off=bias.shape[0]), "dil_attn", 2, DIL_GROUPS,
                      [bias, g_t], [_resident(bias.shape), _resident(g_t.shape)], q, k, vt, blk)


def _solve_unit_lower(a_list, x_list):
    n = len(x_list)
    c = x_list[0].shape[0]
    nb = c // SOLVE_BLOCK
    sub = SOLVE_BLOCK // SUBLANES
    a_split = [_split2(a[:, 0:c]) for a in a_list]
    xs = [[x[SOLVE_BLOCK * i:SOLVE_BLOCK * (i + 1)] for i in range(nb)] for x in x_list]
    zeros = jnp.zeros((SOLVE_BLOCK, LANES), F32)
    for b in range(nb):
        base = b * SOLVE_BLOCK
        rows = [[xs[u][b][SUBLANES * i:SUBLANES * (i + 1)] for i in range(sub)] for u in range(n)]
        for t in range(SOLVE_BLOCK - 1):
            bi = t // SUBLANES
            for u in range(n):
                xt = rows[u][bi][t % SUBLANES:t % SUBLANES + 1, :]
                for i in range(bi, sub):
                    r0 = base + SUBLANES * i
                    a_col = a_list[u][r0:r0 + SUBLANES, base + t:base + t + 1]
                    rows[u][i] = rows[u][i] - a_col * xt
        for u in range(n):
            xs[u][b] = jnp.concatenate(rows[u], axis=0)
        if b + 1 < nb:
            below = slice(base + SOLVE_BLOCK, c)
            for u in range(n):
                a_hi, a_lo = a_split[u]
                x_hi, x_lo = _split2(jnp.concatenate(
                    [zeros] * b + [xs[u][b]] + [zeros] * (nb - 1 - b), axis=0))
                upd = _dot(a_lo[below], x_hi) + _dot(a_hi[below], x_lo) + _dot(a_hi[below], x_hi)
                for j in range(b + 1, nb):
                    xs[u][j] = xs[u][j] - upd[(j - b - 1) * SOLVE_BLOCK:(j - b) * SOLVE_BLOCK]
    return [jnp.concatenate(x, axis=0) for x in xs]


def _to_slab(col_block, half, lower):
    lane = lax.broadcasted_iota(jnp.int32, (1, LANES), 1)
    want_low = lower
    have_low = half == 0
    src = col_block if want_low == have_low else pltpu.roll(col_block, HEAD_DIM, 1)
    keep = (lane < HEAD_DIM) if want_low else (lane >= HEAD_DIM)
    return jnp.where(keep, src, 0.0)


def _rwkv_kernel(*refs, n_chunks, has_vres):
    if has_vres:
        (cc_ref, vf_ref, mu_ref, w0_ref, w2_ref, a0_ref, a2_ref, g2_ref, kk_ref, ka_ref, rk_ref,
         gng_ref, gnb_ref, v0_ref, v1_ref, v2_ref, bd_ref, tri_ref,
         o_ref, carry_ref, hs_ref) = refs
    else:
        (cc_ref, mu_ref, w0_ref, w2_ref, a0_ref, a2_ref, g2_ref, kk_ref, ka_ref, rk_ref,
         gng_ref, gnb_ref, bd_ref, tri_ref,
         o_ref, vf_out_ref, carry_ref, hs_ref) = refs
    cw = C_WIDTH
    cl = RWKV_CHUNK

    @pl.when(pl.program_id(1) == 0)
    def _():
        carry_ref[...] = jnp.zeros(carry_ref.shape, F32)
        hs_ref[...] = jnp.zeros(hs_ref.shape, F32)

    c_raw = cc_ref[0]
    tb = c_raw.shape[0]
    row = lax.broadcasted_iota(jnp.int32, (tb, 1), 0)
    prev = jnp.where(row == 0, carry_ref[...], pltpu.roll(c_raw, 1, 0))
    carry_ref[...] = c_raw[tb - 1:tb, :]
    c = c_raw + (prev - c_raw) * mu_ref[...]

    r = c[:, 0:cw]
    k = c[:, cw:2 * cw]
    v = c[:, 2 * cw:3 * cw]
    xwa = c[:, 3 * cw:3 * cw + LANES]
    xg = c[:, 3 * cw + LANES:3 * cw + 2 * LANES]
    if has_vres:
        mix = jax.nn.sigmoid(v0_ref[...] + _dot2r(_dot2r(v, v1_ref[...]), v2_ref[...]))
        v = v + (vf_ref[0] - v) * mix
    else:
        vf_out_ref[0] = v
    bd = bd_ref[...]
    lw = -DECAY_RATE * jax.nn.sigmoid(w0_ref[...] + _dot1(jnp.tanh(xwa), w2_ref[...]))
    a = jax.nn.sigmoid(a0_ref[...] + _dot1(xwa, a2_ref[...]))
    gate = _dot2r(jax.nn.sigmoid(xg), g2_ref[...])
    kkv = k * kk_ref[...]
    kappa = kkv / jnp.maximum(jnp.sqrt(_dot_sel_rhs(kkv * kkv, bd)), 1e-12)
    k2 = k * (1.0 + (a - 1.0) * ka_ref[...])
    bonus = _dot_sel_rhs(r * k2 * rk_ref[...], bd)
    beta = a * kappa

    lane = lax.broadcasted_iota(jnp.int32, (1, LANES), 1)
    low = lane < HEAD_DIM
    lane2 = lax.broadcasted_iota(jnp.int32, (1, cw), 1)
    rr = lax.broadcasted_iota(jnp.int32, (2 * cl, LANES), 0)
    cidx = lax.broadcasted_iota(jnp.int32, (2 * cl, LANES), 1) % cl
    tri_mask = cidx < jnp.where(rr < cl, rr, rr - cl + 1)
    eye = (lax.broadcasted_iota(jnp.int32, (cl, LANES), 0)
           == lax.broadcasted_iota(jnp.int32, (cl, LANES), 1))
    tri = tri_ref[...]

    zeros = jnp.zeros((cl, LANES), F32)
    units = []
    for ci in range(n_chunks):
        sl = slice(ci * cl, (ci + 1) * cl)
        lwc = lw[sl]
        cum = _dot_sel_lhs(tri, lwc)
        cum_end = cum[cl - 1:cl, :]
        rt = r[sl] * jnp.exp(cum)
        kt = kappa[sl] * jnp.exp(cum - lwc)
        e_inv = jnp.exp(-cum)
        e_end = jnp.exp(cum_end - cum)
        bh = beta[sl] * e_end
        kh = k2[sl] * e_end
        p_end = jnp.exp(cum_end)
        vc = v[sl]
        right = jnp.concatenate([beta[sl] * e_inv, k2[sl] * e_inv], axis=0)
        for h in range(C_HEADS):
            cb, half = divmod(h, 2)
            cs = slice(cb * LANES, (cb + 1) * LANES)
            hm = (lane2 // HEAD_DIM) == h
            units.append(dict(
                ci=ci, h=h, right=right,
                left=jnp.concatenate([jnp.where(hm, kt, 0.0), jnp.where(hm, rt, 0.0)], axis=0),
                vw=_to_slab(vc[:, cs], half, lower=False),
                kw=_to_slab(kt[:, cs], half, lower=True),
                rw=_to_slab(rt[:, cs], half, lower=True),
                bkw=jnp.concatenate([_to_slab(bh[:, cs], half, True), _to_slab(kh[:, cs], half, True)], axis=0),
                pe=_to_slab(jnp.broadcast_to(p_end[:, cs], (cl, LANES)), half, True)))
    for u in units:
        u["m"] = jnp.where(tri_mask, _dot2l(u["left"], u["right"], _NT), 0.0)
    for u in units:
        u["rhs"] = u["kw"] + _dot1(u["m"][0:cl], jnp.concatenate([zeros, u["vw"]], axis=0))
    xs = _solve_unit_lower([u["m"][0:cl] for u in units], [u["rhs"] for u in units])
    for u, x in zip(units, xs):
        u["st"] = jnp.concatenate([-x, u["vw"]], axis=0)
    for u in units:
        u["qy"] = u["rw"] + _dot2r(u["m"][cl:2 * cl], u["st"])
    for u in units:
        u["gh"] = jnp.where(eye, u["pe"], 0.0) + _dot1(u["bkw"].T[0:cl], u["st"])
    for u in units:
        hs = hs_ref[u["h"]]
        u["y"] = u["qy"] + _dot1(u["qy"], hs)
        hs_ref[u["h"], 0:cl, :] = jnp.where(low, 0.0, u["gh"] + _dot2r(u["gh"], hs))
    y_chunks = []
    for ci in range(n_chunks):
        yh = [u["y"] for u in units if u["ci"] == ci]
        cols = [jnp.where(low, pltpu.roll(yh[2 * cb], HEAD_DIM, 1), yh[2 * cb + 1])
                for cb in range(C_HEADS // 2)]
        y_chunks.append(jnp.concatenate(cols, axis=1))
    y = jnp.concatenate(y_chunks, axis=0) if n_chunks > 1 else y_chunks[0]

    inv = 1.0 / HEAD_DIM
    mean = _dot_sel_rhs(y, bd) * inv
    d = y - mean
    var = _dot_sel_rhs(d * d, bd) * inv
    yn = d * lax.rsqrt(var + C_GN_EPS) * gng_ref[...] + gnb_ref[...]
    out = (yn + bonus * v) * gate
    o_ref[0] = out.astype(o_ref.dtype)


def _rwkv(cc, v_first, p, consts):
    b, s, _ = cc.shape
    n_chunks = RWKV_CHUNKS_PER_STEP
    tb = n_chunks * RWKV_CHUNK
    has_vres = v_first is not None
    blk_cc = pl.BlockSpec((1, tb, C_COLS), lambda bi, j: (bi, j, 0))
    blk_cw = pl.BlockSpec((1, tb, C_WIDTH), lambda bi, j: (bi, j, 0))
    vec = lambda n: _resident((1, n))
    params = [p["mu"], p["w0"], p["w2"], p["a0"], p["a2"], p["g2"], p["k_k"], p["k_a"], p["r_k"],
              p["gn_g"], p["gn_b"]]
    specs = [vec(C_COLS), vec(C_WIDTH), _resident((LANES, C_WIDTH)), vec(C_WIDTH),
             _resident((LANES, C_WIDTH)), _resident((LANES, C_WIDTH)), vec(C_WIDTH), vec(C_WIDTH),
             vec(C_WIDTH), vec(C_WIDTH), vec(C_WIDTH)]
    if has_vres:
        params += [p["v0"], p["v1"], p["v2"]]
        specs += [vec(C_WIDTH), _resident((C_WIDTH, LANES)), _resident((LANES, C_WIDTH))]
    params += [consts["bd"], consts["tri"]]
    specs += [_resident((C_WIDTH, C_WIDTH)), _resident((RWKV_CHUNK, RWKV_CHUNK))]
    o_shape = jax.ShapeDtypeStruct((b, s, C_WIDTH), BF16)
    if has_vres:
        args = [cc, v_first] + params
        in_specs = [blk_cc, blk_cw] + specs
        out_shape, out_specs = o_shape, blk_cw
    else:
        args = [cc] + params
        in_specs = [blk_cc] + specs
        out_shape = (o_shape, jax.ShapeDtypeStruct((b, s, C_WIDTH), F32))
        out_specs = (blk_cw, blk_cw)
    return pl.pallas_call(
        functools.partial(_rwkv_kernel, n_chunks=n_chunks, has_vres=has_vres),
        out_shape=out_shape,
        grid=(b, s // tb),
        in_specs=in_specs,
        out_specs=out_specs,
        scratch_shapes=[pltpu.VMEM((1, C_COLS), F32),
                        pltpu.VMEM((C_HEADS, 2 * RWKV_CHUNK, LANES), F32)],
        compiler_params=_cparams(("parallel", "arbitrary")),
        name="rwkv_vres" if has_vres else "rwkv_first",
    )(*args)


def _out_proj_ffn_kernel(x_ref, oa_ref, ob_ref, oc_ref, w_ref, g1_ref, b1_ref,
                         wg_ref, wu_ref, wd_ref, g2_ref, b2_ref, o_ref, *, alpha):
    mix = (_dot(oa_ref[...], w_ref[0:A_WIDTH, :])
           + _dot(ob_ref[...], w_ref[A_WIDTH:A_WIDTH + B_WIDTH, :])
           + _dot(oc_ref[...], w_ref[A_WIDTH + B_WIDTH:, :]))
    h = _layer_norm(alpha * x_ref[...] + mix, g1_ref[...], b1_ref[...])
    o_ref[...] = _ffn_ln_rows(h, wg_ref, wu_ref, wd_ref, g2_ref[...], b2_ref[...], alpha)


def _out_proj_ffn(x, oa, ob, oc, w_out, g1, b1, ffn_w, g2, b2, alpha, layer):
    t, d = x.shape
    d_ff = ffn_w[0].shape[2]
    tm = min(ROW_TILE, t)
    row = lambda i: (i, 0)
    return pl.pallas_call(
        functools.partial(_out_proj_ffn_kernel, alpha=alpha),
        out_shape=jax.ShapeDtypeStruct((t, d), F32),
        grid=(t // tm,),
        in_specs=[pl.BlockSpec((tm, d), row), pl.BlockSpec((tm, A_WIDTH), row),
                  pl.BlockSpec((tm, B_WIDTH), row), pl.BlockSpec((tm, C_WIDTH), row),
                  _resident_layer(w_out.shape[1:], layer), _resident((1, d)), _resident((1, d))]
        + _ffn_specs(d, d_ff, layer),
        out_specs=pl.BlockSpec((tm, d), row),
        compiler_params=_cparams(("parallel",)),
        name="out_proj_ffn",
    )(x, oa, ob, oc, w_out, g1, b1, *ffn_w, g2, b2)


def _pad_rows(w, total, offset):
    return jnp.zeros((total, w.shape[1]), w.dtype).at[offset:offset + w.shape[0]].set(w)


def kernel(x, ffn_a_gate, ffn_a_up, ffn_a_down, ffn_b_gate, ffn_b_up, ffn_b_down, ln_g, ln_b, w_in, w_out, a_lam_q1, a_lam_k1, a_lam_q2, a_lam_k2, a_norm_g, b_norm_g, c_mu, c_w0, c_w2, c_a0, c_a2, c_g2, c_k_k, c_k_a, c_r_k, c_gn_g, c_gn_b, c_v0, c_v1, c_v2):
    bsz, s_len, d_model = x.shape
    depth = w_in.shape[0]
    alpha = (2 * depth) ** 0.25
    t = bsz * s_len

    tabs = _rope_tables(s_len, A_QK_DIM) + _rope_tables(s_len, HEAD_DIM)
    dil_bias = jnp.asarray(_dilated_log2_counts(min(ATT_BLOCK, s_len)))
    head_of = np.arange(C_WIDTH) // HEAD_DIM
    consts = {
        "bd": jnp.asarray(head_of[:, None] == head_of[None, :], BF16),
        "tri": jnp.asarray(np.tril(np.ones((RWKV_CHUNK, RWKV_CHUNK), np.float32)), BF16),
    }
    row = lambda vct: vct.reshape(1, -1)
    att_blk = min(ATT_BLOCK, s_len)
    gain_t = lambda vct: jnp.broadcast_to(jnp.tile(vct, LANES // HEAD_DIM)[:, None], (LANES, att_blk))

    ffn_a = (ffn_a_gate.astype(BF16), ffn_a_up.astype(BF16), ffn_a_down.astype(BF16))
    ffn_b = (ffn_b_gate.astype(BF16), ffn_b_up.astype(BF16), ffn_b_down.astype(BF16))
    w_in_b = w_in.astype(BF16)
    v_cols = np.r_[2 * A_WIDTH:3 * A_WIDTH, 3 * A_WIDTH + 2 * B_WIDTH:3 * A_WIDTH + 3 * B_WIDTH]
    w_vt_b = jnp.swapaxes(w_in_b[:, :, v_cols], 1, 2)
    w_out_b = w_out.astype(BF16)

    h = x.reshape(t, d_model)
    v_first = None
    for l in range(depth):
        h, aq, ak, avt, bq, bk, bvt, cc = _ffn_in_proj(h, ffn_a, row(ln_g[l, 0]), row(ln_b[l, 0]),
                                                       w_in_b, w_vt_b, tabs, s_len, alpha, l)
        shp = (bsz, s_len, A_WIDTH)

        lam_init = 0.8 - 0.6 * math.exp(-0.3 * l)
        lam_p = jnp.zeros((4, LANES), F32).at[:, :A_QK_DIM].set(
            jnp.stack([a_lam_q1[l], a_lam_k1[l], a_lam_q2[l], a_lam_k2[l]]))
        o_a = _diff_attn(aq.reshape(shp), ak.reshape(shp), avt, lam_p, gain_t(a_norm_g[l]), lam_init)
        o_b = _dil_attn(bq.reshape(shp), bk.reshape(shp), bvt, dil_bias, gain_t(b_norm_g[l]))

        p = {"mu": row(c_mu[l]), "w0": row(c_w0[l]), "a0": row(c_a0[l]),
             "w2": _pad_rows(c_w2[l], LANES, 0), "a2": _pad_rows(c_a2[l], LANES, C_W_RANK),
             "g2": c_g2[l], "k_k": row(c_k_k[l]), "k_a": row(c_k_a[l]), "r_k": row(c_r_k[l]),
             "gn_g": row(c_gn_g[l]), "gn_b": row(c_gn_b[l])}
        cc3 = cc.reshape(bsz, s_len, C_COLS)
        if l == 0:
            o_c, v_first = _rwkv(cc3, None, p, consts)
        else:
            p["v0"] = row(c_v0[l - 1])
            p["v1"] = jnp.zeros((C_WIDTH, LANES), F32).at[:, :C_V_RANK].set(c_v1[l - 1])
            p["v2"] = _pad_rows(c_v2[l - 1], LANES, 0)
            o_c = _rwkv(cc3, v_first, p, consts)

        h = _out_proj_ffn(h, o_a.reshape(t, A_WIDTH), o_b.reshape(t, B_WIDTH), o_c.reshape(t, C_WIDTH),
                          w_out_b, row(ln_g[l, 1]), row(ln_b[l, 1]),
                          ffn_b, row(ln_g[l, 2]), row(ln_b[l, 2]), alpha, l)
    return h.reshape(bsz, s_len, d_model)
```

```python
import functools
import math

import numpy as np
import jax
import jax.numpy as jnp
from jax import lax
from jax.experimental import pallas as pl
from jax.experimental.pallas import tpu as pltpu

F32 = jnp.float32
BF16 = jnp.bfloat16

HEAD_DIM = 64
A_HEADS = 6
B_HEADS = 6
C_HEADS = 4
A_QK_DIM = 32
A_WIDTH = A_HEADS * HEAD_DIM
B_WIDTH = B_HEADS * HEAD_DIM
C_WIDTH = C_HEADS * HEAD_DIM
C_W_RANK = 64
C_A_RANK = 64
C_V_RANK = 32
C_G_RANK = 128
C_COLS = 3 * C_WIDTH + C_W_RANK + C_A_RANK + C_G_RANK
DILATED_PATTERNS = ((128, 1), (512, 4), (2048, 16))
ROPE_THETA = 10000.0
LN_EPS = 1e-5
RMS_EPS = 1e-5
C_GN_EPS = 64e-5
NEG_INF = -1e30
DECAY_RATE = math.exp(-0.5)
LOG2_E = math.log2(math.e)

LANES = 128
SUBLANES = 8
VMEM_LIMIT_BYTES = 56 * 1024 * 1024

ROW_TILE = 512
FF_CHUNK = 256
ATT_BLOCK = 512
RWKV_CHUNK = 64
RWKV_CHUNKS_PER_STEP = 8
SOLVE_BLOCK = 16

_NN = (((1,), (0,)), ((), ()))
_NT = (((1,), (1,)), ((), ()))


def _dot(a, b, dims=_NN):
    return lax.dot_general(a, b, dims, preferred_element_type=F32)


def _split2(x):
    hi = x.astype(BF16)
    lo = (x - hi.astype(F32)).astype(BF16)
    return hi, lo


def _split3(x):
    hi = x.astype(BF16)
    r = x - hi.astype(F32)
    mid = r.astype(BF16)
    lo = (r - mid.astype(F32)).astype(BF16)
    return hi, mid, lo


def _dot2r(a, b, dims=_NN):
    a_hi = a.astype(BF16)
    b_hi, b_lo = _split2(b)
    return _dot(a_hi, b_lo, dims) + _dot(a_hi, b_hi, dims)


def _dot1(a, b, dims=_NN):
    return _dot(a.astype(BF16), b.astype(BF16), dims)


def _dot_sel_lhs(sel, x, dims=_NN):
    h, m, l = _split3(x)
    return _dot(sel, l, dims) + _dot(sel, m, dims) + _dot(sel, h, dims)


def _dot_sel_rhs(x, sel):
    h, l = _split2(x)
    return _dot(l, sel) + _dot(h, sel)


def _layer_norm(y, g, b):
    mu = jnp.mean(y, axis=-1, keepdims=True)
    d = y - mu
    var = jnp.mean(d * d, axis=-1, keepdims=True)
    return d * lax.rsqrt(var + LN_EPS) * g + b


def _cparams(sem):
    return pltpu.CompilerParams(dimension_semantics=sem, vmem_limit_bytes=VMEM_LIMIT_BYTES)


def _resident(shape):
    nd = len(shape)
    return pl.BlockSpec(shape, lambda *_: (0,) * nd, pipeline_mode=pl.Buffered(1))


def _resident_layer(shape, layer):
    nd = len(shape)
    return pl.BlockSpec((None,) + tuple(shape), lambda *_: (layer,) + (0,) * nd,
                        pipeline_mode=pl.Buffered(1))


def _ffn_ln_rows(x, wg_ref, wu_ref, wd_ref, g, b, alpha):
    d_ff = wg_ref.shape[1]
    xb = x.astype(BF16)
    acc = jnp.zeros(x.shape, F32)
    for c0 in range(0, d_ff, FF_CHUNK):
        gate = _dot(xb, wg_ref[:, c0:c0 + FF_CHUNK])
        up = _dot(xb, wu_ref[:, c0:c0 + FF_CHUNK])
        h = (gate * jax.nn.sigmoid(gate)) * up
        acc = acc + _dot(h.astype(BF16), wd_ref[c0:c0 + FF_CHUNK, :])
    return _layer_norm(alpha * x + 0.5 * acc, g, b)


def _ffn_specs(d, d_ff, layer):
    return [_resident_layer((d, d_ff), layer), _resident_layer((d, d_ff), layer),
            _resident_layer((d_ff, d), layer), _resident((1, d)), _resident((1, d))]


def _rope_lanes(y, cos, sin_signed, group):
    half = group // 2
    lane = lax.broadcasted_iota(jnp.int32, (1, LANES), 1)
    first = (lane % group) < half
    swapped = jnp.where(first, pltpu.roll(y, LANES - half, 1), pltpu.roll(y, half, 1))
    return y * cos + swapped * sin_signed


def _in_proj_rows(x, w_ref, wvt_ref, ca_ref, sa_ref, cb_ref, sb_ref,
                  aq_ref, ak_ref, avt_ref, bq_ref, bk_ref, bvt_ref, cc_ref):
    xb = x.astype(BF16)
    ca, sa, cb, sb = ca_ref[...], sa_ref[...], cb_ref[...], sb_ref[...]
    a_scale = A_QK_DIM ** -0.5 * LOG2_E
    b_scale = HEAD_DIM ** -0.5 * LOG2_E
    b_col = 3 * A_WIDTH
    plan = ((aq_ref, 0, (ca, sa, A_QK_DIM), a_scale),
            (ak_ref, A_WIDTH, (ca, sa, A_QK_DIM), None),
            (bq_ref, b_col, (cb, sb, HEAD_DIM), b_scale),
            (bk_ref, b_col + B_WIDTH, (cb, sb, HEAD_DIM), None))
    segs = [(ref, s0, col + s0, rope, scale) for ref, col, rope, scale in plan
            for s0 in range(0, A_WIDTH, LANES)]
    wide = 2 * LANES
    for i in range(0, len(segs), wide // LANES):
        c0 = segs[i][2]
        y = _dot(xb, w_ref[:, c0:c0 + wide])
        for j in range(wide // LANES):
            ref, s0, _, rope, scale = segs[i + j]
            z = _rope_lanes(y[:, j * LANES:(j + 1) * LANES], *rope)
            if scale is not None:
                z = z * scale
            ref[:, s0:s0 + LANES] = z.astype(ref.dtype)
    col = b_col + 3 * B_WIDTH
    for s0 in range(0, C_COLS, wide):
        cc_ref[:, s0:s0 + wide] = _dot(xb, w_ref[:, col + s0:col + s0 + wide])
    avt_ref[...] = _dot(wvt_ref[0:A_WIDTH, :], xb, _NT).astype(avt_ref.dtype)
    bvt_ref[...] = _dot(wvt_ref[A_WIDTH:A_WIDTH + B_WIDTH, :], xb, _NT).astype(bvt_ref.dtype)


def _ffn_in_proj_kernel(x_ref, wg_ref, wu_ref, wd_ref, g_ref, b_ref, w_ref, wvt_ref,
                        ca_ref, sa_ref, cb_ref, sb_ref, h_ref, *out_refs, alpha):
    h = _ffn_ln_rows(x_ref[...], wg_ref, wu_ref, wd_ref, g_ref[...], b_ref[...], alpha)
    h_ref[...] = h
    _in_proj_rows(h, w_ref, wvt_ref, ca_ref, sa_ref, cb_ref, sb_ref, *out_refs)


def _ffn_in_proj(x, ffn_w, g, b, w_in, w_vt, tabs, s_len, alpha, layer):
    t, d = x.shape
    d_ff = ffn_w[0].shape[2]
    tm = min(ROW_TILE, s_len)
    spb = s_len // tm
    row = lambda i: (i, 0)
    tab = pl.BlockSpec((tm, LANES), lambda i: (i % spb, 0))
    qk = jax.ShapeDtypeStruct((t, A_WIDTH), BF16)
    vt = jax.ShapeDtypeStruct((A_WIDTH, t), BF16)
    qk_spec = pl.BlockSpec((tm, A_WIDTH), row)
    vt_spec = pl.BlockSpec((A_WIDTH, tm), lambda i: (0, i))
    return pl.pallas_call(
        functools.partial(_ffn_in_proj_kernel, alpha=alpha),
        out_shape=(jax.ShapeDtypeStruct((t, d), F32), qk, qk, vt, qk, qk, vt,
                   jax.ShapeDtypeStruct((t, C_COLS), F32)),
        grid=(t // tm,),
        in_specs=[pl.BlockSpec((tm, d), row)] + _ffn_specs(d, d_ff, layer)
        + [_resident_layer(w_in.shape[1:], layer), _resident_layer(w_vt.shape[1:], layer), tab, tab, tab, tab],
        out_specs=(pl.BlockSpec((tm, d), row), qk_spec, qk_spec, vt_spec, qk_spec, qk_spec, vt_spec,
                   pl.BlockSpec((tm, C_COLS), row)),
        compiler_params=_cparams(("parallel",)),
        name="ffn_in_proj",
    )(x, *ffn_w, g, b, w_in, w_vt, *tabs)


def _rope_tables(s_len, group):
    half = group // 2
    inv = ROPE_THETA ** (-jnp.arange(0, group, 2, dtype=F32) / group)
    ang = jnp.arange(s_len, dtype=F32)[:, None] * inv[None, :]
    lane = np.arange(LANES)
    idx = lane % half
    sign = np.where((lane % group) < half, -1.0, 1.0).astype(np.float32)
    cos = jnp.cos(ang)[:, idx]
    sin = jnp.sin(ang)[:, idx] * sign[None, :]
    return cos, sin


V_ROWS = HEAD_DIM + 16
DIFF_GROUPS = 4
DIL_GROUPS = 2


def _attn_two_pass(q_stack_ref, k_ref, vt_ref, s_refs, acc_ref, qi, blk, n_comb, n_groups,
                   off_bias, diag_bias):
    gs = blk // n_groups
    per_head = n_comb // 2
    gcols = n_comb * gs

    def cols(g):
        return slice(g * gcols, (g + 1) * gcols)

    def tiled(bias):
        return jnp.concatenate([bias] * n_comb, axis=1)

    def k_at(kb, size):
        return k_ref[0, kb * blk:kb * blk + size, :]

    vx_cache = {}

    def vx_at(h, kb, size):
        if (h, kb, size) not in vx_cache:
            v = vt_ref[h * HEAD_DIM:(h + 1) * HEAD_DIM, kb * blk:kb * blk + size]
            vx_cache[h, kb, size] = jnp.concatenate(
                [v, jnp.ones((V_ROWS - HEAD_DIM, size), v.dtype)], axis=0)
        return vx_cache[h, kb, size]

    def col_max(m, s):
        part = jnp.max(s.reshape(s.shape[0] // SUBLANES, SUBLANES, s.shape[1]), axis=0)
        return part if m is None else jnp.maximum(m, part)

    mx = [None] * n_groups
    m = [None] * n_groups
    acc = [[None, None] for _ in range(n_groups)]

    def pass_a(g, kb):
        s = _dot(k_at(kb, blk), q_stack_ref[cols(g), :], _NT)
        bias = off_bias(kb)
        if bias is not None:
            s = s + tiled(bias[:, g * gs:(g + 1) * gs])
        s_refs[g][kb] = s
        mx[g] = col_max(mx[g], s)

    def pass_a_diag(g):
        nk = (g + 1) * gs
        s = _dot(k_at(qi, nk), q_stack_ref[cols(g), :], _NT)
        bias = diag_bias(g)
        if bias is not None:
            s = s + tiled(bias)
        s_refs[g][qi, 0:nk, :] = s
        m[g] = jnp.max(col_max(mx[g], s), axis=0, keepdims=True)

    def keys_of(g, kb):
        return blk if kb < qi else (g + 1) * gs

    def probs(g, kb):
        return jnp.exp2(s_refs[g][kb, 0:keys_of(g, kb), :] - m[g]).astype(BF16)

    def pass_b(g, kb, p):
        for h in range(2):
            d = _dot(vx_at(h, kb, keys_of(g, kb)), p[:, h * per_head * gs:(h + 1) * per_head * gs])
            acc[g][h] = d if acc[g][h] is None else acc[g][h] + d

    for kb in range(qi):
        pass_a(0, kb)
    pass_a_diag(0)
    for g in range(1, n_groups + 1):
        p = probs(g - 1, 0)
        for kb in range(qi + 1):
            if g < n_groups and kb < qi:
                pass_a(g, kb)
            elif g < n_groups:
                pass_a_diag(g)
            p_next = probs(g - 1, kb + 1) if kb < qi else None
            pass_b(g - 1, kb, p)
            p = p_next
    for h in range(2):
        acc_ref[h] = jnp.concatenate([acc[g][h] for g in range(n_groups)], axis=1)


def _for_each_query_block(qi, n_q, fn):
    for n in range(n_q):
        @pl.when(qi == n)
        def _(n=n):
            fn(n)


def _normalized(acc_ref, c, n_comb, n_groups):
    per_head = n_comb // 2
    h, j = divmod(c, per_head)
    blk = acc_ref.shape[2] // per_head
    gs = blk // n_groups
    acc = acc_ref[h]
    acc = jnp.concatenate([acc[:, (g * per_head + j) * gs:(g * per_head + j + 1) * gs]
                           for g in range(n_groups)], axis=1)
    return acc[0:HEAD_DIM] / acc[HEAD_DIM:HEAD_DIM + 1]


def _store_q_stack(q_stack_ref, q, lane_masks, n_groups):
    gs = q.shape[0] // n_groups
    zero = jnp.zeros_like(q)
    n_comb = len(lane_masks)
    for g in range(n_groups):
        for c, mask in enumerate(lane_masks):
            r0 = (g * n_comb + c) * gs
            q_stack_ref[r0:r0 + gs, :] = jnp.where(mask, q, zero)[g * gs:(g + 1) * gs]


def _head_rms_t(o_t, g_t):
    sq = o_t * o_t
    ms = [jnp.sum(sq[h * HEAD_DIM:(h + 1) * HEAD_DIM], axis=0, keepdims=True) * (1.0 / HEAD_DIM)
          for h in range(2)]
    scale = jnp.concatenate([jnp.broadcast_to(lax.rsqrt(m + RMS_EPS), (HEAD_DIM, o_t.shape[1])) for m in ms],
                            axis=0)
    return o_t * scale * g_t


def _diff_attn_kernel(lam_ref, gt_ref, q_ref, k_ref, vt_ref, o_ref, q4_ref, acc_ref, *s_refs,
                      blk, lam_init):
    qi = pl.program_id(2)
    n_groups = DIFF_GROUPS
    gs = blk // n_groups
    lane = lax.broadcasted_iota(jnp.int32, (1, LANES), 1)
    _store_q_stack(q4_ref, q_ref[0], [(lane // A_QK_DIM) == c for c in range(4)], n_groups)

    def causal(g):
        key = lax.broadcasted_iota(jnp.int32, ((g + 1) * gs, gs), 0)
        qry = lax.broadcasted_iota(jnp.int32, ((g + 1) * gs, gs), 1) + g * gs
        return jnp.where(key <= qry, 0.0, NEG_INF)

    _for_each_query_block(qi, k_ref.shape[1] // blk, lambda n: _attn_two_pass(
        q4_ref, k_ref, vt_ref, s_refs, acc_ref, n, blk, 4, n_groups, lambda kb: None, causal))

    lp = lam_ref[...]
    lam = (jnp.exp(jnp.sum(lp[0:1] * lp[1:2], axis=1, keepdims=True))
           - jnp.exp(jnp.sum(lp[2:3] * lp[3:4], axis=1, keepdims=True)) + lam_init)
    o_t = jnp.concatenate([_normalized(acc_ref, 2 * h, 4, n_groups)
                           - lam * _normalized(acc_ref, 2 * h + 1, 4, n_groups)
                           for h in range(2)], axis=0)
    o_t = _head_rms_t(o_t, gt_ref[...]) * (1.0 - lam_init)
    o_ref[0] = o_t.T.astype(o_ref.dtype)


def _attn_call(kernel_fn, name, n_comb, n_groups, consts, const_specs, q, k, vt, blk):
    b, s, w = q.shape
    pairs = w // LANES
    return pl.pallas_call(
        kernel_fn,
        out_shape=jax.ShapeDtypeStruct((b, s, w), BF16),
        grid=(b, pairs, s // blk),
        in_specs=const_specs + [
            pl.BlockSpec((1, blk, LANES), lambda bi, p, i: (bi, i, p)),
            pl.BlockSpec((1, s, LANES), lambda bi, p, i: (bi, 0, p)),
            pl.BlockSpec((LANES, s), lambda bi, p, i: (p, bi)),
        ],
        out_specs=pl.BlockSpec((1, blk, LANES), lambda bi, p, i: (bi, i, p)),
        scratch_shapes=[pltpu.VMEM((n_comb * blk, LANES), BF16),
                        pltpu.VMEM((2, V_ROWS, n_comb // 2 * blk), F32)]
        + [pltpu.VMEM((s // blk, blk, n_comb * blk // n_groups), F32) for _ in range(n_groups)],
        compiler_params=_cparams(("parallel", "parallel", "arbitrary")),
        name=name,
    )(*consts, q, k, vt)


def _diff_attn(q, k, vt, lam_p, g_t, lam_init):
    blk = g_t.shape[1]
    return _attn_call(functools.partial(_diff_attn_kernel, blk=blk, lam_init=lam_init), "diff_attn", 4, DIFF_GROUPS,
                      [lam_p, g_t], [_resident((4, LANES)), _resident(g_t.shape)], q, k, vt, blk)


def _dilated_log2_counts(blk):
    far = max(w for w, _ in DILATED_PATTERNS if w < max(p[0] for p in DILATED_PATTERNS))
    n_off = far // blk + 2
    j = np.arange(blk)[:, None]
    i = np.arange(blk)[None, :]
    cnt = np.zeros((n_off, blk, blk), np.float64)
    for o in range(n_off):
        delta = o * blk + i - j
        for window, dil in DILATED_PATTERNS:
            cnt[o] += ((delta >= 0) & (delta <= window) & (delta % dil == 0))
    return np.where(cnt > 0, np.log2(np.maximum(cnt, 1.0)), NEG_INF).astype(np.float32)


def _dil_attn_kernel(bias_ref, gt_ref, q_ref, k_ref, vt_ref, o_ref, q2_ref, acc_ref, *s_refs,
                     blk, n_off):
    qi = pl.program_id(2)
    n_groups = DIL_GROUPS
    gs = blk // n_groups
    lane = lax.broadcasted_iota(jnp.int32, (1, LANES), 1)
    low = lane < HEAD_DIM
    _store_q_stack(q2_ref, q_ref[0], [low, jnp.logical_not(low)], n_groups)

    _for_each_query_block(qi, k_ref.shape[1] // blk, lambda n: _attn_two_pass(
        q2_ref, k_ref, vt_ref, s_refs, acc_ref, n, blk, 2, n_groups,
        lambda kb: bias_ref[min(n - kb, n_off - 1)],
        lambda g: bias_ref[0, 0:(g + 1) * gs, g * gs:(g + 1) * gs]))

    o_t = jnp.concatenate([_normalized(acc_ref, c, 2, n_groups) for c in range(2)], axis=0)
    o_ref[0] = _head_rms_t(o_t, gt_ref[...]).T.astype(o_ref.dtype)


def _dil_attn(q, k, vt, bias, g_t):
    blk = bias.shape[1]
    return _attn_call(functools.partial(_dil_attn_kernel, blk=blk, n_off=bias.shape[0]), "dil_attn", 2, DIL_GROUPS,
                      [bias, g_t], [_resident(bias.shape), _resident(g_t.shape)], q, k, vt, blk)


def _solve_unit_lower(a_list, x_list):
    n = len(x_list)
    c = x_list[0].shape[0]
    nb = c // SOLVE_BLOCK
    sub = SOLVE_BLOCK // SUBLANES
    a_split = [_split2(a[:, 0:c]) for a in a_list]
    xs = [[x[SOLVE_BLOCK * i:SOLVE_BLOCK * (i + 1)] for i in range(nb)] for x in x_list]
    zeros = jnp.zeros((SOLVE_BLOCK, LANES), F32)
    for b in range(nb):
        base = b * SOLVE_BLOCK
        rows = [[xs[u][b][SUBLANES * i:SUBLANES * (i + 1)] for i in range(sub)] for u in range(n)]
        for t in range(SOLVE_BLOCK - 1):
            bi = t // SUBLANES
            for u in range(n):
                xt = rows[u][bi][t % SUBLANES:t % SUBLANES + 1, :]
                for i in range(bi, sub):
                    r0 = base + SUBLANES * i
                    a_col = a_list[u][r0:r0 + SUBLANES, base + t:base + t + 1]
                    rows[u][i] = rows[u][i] - a_col * xt
        for u in range(n):
            xs[u][b] = jnp.concatenate(rows[u], axis=0)
        if b + 1 < nb:
            below = slice(base + SOLVE_BLOCK, c)
            for u in range(n):
                a_hi, a_lo = a_split[u]
                x_hi, x_lo = _split2(jnp.concatenate(
                    [zeros] * b + [xs[u][b]] + [zeros] * (nb - 1 - b), axis=0))
                upd = _dot(a_lo[below], x_hi) + _dot(a_hi[below], x_lo) + _dot(a_hi[below], x_hi)
                for j in range(b + 1, nb):
                    xs[u][j] = xs[u][j] - upd[(j - b - 1) * SOLVE_BLOCK:(j - b) * SOLVE_BLOCK]
    return [jnp.concatenate(x, axis=0) for x in xs]


def _to_slab(col_block, half, lower):
    lane = lax.broadcasted_iota(jnp.int32, (1, LANES), 1)
    want_low = lower
    have_low = half == 0
    src = col_block if want_low == have_low else pltpu.roll(col_block, HEAD_DIM, 1)
    keep = (lane < HEAD_DIM) if want_low else (lane >= HEAD_DIM)
    return jnp.where(keep, src, 0.0)


def _rwkv_kernel(*refs, n_chunks, has_vres):
    if has_vres:
        (cc_ref, vf_ref, mu_ref, w0_ref, w2_ref, a0_ref, a2_ref, g2_ref, kk_ref, ka_ref, rk_ref,
         gng_ref, gnb_ref, v0_ref, v1_ref, v2_ref, bd_ref, tri_ref,
         o_ref, carry_ref, hs_ref) = refs
    else:
        (cc_ref, mu_ref, w0_ref, w2_ref, a0_ref, a2_ref, g2_ref, kk_ref, ka_ref, rk_ref,
         gng_ref, gnb_ref, bd_ref, tri_ref,
         o_ref, vf_out_ref, carry_ref, hs_ref) = refs
    cw = C_WIDTH
    cl = RWKV_CHUNK

    @pl.when(pl.program_id(1) == 0)
    def _():
        carry_ref[...] = jnp.zeros(carry_ref.shape, F32)
        hs_ref[...] = jnp.zeros(hs_ref.shape, F32)

    c_raw = cc_ref[0]
    tb = c_raw.shape[0]
    row = lax.broadcasted_iota(jnp.int32, (tb, 1), 0)
    prev = jnp.where(row == 0, carry_ref[...], pltpu.roll(c_raw, 1, 0))
    carry_ref[...] = c_raw[tb - 1:tb, :]
    c = c_raw + (prev - c_raw) * mu_ref[...]

    r = c[:, 0:cw]
    k = c[:, cw:2 * cw]
    v = c[:, 2 * cw:3 * cw]
    xwa = c[:, 3 * cw:3 * cw + LANES]
    xg = c[:, 3 * cw + LANES:3 * cw + 2 * LANES]
    if has_vres:
        mix = jax.nn.sigmoid(v0_ref[...] + _dot2r(_dot2r(v, v1_ref[...]), v2_ref[...]))
        v = v + (vf_ref[0] - v) * mix
    else:
        vf_out_ref[0] = v
    bd = bd_ref[...]
    lw = -DECAY_RATE * jax.nn.sigmoid(w0_ref[...] + _dot1(jnp.tanh(xwa), w2_ref[...]))
    a = jax.nn.sigmoid(a0_ref[...] + _dot1(xwa, a2_ref[...]))
    gate = _dot2r(jax.nn.sigmoid(xg), g2_ref[...])
    kkv = k * kk_ref[...]
    kappa = kkv / jnp.maximum(jnp.sqrt(_dot_sel_rhs(kkv * kkv, bd)), 1e-12)
    k2 = k * (1.0 + (a - 1.0) * ka_ref[...])
    bonus = _dot_sel_rhs(r * k2 * rk_ref[...], bd)
    beta = a * kappa

    lane = lax.broadcasted_iota(jnp.int32, (1, LANES), 1)
    low = lane < HEAD_DIM
    lane2 = lax.broadcasted_iota(jnp.int32, (1, cw), 1)
    rr = lax.broadcasted_iota(jnp.int32, (2 * cl, LANES), 0)
    cidx = lax.broadcasted_iota(jnp.int32, (2 * cl, LANES), 1) % cl
    tri_mask = cidx < jnp.where(rr < cl, rr, rr - cl + 1)
    eye = (lax.broadcasted_iota(jnp.int32, (cl, LANES), 0)
           == lax.broadcasted_iota(jnp.int32, (cl, LANES), 1))
    tri = tri_ref[...]

    zeros = jnp.zeros((cl, LANES), F32)
    units = []
    for ci in range(n_chunks):
        sl = slice(ci * cl, (ci + 1) * cl)
        lwc = lw[sl]
        cum = _dot_sel_lhs(tri, lwc)
        cum_end = cum[cl - 1:cl, :]
        rt = r[sl] * jnp.exp(cum)
        kt = kappa[sl] * jnp.exp(cum - lwc)
        e_inv = jnp.exp(-cum)
        e_end = jnp.exp(cum_end - cum)
        bh = beta[sl] * e_end
        kh = k2[sl] * e_end
        p_end = jnp.exp(cum_end)
        vc = v[sl]
        right = jnp.concatenate([beta[sl] * e_inv, k2[sl] * e_inv], axis=0).astype(BF16)
        left_hi, left_lo = _split2(jnp.concatenate([kt, rt], axis=0))
        for h in range(C_HEADS):
            cb, half = divmod(h, 2)
            cs = slice(cb * LANES, (cb + 1) * LANES)
            hm = ((lane2 // HEAD_DIM) == h).astype(BF16)
            units.append(dict(
                ci=ci, h=h, right=right, left_hi=left_hi * hm, left_lo=left_lo * hm,
                vw=_to_slab(vc[:, cs], half, lower=False),
                kw=_to_slab(kt[:, cs], half, lower=True),
                rw=_to_slab(rt[:, cs], half, lower=True),
                bkw=jnp.concatenate([_to_slab(bh[:, cs], half, True), _to_slab(kh[:, cs], half, True)], axis=0),
                pe=_to_slab(jnp.broadcast_to(p_end[:, cs], (cl, LANES)), half, True)))
    for u in units:
        u["m"] = jnp.where(tri_mask, _dot(u["left_lo"], u["right"], _NT) + _dot(u["left_hi"], u["right"], _NT), 0.0)
    for u in units:
        u["rhs"] = u["kw"] + _dot1(u["m"][0:cl], jnp.concatenate([zeros, u["vw"]], axis=0))
    xs = _solve_unit_lower([u["m"][0:cl] for u in units], [u["rhs"] for u in units])
    for u, x in zip(units, xs):
        u["st"] = jnp.concatenate([-x, u["vw"]], axis=0)
    for u in units:
        u["qy"] = u["rw"] + _dot2r(u["m"][cl:2 * cl], u["st"])
    for u in units:
        u["gh"] = jnp.where(eye, u["pe"], 0.0) + _dot1(u["bkw"].T[0:cl], u["st"])
    for u in units:
        hs = hs_ref[u["h"]]
        u["y"] = u["qy"] + _dot1(u["qy"], hs)
        hs_ref[u["h"], 0:cl, :] = jnp.where(low, 0.0, u["gh"] + _dot2r(u["gh"], hs))
    y_chunks = []
    for ci in range(n_chunks):
        yh = [u["y"] for u in units if u["ci"] == ci]
        cols = [jnp.where(low, pltpu.roll(yh[2 * cb], HEAD_DIM, 1), yh[2 * cb + 1])
                for cb in range(C_HEADS // 2)]
        y_chunks.append(jnp.concatenate(cols, axis=1))
    y = jnp.concatenate(y_chunks, axis=0) if n_chunks > 1 else y_chunks[0]

    inv = 1.0 / HEAD_DIM
    mean = _dot_sel_rhs(y, bd) * inv
    d = y - mean
    var = _dot_sel_rhs(d * d, bd) * inv
    yn = d * lax.rsqrt(var + C_GN_EPS) * gng_ref[...] + gnb_ref[...]
    out = (yn + bonus * v) * gate
    o_ref[0] = out.astype(o_ref.dtype)


def _rwkv(cc, v_first, p, consts):
    b, s, _ = cc.shape
    n_chunks = RWKV_CHUNKS_PER_STEP
    tb = n_chunks * RWKV_CHUNK
    has_vres = v_first is not None
    blk_cc = pl.BlockSpec((1, tb, C_COLS), lambda bi, j: (bi, j, 0))
    blk_cw = pl.BlockSpec((1, tb, C_WIDTH), lambda bi, j: (bi, j, 0))
    vec = lambda n: _resident((1, n))
    params = [p["mu"], p["w0"], p["w2"], p["a0"], p["a2"], p["g2"], p["k_k"], p["k_a"], p["r_k"],
              p["gn_g"], p["gn_b"]]
    specs = [vec(C_COLS), vec(C_WIDTH), _resident((LANES, C_WIDTH)), vec(C_WIDTH),
             _resident((LANES, C_WIDTH)), _resident((LANES, C_WIDTH)), vec(C_WIDTH), vec(C_WIDTH),
             vec(C_WIDTH), vec(C_WIDTH), vec(C_WIDTH)]
    if has_vres:
        params += [p["v0"], p["v1"], p["v2"]]
        specs += [vec(C_WIDTH), _resident((C_WIDTH, LANES)), _resident((LANES, C_WIDTH))]
    params += [consts["bd"], consts["tri"]]
    specs += [_resident((C_WIDTH, C_WIDTH)), _resident((RWKV_CHUNK, RWKV_CHUNK))]
    o_shape = jax.ShapeDtypeStruct((b, s, C_WIDTH), BF16)
    if has_vres:
        args = [cc, v_first] + params
        in_specs = [blk_cc, blk_cw] + specs
        out_shape, out_specs = o_shape, blk_cw
    else:
        args = [cc] + params
        in_specs = [blk_cc] + specs
        out_shape = (o_shape, jax.ShapeDtypeStruct((b, s, C_WIDTH), F32))
        out_specs = (blk_cw, blk_cw)
    return pl.pallas_call(
        functools.partial(_rwkv_kernel, n_chunks=n_chunks, has_vres=has_vres),
        out_shape=out_shape,
        grid=(b, s // tb),
        in_specs=in_specs,
        out_specs=out_specs,
        scratch_shapes=[pltpu.VMEM((1, C_COLS), F32),
                        pltpu.VMEM((C_HEADS, 2 * RWKV_CHUNK, LANES), F32)],
        compiler_params=_cparams(("parallel", "arbitrary")),
        name="rwkv_vres" if has_vres else "rwkv_first",
    )(*args)


def _out_proj_ffn_kernel(x_ref, oa_ref, ob_ref, oc_ref, w_ref, g1_ref, b1_ref,
                         wg_ref, wu_ref, wd_ref, g2_ref, b2_ref, o_ref, *, alpha):
    mix = _dot(jnp.concatenate([oa_ref[...], ob_ref[...], oc_ref[...]], axis=1), w_ref[...])
    h = _layer_norm(alpha * x_ref[...] + mix, g1_ref[...], b1_ref[...])
    o_ref[...] = _ffn_ln_rows(h, wg_ref, wu_ref, wd_ref, g2_ref[...], b2_ref[...], alpha)


def _out_proj_ffn(x, oa, ob, oc, w_out, g1, b1, ffn_w, g2, b2, alpha, layer):
    t, d = x.shape
    d_ff = ffn_w[0].shape[2]
    tm = min(ROW_TILE, t)
    row = lambda i: (i, 0)
    return pl.pallas_call(
        functools.partial(_out_proj_ffn_kernel, alpha=alpha),
        out_shape=jax.ShapeDtypeStruct((t, d), F32),
        grid=(t // tm,),
        in_specs=[pl.BlockSpec((tm, d), row), pl.BlockSpec((tm, A_WIDTH), row),
                  pl.BlockSpec((tm, B_WIDTH), row), pl.BlockSpec((tm, C_WIDTH), row),
                  _resident_layer(w_out.shape[1:], layer), _resident((1, d)), _resident((1, d))]
        + _ffn_specs(d, d_ff, layer),
        out_specs=pl.BlockSpec((tm, d), row),
        compiler_params=_cparams(("parallel",)),
        name="out_proj_ffn",
    )(x, oa, ob, oc, w_out, g1, b1, *ffn_w, g2, b2)


def _pad_rows(w, total, offset):
    return jnp.zeros((total, w.shape[1]), w.dtype).at[offset:offset + w.shape[0]].set(w)


def kernel(x, ffn_a_gate, ffn_a_up, ffn_a_down, ffn_b_gate, ffn_b_up, ffn_b_down, ln_g, ln_b, w_in, w_out, a_lam_q1, a_lam_k1, a_lam_q2, a_lam_k2, a_norm_g, b_norm_g, c_mu, c_w0, c_w2, c_a0, c_a2, c_g2, c_k_k, c_k_a, c_r_k, c_gn_g, c_gn_b, c_v0, c_v1, c_v2):
    bsz, s_len, d_model = x.shape
    depth = w_in.shape[0]
    alpha = (2 * depth) ** 0.25
    t = bsz * s_len

    tabs = _rope_tables(s_len, A_QK_DIM) + _rope_tables(s_len, HEAD_DIM)
    dil_bias = jnp.asarray(_dilated_log2_counts(min(ATT_BLOCK, s_len)))
    head_of = np.arange(C_WIDTH) // HEAD_DIM
    consts = {
        "bd": jnp.asarray(head_of[:, None] == head_of[None, :], BF16),
        "tri": jnp.asarray(np.tril(np.ones((RWKV_CHUNK, RWKV_CHUNK), np.float32)), BF16),
    }
    row = lambda vct: vct.reshape(1, -1)
    att_blk = min(ATT_BLOCK, s_len)
    gain_t = lambda vct: jnp.broadcast_to(jnp.tile(vct, LANES // HEAD_DIM)[:, None], (LANES, att_blk))

    ffn_a = (ffn_a_gate.astype(BF16), ffn_a_up.astype(BF16), ffn_a_down.astype(BF16))
    ffn_b = (ffn_b_gate.astype(BF16), ffn_b_up.astype(BF16), ffn_b_down.astype(BF16))
    w_in_b = w_in.astype(BF16)
    v_cols = np.r_[2 * A_WIDTH:3 * A_WIDTH, 3 * A_WIDTH + 2 * B_WIDTH:3 * A_WIDTH + 3 * B_WIDTH]
    w_vt_b = jnp.swapaxes(w_in_b[:, :, v_cols], 1, 2)
    w_out_b = w_out.astype(BF16)

    h = x.reshape(t, d_model)
    v_first = None
    for l in range(depth):
        h, aq, ak, avt, bq, bk, bvt, cc = _ffn_in_proj(h, ffn_a, row(ln_g[l, 0]), row(ln_b[l, 0]),
                                                       w_in_b, w_vt_b, tabs, s_len, alpha, l)
        shp = (bsz, s_len, A_WIDTH)

        lam_init = 0.8 - 0.6 * math.exp(-0.3 * l)
        lam_p = jnp.zeros((4, LANES), F32).at[:, :A_QK_DIM].set(
            jnp.stack([a_lam_q1[l], a_lam_k1[l], a_lam_q2[l], a_lam_k2[l]]))
        o_a = _diff_attn(aq.reshape(shp), ak.reshape(shp), avt, lam_p, gain_t(a_norm_g[l]), lam_init)
        o_b = _dil_attn(bq.reshape(shp), bk.reshape(shp), bvt, dil_bias, gain_t(b_norm_g[l]))

        p = {"mu": row(c_mu[l]), "w0": row(c_w0[l]), "a0": row(c_a0[l]),
             "w2": _pad_rows(c_w2[l], LANES, 0), "a2": _pad_rows(c_a2[l], LANES, C_W_RANK),
             "g2": c_g2[l], "k_k": row(c_k_k[l]), "k_a": row(c_k_a[l]), "r_k": row(c_r_k[l]),
             "gn_g": row(c_gn_g[l]), "gn_b": row(c_gn_b[l])}
        cc3 = cc.reshape(bsz, s_len, C_COLS)
        if l == 0:
            o_c, v_first = _rwkv(cc3, None, p, consts)
        else:
            p["v0"] = row(c_v0[l - 1])
            p["v1"] = jnp.zeros((C_WIDTH, LANES), F32).at[:, :C_V_RANK].set(c_v1[l - 1])
            p["v2"] = _pad_rows(c_v2[l - 1], LANES, 0)
            o_c = _rwkv(cc3, v_first, p, consts)

        h = _out_proj_ffn(h, o_a.reshape(t, A_WIDTH), o_b.reshape(t, B_WIDTH), o_c.reshape(t, C_WIDTH),
                          w_out_b, row(ln_g[l, 1]), row(ln_b[l, 1]),
                          ffn_b, row(ln_g[l, 2]), row(ln_b[l, 2]), alpha, l)
    return h.reshape(bsz, s_len, d_model)
```

```python
import functools
import math

import numpy as np
import jax
import jax.numpy as jnp
from jax import lax
from jax.experimental import pallas as pl
from jax.experimental.pallas import tpu as pltpu

F32 = jnp.float32
BF16 = jnp.bfloat16

HEAD_DIM = 64
A_HEADS = 6
B_HEADS = 6
C_HEADS = 4
A_QK_DIM = 32
A_WIDTH = A_HEADS * HEAD_DIM
B_WIDTH = B_HEADS * HEAD_DIM
C_WIDTH = C_HEADS * HEAD_DIM
C_W_RANK = 64
C_A_RANK = 64
C_V_RANK = 32
C_G_RANK = 128
C_COLS = 3 * C_WIDTH + C_W_RANK + C_A_RANK + C_G_RANK
DILATED_PATTERNS = ((128, 1), (512, 4), (2048, 16))
ROPE_THETA = 10000.0
LN_EPS = 1e-5
RMS_EPS = 1e-5
C_GN_EPS = 64e-5
NEG_INF = -1e30
DECAY_RATE = math.exp(-0.5)
LOG2_E = math.log2(math.e)

LANES = 128
SUBLANES = 8
VMEM_LIMIT_BYTES = 56 * 1024 * 1024

ROW_TILE = 512
FF_CHUNK = 256
ATT_BLOCK = 512
RWKV_CHUNK = 64
RWKV_CHUNKS_PER_STEP = 8
SOLVE_BLOCK = 16

_NN = (((1,), (0,)), ((), ()))
_NT = (((1,), (1,)), ((), ()))


def _dot(a, b, dims=_NN):
    return lax.dot_general(a, b, dims, preferred_element_type=F32)


def _split2(x):
    hi = x.astype(BF16)
    lo = (x - hi.astype(F32)).astype(BF16)
    return hi, lo


def _split3(x):
    hi = x.astype(BF16)
    r = x - hi.astype(F32)
    mid = r.astype(BF16)
    lo = (r - mid.astype(F32)).astype(BF16)
    return hi, mid, lo


def _dot2r(a, b, dims=_NN):
    a_hi = a.astype(BF16)
    b_hi, b_lo = _split2(b)
    return _dot(a_hi, b_lo, dims) + _dot(a_hi, b_hi, dims)


def _dot1(a, b, dims=_NN):
    return _dot(a.astype(BF16), b.astype(BF16), dims)


def _dot_sel_lhs(sel, x, dims=_NN):
    h, m, l = _split3(x)
    return _dot(sel, l, dims) + _dot(sel, m, dims) + _dot(sel, h, dims)


def _dot_sel_rhs(x, sel):
    h, l = _split2(x)
    return _dot(l, sel) + _dot(h, sel)


def _layer_norm(y, g, b):
    mu = jnp.mean(y, axis=-1, keepdims=True)
    d = y - mu
    var = jnp.mean(d * d, axis=-1, keepdims=True)
    return d * lax.rsqrt(var + LN_EPS) * g + b


def _cparams(sem):
    return pltpu.CompilerParams(dimension_semantics=sem, vmem_limit_bytes=VMEM_LIMIT_BYTES)


def _resident(shape):
    nd = len(shape)
    return pl.BlockSpec(shape, lambda *_: (0,) * nd, pipeline_mode=pl.Buffered(1))


def _resident_layer(shape, layer):
    nd = len(shape)
    return pl.BlockSpec((None,) + tuple(shape), lambda *_: (layer,) + (0,) * nd,
                        pipeline_mode=pl.Buffered(1))


def _ffn_ln_rows(x, wg_ref, wu_ref, wd_ref, g, b, alpha):
    d_ff = wg_ref.shape[1]
    xb = x.astype(BF16)
    acc = jnp.zeros(x.shape, F32)
    for c0 in range(0, d_ff, FF_CHUNK):
        gate = _dot(xb, wg_ref[:, c0:c0 + FF_CHUNK])
        up = _dot(xb, wu_ref[:, c0:c0 + FF_CHUNK])
        h = (gate * jax.nn.sigmoid(gate)) * up
        acc = acc + _dot(h.astype(BF16), wd_ref[c0:c0 + FF_CHUNK, :])
    return _layer_norm(alpha * x + 0.5 * acc, g, b)


def _ffn_specs(d, d_ff, layer):
    return [_resident_layer((d, d_ff), layer), _resident_layer((d, d_ff), layer),
            _resident_layer((d_ff, d), layer), _resident((1, d)), _resident((1, d))]


def _rope_lanes(y, cos, sin_signed, group):
    half = group // 2
    lane = lax.broadcasted_iota(jnp.int32, (1, LANES), 1)
    first = (lane % group) < half
    swapped = jnp.where(first, pltpu.roll(y, LANES - half, 1), pltpu.roll(y, half, 1))
    return y * cos + swapped * sin_signed


def _in_proj_rows(x, w_ref, wvt_ref, ca_ref, sa_ref, cb_ref, sb_ref,
                  aq_ref, ak_ref, avt_ref, bq_ref, bk_ref, bvt_ref, cc_ref):
    xb = x.astype(BF16)
    ca, sa, cb, sb = ca_ref[...], sa_ref[...], cb_ref[...], sb_ref[...]
    a_scale = A_QK_DIM ** -0.5 * LOG2_E
    b_scale = HEAD_DIM ** -0.5 * LOG2_E
    b_col = 3 * A_WIDTH
    plan = ((aq_ref, 0, (ca, sa, A_QK_DIM), a_scale),
            (ak_ref, A_WIDTH, (ca, sa, A_QK_DIM), None),
            (bq_ref, b_col, (cb, sb, HEAD_DIM), b_scale),
            (bk_ref, b_col + B_WIDTH, (cb, sb, HEAD_DIM), None))
    segs = [(ref, s0, col + s0, rope, scale) for ref, col, rope, scale in plan
            for s0 in range(0, A_WIDTH, LANES)]
    wide = 2 * LANES
    for i in range(0, len(segs), wide // LANES):
        c0 = segs[i][2]
        y = _dot(xb, w_ref[:, c0:c0 + wide])
        for j in range(wide // LANES):
            ref, s0, _, rope, scale = segs[i + j]
            z = _rope_lanes(y[:, j * LANES:(j + 1) * LANES], *rope)
            if scale is not None:
                z = z * scale
            ref[:, s0:s0 + LANES] = z.astype(ref.dtype)
    col = b_col + 3 * B_WIDTH
    for s0 in range(0, C_COLS, wide):
        cc_ref[:, s0:s0 + wide] = _dot(xb, w_ref[:, col + s0:col + s0 + wide])
    avt_ref[...] = _dot(wvt_ref[0:A_WIDTH, :], xb, _NT).astype(avt_ref.dtype)
    bvt_ref[...] = _dot(wvt_ref[A_WIDTH:A_WIDTH + B_WIDTH, :], xb, _NT).astype(bvt_ref.dtype)


def _ffn_in_proj_kernel(x_ref, wg_ref, wu_ref, wd_ref, g_ref, b_ref, w_ref, wvt_ref,
                        ca_ref, sa_ref, cb_ref, sb_ref, h_ref, *out_refs, alpha):
    h = _ffn_ln_rows(x_ref[...], wg_ref, wu_ref, wd_ref, g_ref[...], b_ref[...], alpha)
    h_ref[...] = h
    _in_proj_rows(h, w_ref, wvt_ref, ca_ref, sa_ref, cb_ref, sb_ref, *out_refs)


def _ffn_in_proj(x, ffn_w, g, b, w_in, w_vt, tabs, s_len, alpha, layer):
    t, d = x.shape
    d_ff = ffn_w[0].shape[2]
    tm = min(ROW_TILE, s_len)
    spb = s_len // tm
    row = lambda i: (i, 0)
    tab = pl.BlockSpec((tm, LANES), lambda i: (i % spb, 0))
    qk = jax.ShapeDtypeStruct((t, A_WIDTH), BF16)
    vt = jax.ShapeDtypeStruct((A_WIDTH, t), BF16)
    qk_spec = pl.BlockSpec((tm, A_WIDTH), row)
    vt_spec = pl.BlockSpec((A_WIDTH, tm), lambda i: (0, i))
    return pl.pallas_call(
        functools.partial(_ffn_in_proj_kernel, alpha=alpha),
        out_shape=(jax.ShapeDtypeStruct((t, d), F32), qk, qk, vt, qk, qk, vt,
                   jax.ShapeDtypeStruct((t, C_COLS), F32)),
        grid=(t // tm,),
        in_specs=[pl.BlockSpec((tm, d), row)] + _ffn_specs(d, d_ff, layer)
        + [_resident_layer(w_in.shape[1:], layer), _resident_layer(w_vt.shape[1:], layer), tab, tab, tab, tab],
        out_specs=(pl.BlockSpec((tm, d), row), qk_spec, qk_spec, vt_spec, qk_spec, qk_spec, vt_spec,
                   pl.BlockSpec((tm, C_COLS), row)),
        compiler_params=_cparams(("parallel",)),
        name="ffn_in_proj",
    )(x, *ffn_w, g, b, w_in, w_vt, *tabs)


def _rope_tables(s_len, group):
    half = group // 2
    inv = ROPE_THETA ** (-jnp.arange(0, group, 2, dtype=F32) / group)
    ang = jnp.arange(s_len, dtype=F32)[:, None] * inv[None, :]
    lane = np.arange(LANES)
    idx = lane % half
    sign = np.where((lane % group) < half, -1.0, 1.0).astype(np.float32)
    cos = jnp.cos(ang)[:, idx]
    sin = jnp.sin(ang)[:, idx] * sign[None, :]
    return cos, sin


V_ROWS = HEAD_DIM + 16
DIFF_GROUPS = 4
DIL_GROUPS = 2


def _attn_two_pass(q_stack_ref, k_ref, vt_ref, s_refs, acc_ref, qi, blk, n_comb, n_groups,
                   off_bias, diag_bias):
    gs = blk // n_groups
    per_head = n_comb // 2
    gcols = n_comb * gs

    def cols(g):
        return slice(g * gcols, (g + 1) * gcols)

    def tiled(bias):
        return jnp.concatenate([bias] * n_comb, axis=1)

    def k_at(kb, size):
        return k_ref[0, kb * blk:kb * blk + size, :]

    vx_cache = {}

    def vx_at(h, kb, size):
        if (h, kb, size) not in vx_cache:
            v = vt_ref[h * HEAD_DIM:(h + 1) * HEAD_DIM, kb * blk:kb * blk + size]
            vx_cache[h, kb, size] = jnp.concatenate(
                [v, jnp.ones((V_ROWS - HEAD_DIM, size), v.dtype)], axis=0)
        return vx_cache[h, kb, size]

    def col_max(m, s):
        part = jnp.max(s.reshape(s.shape[0] // SUBLANES, SUBLANES, s.shape[1]), axis=0)
        return part if m is None else jnp.maximum(m, part)

    mx = [None] * n_groups
    m = [None] * n_groups
    acc = [[None, None] for _ in range(n_groups)]

    def pass_a(g, kb):
        s = _dot(k_at(kb, blk), q_stack_ref[cols(g), :], _NT)
        bias = off_bias(kb)
        if bias is not None:
            s = s + tiled(bias[:, g * gs:(g + 1) * gs])
        s_refs[g][kb] = s
        mx[g] = col_max(mx[g], s)

    def pass_a_diag(g):
        nk = (g + 1) * gs
        s = _dot(k_at(qi, nk), q_stack_ref[cols(g), :], _NT)
        bias = diag_bias(g)
        if bias is not None:
            s = s + tiled(bias)
        s_refs[g][qi, 0:nk, :] = s
        m[g] = jnp.max(col_max(mx[g], s), axis=0, keepdims=True)

    def keys_of(g, kb):
        return blk if kb < qi else (g + 1) * gs

    def probs(g, kb):
        return jnp.exp2(s_refs[g][kb, 0:keys_of(g, kb), :] - m[g]).astype(BF16)

    def pass_b(g, kb, p):
        for h in range(2):
            d = _dot(vx_at(h, kb, keys_of(g, kb)), p[:, h * per_head * gs:(h + 1) * per_head * gs])
            acc[g][h] = d if acc[g][h] is None else acc[g][h] + d

    for kb in range(qi):
        pass_a(0, kb)
    pass_a_diag(0)
    for g in range(1, n_groups + 1):
        p = probs(g - 1, 0)
        for kb in range(qi + 1):
            if g < n_groups and kb < qi:
                pass_a(g, kb)
            elif g < n_groups:
                pass_a_diag(g)
            p_next = probs(g - 1, kb + 1) if kb < qi else None
            pass_b(g - 1, kb, p)
            p = p_next
    for h in range(2):
        acc_ref[h] = jnp.concatenate([acc[g][h] for g in range(n_groups)], axis=1)


def _normalized(acc_ref, c, n_comb, n_groups):
    per_head = n_comb // 2
    h, j = divmod(c, per_head)
    blk = acc_ref.shape[2] // per_head
    gs = blk // n_groups
    acc = acc_ref[h]
    acc = jnp.concatenate([acc[:, (g * per_head + j) * gs:(g * per_head + j + 1) * gs]
                           for g in range(n_groups)], axis=1)
    return acc[0:HEAD_DIM] / acc[HEAD_DIM:HEAD_DIM + 1]


def _store_q_stack(q_stack_ref, q, lane_masks, n_groups):
    gs = q.shape[0] // n_groups
    zero = jnp.zeros_like(q)
    n_comb = len(lane_masks)
    for g in range(n_groups):
        for c, mask in enumerate(lane_masks):
            r0 = (g * n_comb + c) * gs
            q_stack_ref[r0:r0 + gs, :] = jnp.where(mask, q, zero)[g * gs:(g + 1) * gs]


def _head_rms_t(o_t, g_t):
    sq = o_t * o_t
    ms = [jnp.sum(sq[h * HEAD_DIM:(h + 1) * HEAD_DIM], axis=0, keepdims=True) * (1.0 / HEAD_DIM)
          for h in range(2)]
    scale = jnp.concatenate([jnp.broadcast_to(lax.rsqrt(m + RMS_EPS), (HEAD_DIM, o_t.shape[1])) for m in ms],
                            axis=0)
    return o_t * scale * g_t


def _diff_attn_kernel(lam_ref, gt_ref, q_ref, k_ref, vt_ref, o_ref, q4_ref, acc_ref, *s_refs,
                      blk, lam_init):
    n_groups = DIFF_GROUPS
    gs = blk // n_groups
    lane = lax.broadcasted_iota(jnp.int32, (1, LANES), 1)
    lane_masks = [(lane // A_QK_DIM) == c for c in range(4)]

    def causal(g):
        key = lax.broadcasted_iota(jnp.int32, ((g + 1) * gs, gs), 0)
        qry = lax.broadcasted_iota(jnp.int32, ((g + 1) * gs, gs), 1) + g * gs
        return jnp.where(key <= qry, 0.0, NEG_INF)

    lp = lam_ref[...]
    lam = (jnp.exp(jnp.sum(lp[0:1] * lp[1:2], axis=1, keepdims=True))
           - jnp.exp(jnp.sum(lp[2:3] * lp[3:4], axis=1, keepdims=True)) + lam_init)
    for n in range(k_ref.shape[1] // blk):
        rows = slice(n * blk, (n + 1) * blk)
        _store_q_stack(q4_ref, q_ref[0, rows, :], lane_masks, n_groups)
        _attn_two_pass(q4_ref, k_ref, vt_ref, s_refs, acc_ref, n, blk, 4, n_groups, lambda kb: None, causal)
        o_t = jnp.concatenate([_normalized(acc_ref, 2 * h, 4, n_groups)
                               - lam * _normalized(acc_ref, 2 * h + 1, 4, n_groups)
                               for h in range(2)], axis=0)
        o_t = _head_rms_t(o_t, gt_ref[...]) * (1.0 - lam_init)
        o_ref[0, rows, :] = o_t.T.astype(o_ref.dtype)


def _attn_call(kernel_fn, name, n_comb, n_groups, consts, const_specs, q, k, vt, blk):
    b, s, w = q.shape
    pairs = w // LANES
    return pl.pallas_call(
        kernel_fn,
        out_shape=jax.ShapeDtypeStruct((b, s, w), BF16),
        grid=(b, pairs),
        in_specs=const_specs + [
            pl.BlockSpec((1, s, LANES), lambda bi, p: (bi, 0, p)),
            pl.BlockSpec((1, s, LANES), lambda bi, p: (bi, 0, p)),
            pl.BlockSpec((LANES, s), lambda bi, p: (p, bi)),
        ],
        out_specs=pl.BlockSpec((1, s, LANES), lambda bi, p: (bi, 0, p)),
        scratch_shapes=[pltpu.VMEM((n_comb * blk, LANES), BF16),
                        pltpu.VMEM((2, V_ROWS, n_comb // 2 * blk), F32)]
        + [pltpu.VMEM((s // blk, blk, n_comb * blk // n_groups), F32) for _ in range(n_groups)],
        compiler_params=_cparams(("parallel", "parallel")),
        name=name,
    )(*consts, q, k, vt)


def _diff_attn(q, k, vt, lam_p, g_t, lam_init):
    blk = g_t.shape[1]
    return _attn_call(functools.partial(_diff_attn_kernel, blk=blk, lam_init=lam_init), "diff_attn", 4, DIFF_GROUPS,
                      [lam_p, g_t], [_resident((4, LANES)), _resident(g_t.shape)], q, k, vt, blk)


def _dilated_log2_counts(blk):
    far = max(w for w, _ in DILATED_PATTERNS if w < max(p[0] for p in DILATED_PATTERNS))
    n_off = far // blk + 2
    j = np.arange(blk)[:, None]
    i = np.arange(blk)[None, :]
    cnt = np.zeros((n_off, blk, blk), np.float64)
    for o in range(n_off):
        delta = o * blk + i - j
        for window, dil in DILATED_PATTERNS:
            cnt[o] += ((delta >= 0) & (delta <= window) & (delta % dil == 0))
    return np.where(cnt > 0, np.log2(np.maximum(cnt, 1.0)), NEG_INF).astype(np.float32)


def _dil_attn_kernel(bias_ref, gt_ref, q_ref, k_ref, vt_ref, o_ref, q2_ref, acc_ref, *s_refs,
                     blk, n_off):
    n_groups = DIL_GROUPS
    gs = blk // n_groups
    lane = lax.broadcasted_iota(jnp.int32, (1, LANES), 1)
    low = lane < HEAD_DIM
    for n in range(k_ref.shape[1] // blk):
        rows = slice(n * blk, (n + 1) * blk)
        _store_q_stack(q2_ref, q_ref[0, rows, :], [low, jnp.logical_not(low)], n_groups)
        _attn_two_pass(q2_ref, k_ref, vt_ref, s_refs, acc_ref, n, blk, 2, n_groups,
                       lambda kb: bias_ref[min(n - kb, n_off - 1)],
                       lambda g: bias_ref[0, 0:(g + 1) * gs, g * gs:(g + 1) * gs])
        o_t = jnp.concatenate([_normalized(acc_ref, c, 2, n_groups) for c in range(2)], axis=0)
        o_ref[0, rows, :] = _head_rms_t(o_t, gt_ref[...]).T.astype(o_ref.dtype)


def _dil_attn(q, k, vt, bias, g_t):
    blk = bias.shape[1]
    return _attn_call(functools.partial(_dil_attn_kernel, blk=blk, n_off=bias.shape[0]), "dil_attn", 2, DIL_GROUPS,
                      [bias, g_t], [_resident(bias.shape), _resident(g_t.shape)], q, k, vt, blk)


def _solve_unit_lower(a_list, x_list):
    n = len(x_list)
    c = x_list[0].shape[0]
    nb = c // SOLVE_BLOCK
    sub = SOLVE_BLOCK // SUBLANES
    a_split = [_split2(a[:, 0:c]) for a in a_list]
    xs = [[x[SOLVE_BLOCK * i:SOLVE_BLOCK * (i + 1)] for i in range(nb)] for x in x_list]
    zeros = jnp.zeros((SOLVE_BLOCK, LANES), F32)
    for b in range(nb):
        base = b * SOLVE_BLOCK
        rows = [[xs[u][b][SUBLANES * i:SUBLANES * (i + 1)] for i in range(sub)] for u in range(n)]
        for t in range(SOLVE_BLOCK - 1):
            bi = t // SUBLANES
            for u in range(n):
                xt = rows[u][bi][t % SUBLANES:t % SUBLANES + 1, :]
                for i in range(bi, sub):
                    r0 = base + SUBLANES * i
                    a_col = a_list[u][r0:r0 + SUBLANES, base + t:base + t + 1]
                    rows[u][i] = rows[u][i] - a_col * xt
        for u in range(n):
            xs[u][b] = jnp.concatenate(rows[u], axis=0)
        if b + 1 < nb:
            below = slice(base + SOLVE_BLOCK, c)
            for u in range(n):
                a_hi, a_lo = a_split[u]
                x_hi, x_lo = _split2(jnp.concatenate(
                    [zeros] * b + [xs[u][b]] + [zeros] * (nb - 1 - b), axis=0))
                upd = _dot(a_lo[below], x_hi) + _dot(a_hi[below], x_lo) + _dot(a_hi[below], x_hi)
                for j in range(b + 1, nb):
                    xs[u][j] = xs[u][j] - upd[(j - b - 1) * SOLVE_BLOCK:(j - b) * SOLVE_BLOCK]
    return [jnp.concatenate(x, axis=0) for x in xs]


def _to_slab(col_block, half, lower):
    lane = lax.broadcasted_iota(jnp.int32, (1, LANES), 1)
    want_low = lower
    have_low = half == 0
    src = col_block if want_low == have_low else pltpu.roll(col_block, HEAD_DIM, 1)
    keep = (lane < HEAD_DIM) if want_low else (lane >= HEAD_DIM)
    return jnp.where(keep, src, 0.0)


def _rwkv_kernel(*refs, n_chunks, has_vres):
    if has_vres:
        (cc_ref, vf_ref, mu_ref, w0_ref, w2_ref, a0_ref, a2_ref, g2_ref, kk_ref, ka_ref, rk_ref,
         gng_ref, gnb_ref, v0_ref, v1_ref, v2_ref, bd_ref, tri_ref,
         o_ref, carry_ref, hs_ref) = refs
    else:
        (cc_ref, mu_ref, w0_ref, w2_ref, a0_ref, a2_ref, g2_ref, kk_ref, ka_ref, rk_ref,
         gng_ref, gnb_ref, bd_ref, tri_ref,
         o_ref, vf_out_ref, carry_ref, hs_ref) = refs
    cw = C_WIDTH
    cl = RWKV_CHUNK

    @pl.when(pl.program_id(1) == 0)
    def _():
        carry_ref[...] = jnp.zeros(carry_ref.shape, F32)
        hs_ref[...] = jnp.zeros(hs_ref.shape, F32)

    c_raw = cc_ref[0]
    tb = c_raw.shape[0]
    row = lax.broadcasted_iota(jnp.int32, (tb, 1), 0)
    prev = jnp.where(row == 0, carry_ref[...], pltpu.roll(c_raw, 1, 0))
    carry_ref[...] = c_raw[tb - 1:tb, :]
    c = c_raw + (prev - c_raw) * mu_ref[...]

    r = c[:, 0:cw]
    k = c[:, cw:2 * cw]
    v = c[:, 2 * cw:3 * cw]
    xwa = c[:, 3 * cw:3 * cw + LANES]
    xg = c[:, 3 * cw + LANES:3 * cw + 2 * LANES]
    if has_vres:
        mix = jax.nn.sigmoid(v0_ref[...] + _dot2r(_dot2r(v, v1_ref[...]), v2_ref[...]))
        v = v + (vf_ref[0] - v) * mix
    else:
        vf_out_ref[0] = v
    bd = bd_ref[...]
    lw = -DECAY_RATE * jax.nn.sigmoid(w0_ref[...] + _dot1(jnp.tanh(xwa), w2_ref[...]))
    a = jax.nn.sigmoid(a0_ref[...] + _dot1(xwa, a2_ref[...]))
    gate = _dot2r(jax.nn.sigmoid(xg), g2_ref[...])
    kkv = k * kk_ref[...]
    kappa = kkv / jnp.maximum(jnp.sqrt(_dot_sel_rhs(kkv * kkv, bd)), 1e-12)
    k2 = k * (1.0 + (a - 1.0) * ka_ref[...])
    bonus = _dot_sel_rhs(r * k2 * rk_ref[...], bd)
    beta = a * kappa

    lane = lax.broadcasted_iota(jnp.int32, (1, LANES), 1)
    low = lane < HEAD_DIM
    lane2 = lax.broadcasted_iota(jnp.int32, (1, cw), 1)
    rr = lax.broadcasted_iota(jnp.int32, (2 * cl, LANES), 0)
    cidx = lax.broadcasted_iota(jnp.int32, (2 * cl, LANES), 1) % cl
    tri_mask = cidx < jnp.where(rr < cl, rr, rr - cl + 1)
    eye = (lax.broadcasted_iota(jnp.int32, (cl, LANES), 0)
           == lax.broadcasted_iota(jnp.int32, (cl, LANES), 1))
    tri = tri_ref[...]

    zeros = jnp.zeros((cl, LANES), F32)
    units = []
    for ci in range(n_chunks):
        sl = slice(ci * cl, (ci + 1) * cl)
        lwc = lw[sl]
        cum = _dot_sel_lhs(tri, lwc)
        cum_end = cum[cl - 1:cl, :]
        rt = r[sl] * jnp.exp(cum)
        kt = kappa[sl] * jnp.exp(cum - lwc)
        e_inv = jnp.exp(-cum)
        e_end = jnp.exp(cum_end - cum)
        bh = beta[sl] * e_end
        kh = k2[sl] * e_end
        p_end = jnp.exp(cum_end)
        vc = v[sl]
        right = jnp.concatenate([beta[sl] * e_inv, k2[sl] * e_inv], axis=0).astype(BF16)
        left_hi, left_lo = _split2(jnp.concatenate([kt, rt], axis=0))
        for h in range(C_HEADS):
            cb, half = divmod(h, 2)
            cs = slice(cb * LANES, (cb + 1) * LANES)
            hm = ((lane2 // HEAD_DIM) == h).astype(BF16)
            units.append(dict(
                ci=ci, h=h, right=right, left_hi=left_hi * hm, left_lo=left_lo * hm,
                vw=_to_slab(vc[:, cs], half, lower=False),
                kw=_to_slab(kt[:, cs], half, lower=True),
                rw=_to_slab(rt[:, cs], half, lower=True),
                bkw=jnp.concatenate([_to_slab(bh[:, cs], half, True), _to_slab(kh[:, cs], half, True)], axis=0),
                pe=_to_slab(jnp.broadcast_to(p_end[:, cs], (cl, LANES)), half, True)))
    def stage_m(us):
        for u in us:
            u["m"] = jnp.where(tri_mask, _dot(u["left_lo"], u["right"], _NT)
                               + _dot(u["left_hi"], u["right"], _NT), 0.0)

    def stage_rhs(us):
        for u in us:
            u["rhs"] = u["kw"] + _dot1(u["m"][0:cl], jnp.concatenate([zeros, u["vw"]], axis=0))

    def stage_solve(us):
        xs = _solve_unit_lower([u["m"][0:cl] for u in us], [u["rhs"] for u in us])
        for u, x in zip(us, xs):
            u["st"] = jnp.concatenate([-x, u["vw"]], axis=0)

    def stage_qy(us):
        for u in us:
            u["qy"] = u["rw"] + _dot2r(u["m"][cl:2 * cl], u["st"])

    def stage_gh(us):
        for u in us:
            u["gh"] = jnp.where(eye, u["pe"], 0.0) + _dot1(u["bkw"].T[0:cl], u["st"])

    def state_step(ci):
        for u in units[ci * C_HEADS:(ci + 1) * C_HEADS]:
            hs = hs_ref[u["h"]]
            u["y"] = u["qy"] + _dot1(u["qy"], hs)
            hs_ref[u["h"], 0:cl, :] = jnp.where(low, 0.0, u["gh"] + _dot2r(u["gh"], hs))

    n_first = n_chunks // 2
    first, second = units[:n_first * C_HEADS], units[n_first * C_HEADS:]
    for stage in (stage_m, stage_rhs, stage_solve, stage_qy, stage_gh):
        if first:
            stage(first)
    pending = list(range(n_first))
    for stage in (stage_m, stage_rhs, stage_solve, stage_qy, stage_gh):
        stage(second)
        if pending:
            state_step(pending.pop(0))
    for ci in pending + list(range(n_first, n_chunks)):
        state_step(ci)
    y_chunks = []
    for ci in range(n_chunks):
        yh = [u["y"] for u in units if u["ci"] == ci]
        cols = [jnp.where(low, pltpu.roll(yh[2 * cb], HEAD_DIM, 1), yh[2 * cb + 1])
                for cb in range(C_HEADS // 2)]
        y_chunks.append(jnp.concatenate(cols, axis=1))
    y = jnp.concatenate(y_chunks, axis=0) if n_chunks > 1 else y_chunks[0]

    inv = 1.0 / HEAD_DIM
    mean = _dot_sel_rhs(y, bd) * inv
    d = y - mean
    var = _dot_sel_rhs(d * d, bd) * inv
    yn = d * lax.rsqrt(var + C_GN_EPS) * gng_ref[...] + gnb_ref[...]
    out = (yn + bonus * v) * gate
    o_ref[0] = out.astype(o_ref.dtype)


def _rwkv(cc, v_first, p, consts):
    b, s, _ = cc.shape
    n_chunks = RWKV_CHUNKS_PER_STEP
    tb = n_chunks * RWKV_CHUNK
    has_vres = v_first is not None
    blk_cc = pl.BlockSpec((1, tb, C_COLS), lambda bi, j: (bi, j, 0))
    blk_cw = pl.BlockSpec((1, tb, C_WIDTH), lambda bi, j: (bi, j, 0))
    vec = lambda n: _resident((1, n))
    params = [p["mu"], p["w0"], p["w2"], p["a0"], p["a2"], p["g2"], p["k_k"], p["k_a"], p["r_k"],
              p["gn_g"], p["gn_b"]]
    specs = [vec(C_COLS), vec(C_WIDTH), _resident((LANES, C_WIDTH)), vec(C_WIDTH),
             _resident((LANES, C_WIDTH)), _resident((LANES, C_WIDTH)), vec(C_WIDTH), vec(C_WIDTH),
             vec(C_WIDTH), vec(C_WIDTH), vec(C_WIDTH)]
    if has_vres:
        params += [p["v0"], p["v1"], p["v2"]]
        specs += [vec(C_WIDTH), _resident((C_WIDTH, LANES)), _resident((LANES, C_WIDTH))]
    params += [consts["bd"], consts["tri"]]
    specs += [_resident((C_WIDTH, C_WIDTH)), _resident((RWKV_CHUNK, RWKV_CHUNK))]
    o_shape = jax.ShapeDtypeStruct((b, s, C_WIDTH), BF16)
    if has_vres:
        args = [cc, v_first] + params
        in_specs = [blk_cc, blk_cw] + specs
        out_shape, out_specs = o_shape, blk_cw
    else:
        args = [cc] + params
        in_specs = [blk_cc] + specs
        out_shape = (o_shape, jax.ShapeDtypeStruct((b, s, C_WIDTH), F32))
        out_specs = (blk_cw, blk_cw)
    return pl.pallas_call(
        functools.partial(_rwkv_kernel, n_chunks=n_chunks, has_vres=has_vres),
        out_shape=out_shape,
        grid=(b, s // tb),
        in_specs=in_specs,
        out_specs=out_specs,
        scratch_shapes=[pltpu.VMEM((1, C_COLS), F32),
                        pltpu.VMEM((C_HEADS, 2 * RWKV_CHUNK, LANES), F32)],
        compiler_params=_cparams(("parallel", "arbitrary")),
        name="rwkv_vres" if has_vres else "rwkv_first",
    )(*args)


def _out_proj_ffn_kernel(x_ref, oa_ref, ob_ref, oc_ref, w_ref, g1_ref, b1_ref,
                         wg_ref, wu_ref, wd_ref, g2_ref, b2_ref, o_ref, *, alpha):
    mix = _dot(jnp.concatenate([oa_ref[...], ob_ref[...], oc_ref[...]], axis=1), w_ref[...])
    h = _layer_norm(alpha * x_ref[...] + mix, g1_ref[...], b1_ref[...])
    o_ref[...] = _ffn_ln_rows(h, wg_ref, wu_ref, wd_ref, g2_ref[...], b2_ref[...], alpha)


def _out_proj_ffn(x, oa, ob, oc, w_out, g1, b1, ffn_w, g2, b2, alpha, layer):
    t, d = x.shape
    d_ff = ffn_w[0].shape[2]
    tm = min(ROW_TILE, t)
    row = lambda i: (i, 0)
    return pl.pallas_call(
        functools.partial(_out_proj_ffn_kernel, alpha=alpha),
        out_shape=jax.ShapeDtypeStruct((t, d), F32),
        grid=(t // tm,),
        in_specs=[pl.BlockSpec((tm, d), row), pl.BlockSpec((tm, A_WIDTH), row),
                  pl.BlockSpec((tm, B_WIDTH), row), pl.BlockSpec((tm, C_WIDTH), row),
                  _resident_layer(w_out.shape[1:], layer), _resident((1, d)), _resident((1, d))]
        + _ffn_specs(d, d_ff, layer),
        out_specs=pl.BlockSpec((tm, d), row),
        compiler_params=_cparams(("parallel",)),
        name="out_proj_ffn",
    )(x, oa, ob, oc, w_out, g1, b1, *ffn_w, g2, b2)


def _pad_rows(w, total, offset):
    return jnp.zeros((total, w.shape[1]), w.dtype).at[offset:offset + w.shape[0]].set(w)


def kernel(x, ffn_a_gate, ffn_a_up, ffn_a_down, ffn_b_gate, ffn_b_up, ffn_b_down, ln_g, ln_b, w_in, w_out, a_lam_q1, a_lam_k1, a_lam_q2, a_lam_k2, a_norm_g, b_norm_g, c_mu, c_w0, c_w2, c_a0, c_a2, c_g2, c_k_k, c_k_a, c_r_k, c_gn_g, c_gn_b, c_v0, c_v1, c_v2):
    bsz, s_len, d_model = x.shape
    depth = w_in.shape[0]
    alpha = (2 * depth) ** 0.25
    t = bsz * s_len

    tabs = _rope_tables(s_len, A_QK_DIM) + _rope_tables(s_len, HEAD_DIM)
    dil_bias = jnp.asarray(_dilated_log2_counts(min(ATT_BLOCK, s_len)))
    head_of = np.arange(C_WIDTH) // HEAD_DIM
    consts = {
        "bd": jnp.asarray(head_of[:, None] == head_of[None, :], BF16),
        "tri": jnp.asarray(np.tril(np.ones((RWKV_CHUNK, RWKV_CHUNK), np.float32)), BF16),
    }
    row = lambda vct: vct.reshape(1, -1)
    att_blk = min(ATT_BLOCK, s_len)
    gain_t = lambda vct: jnp.broadcast_to(jnp.tile(vct, LANES // HEAD_DIM)[:, None], (LANES, att_blk))

    ffn_a = (ffn_a_gate.astype(BF16), ffn_a_up.astype(BF16), ffn_a_down.astype(BF16))
    ffn_b = (ffn_b_gate.astype(BF16), ffn_b_up.astype(BF16), ffn_b_down.astype(BF16))
    w_in_b = w_in.astype(BF16)
    v_cols = np.r_[2 * A_WIDTH:3 * A_WIDTH, 3 * A_WIDTH + 2 * B_WIDTH:3 * A_WIDTH + 3 * B_WIDTH]
    w_vt_b = jnp.swapaxes(w_in_b[:, :, v_cols], 1, 2)
    w_out_b = w_out.astype(BF16)

    h = x.reshape(t, d_model)
    v_first = None
    for l in range(depth):
        h, aq, ak, avt, bq, bk, bvt, cc = _ffn_in_proj(h, ffn_a, row(ln_g[l, 0]), row(ln_b[l, 0]),
                                                       w_in_b, w_vt_b, tabs, s_len, alpha, l)
        shp = (bsz, s_len, A_WIDTH)

        lam_init = 0.8 - 0.6 * math.exp(-0.3 * l)
        lam_p = jnp.zeros((4, LANES), F32).at[:, :A_QK_DIM].set(
            jnp.stack([a_lam_q1[l], a_lam_k1[l], a_lam_q2[l], a_lam_k2[l]]))
        o_a = _diff_attn(aq.reshape(shp), ak.reshape(shp), avt, lam_p, gain_t(a_norm_g[l]), lam_init)
        o_b = _dil_attn(bq.reshape(shp), bk.reshape(shp), bvt, dil_bias, gain_t(b_norm_g[l]))

        p = {"mu": row(c_mu[l]), "w0": row(c_w0[l]), "a0": row(c_a0[l]),
             "w2": _pad_rows(c_w2[l], LANES, 0), "a2": _pad_rows(c_a2[l], LANES, C_W_RANK),
             "g2": c_g2[l], "k_k": row(c_k_k[l]), "k_a": row(c_k_a[l]), "r_k": row(c_r_k[l]),
             "gn_g": row(c_gn_g[l]), "gn_b": row(c_gn_b[l])}
        cc3 = cc.reshape(bsz, s_len, C_COLS)
        if l == 0:
            o_c, v_first = _rwkv(cc3, None, p, consts)
        else:
            p["v0"] = row(c_v0[l - 1])
            p["v1"] = jnp.zeros((C_WIDTH, LANES), F32).at[:, :C_V_RANK].set(c_v1[l - 1])
            p["v2"] = _pad_rows(c_v2[l - 1], LANES, 0)
            o_c = _rwkv(cc3, v_first, p, consts)

        h = _out_proj_ffn(h, o_a.reshape(t, A_WIDTH), o_b.reshape(t, B_WIDTH), o_c.reshape(t, C_WIDTH),
                          w_out_b, row(ln_g[l, 1]), row(ln_b[l, 1]),
                          ffn_b, row(ln_g[l, 2]), row(ln_b[l, 2]), alpha, l)
    return h.reshape(bsz, s_len, d_model)
```

```python
import functools
import math

import numpy as np
import jax
import jax.numpy as jnp
from jax import lax
from jax.experimental import pallas as pl
from jax.experimental.pallas import tpu as pltpu

F32 = jnp.float32
BF16 = jnp.bfloat16

HEAD_DIM = 64
A_HEADS = 6
B_HEADS = 6
C_HEADS = 4
A_QK_DIM = 32
A_WIDTH = A_HEADS * HEAD_DIM
B_WIDTH = B_HEADS * HEAD_DIM
C_WIDTH = C_HEADS * HEAD_DIM
C_W_RANK = 64
C_A_RANK = 64
C_V_RANK = 32
C_G_RANK = 128
C_COLS = 3 * C_WIDTH + C_W_RANK + C_A_RANK + C_G_RANK
DILATED_PATTERNS = ((128, 1), (512, 4), (2048, 16))
ROPE_THETA = 10000.0
LN_EPS = 1e-5
RMS_EPS = 1e-5
C_GN_EPS = 64e-5
NEG_INF = -1e30
DECAY_RATE = math.exp(-0.5)
LOG2_E = math.log2(math.e)

LANES = 128
SUBLANES = 8
VMEM_LIMIT_BYTES = 56 * 1024 * 1024

ROW_TILE = 512
FF_CHUNK = 256
ATT_BLOCK = 512
RWKV_CHUNK = 64
RWKV_CHUNKS_PER_STEP = 8
SOLVE_BLOCK = 16

_NN = (((1,), (0,)), ((), ()))
_NT = (((1,), (1,)), ((), ()))


def _dot(a, b, dims=_NN):
    return lax.dot_general(a, b, dims, preferred_element_type=F32)


def _split2(x):
    hi = x.astype(BF16)
    lo = (x - hi.astype(F32)).astype(BF16)
    return hi, lo


def _split3(x):
    hi = x.astype(BF16)
    r = x - hi.astype(F32)
    mid = r.astype(BF16)
    lo = (r - mid.astype(F32)).astype(BF16)
    return hi, mid, lo


def _dot2r(a, b, dims=_NN):
    a_hi = a.astype(BF16)
    b_hi, b_lo = _split2(b)
    return _dot(a_hi, b_lo, dims) + _dot(a_hi, b_hi, dims)


def _dot1(a, b, dims=_NN):
    return _dot(a.astype(BF16), b.astype(BF16), dims)


def _dot_sel_lhs(sel, x, dims=_NN):
    h, m, l = _split3(x)
    return _dot(sel, l, dims) + _dot(sel, m, dims) + _dot(sel, h, dims)


def _dot_sel_rhs(x, sel):
    h, l = _split2(x)
    return _dot(l, sel) + _dot(h, sel)


def _layer_norm(y, g, b):
    mu = jnp.mean(y, axis=-1, keepdims=True)
    d = y - mu
    var = jnp.mean(d * d, axis=-1, keepdims=True)
    return d * lax.rsqrt(var + LN_EPS) * g + b


def _cparams(sem):
    return pltpu.CompilerParams(dimension_semantics=sem, vmem_limit_bytes=VMEM_LIMIT_BYTES)


def _resident(shape):
    nd = len(shape)
    return pl.BlockSpec(shape, lambda *_: (0,) * nd, pipeline_mode=pl.Buffered(1))


def _resident_layer(shape, layer):
    nd = len(shape)
    return pl.BlockSpec((None,) + tuple(shape), lambda *_: (layer,) + (0,) * nd,
                        pipeline_mode=pl.Buffered(1))


def _ffn_ln_rows(x, wg_ref, wu_ref, wd_ref, g, b, alpha):
    d_ff = wg_ref.shape[1]
    xb = x.astype(BF16)
    acc = jnp.zeros(x.shape, F32)
    for c0 in range(0, d_ff, FF_CHUNK):
        gate = _dot(xb, wg_ref[:, c0:c0 + FF_CHUNK])
        up = _dot(xb, wu_ref[:, c0:c0 + FF_CHUNK])
        h = (gate * jax.nn.sigmoid(gate)) * up
        acc = acc + _dot(h.astype(BF16), wd_ref[c0:c0 + FF_CHUNK, :])
    return _layer_norm(alpha * x + 0.5 * acc, g, b)


def _ffn_specs(d, d_ff, layer):
    return [_resident_layer((d, d_ff), layer), _resident_layer((d, d_ff), layer),
            _resident_layer((d_ff, d), layer), _resident((1, d)), _resident((1, d))]


def _rope_lanes(y, cos, sin_signed, group):
    half = group // 2
    lane = lax.broadcasted_iota(jnp.int32, (1, LANES), 1)
    first = (lane % group) < half
    swapped = jnp.where(first, pltpu.roll(y, LANES - half, 1), pltpu.roll(y, half, 1))
    return y * cos + swapped * sin_signed


def _in_proj_rows(x, w_ref, wvt_ref, ca_ref, sa_ref, cb_ref, sb_ref,
                  aq_ref, ak_ref, avt_ref, bq_ref, bk_ref, bvt_ref, cc_ref):
    xb = x.astype(BF16)
    ca, sa, cb, sb = ca_ref[...], sa_ref[...], cb_ref[...], sb_ref[...]
    a_scale = A_QK_DIM ** -0.5 * LOG2_E
    b_scale = HEAD_DIM ** -0.5 * LOG2_E
    b_col = 3 * A_WIDTH
    plan = ((aq_ref, 0, (ca, sa, A_QK_DIM), a_scale),
            (ak_ref, A_WIDTH, (ca, sa, A_QK_DIM), None),
            (bq_ref, b_col, (cb, sb, HEAD_DIM), b_scale),
            (bk_ref, b_col + B_WIDTH, (cb, sb, HEAD_DIM), None))
    segs = [(ref, s0, col + s0, rope, scale) for ref, col, rope, scale in plan
            for s0 in range(0, A_WIDTH, LANES)]
    wide = 2 * LANES
    for i in range(0, len(segs), wide // LANES):
        c0 = segs[i][2]
        y = _dot(xb, w_ref[:, c0:c0 + wide])
        for j in range(wide // LANES):
            ref, s0, _, rope, scale = segs[i + j]
            z = _rope_lanes(y[:, j * LANES:(j + 1) * LANES], *rope)
            if scale is not None:
                z = z * scale
            ref[:, s0:s0 + LANES] = z.astype(ref.dtype)
    col = b_col + 3 * B_WIDTH
    for s0 in range(0, C_COLS, wide):
        cc_ref[:, s0:s0 + wide] = _dot(xb, w_ref[:, col + s0:col + s0 + wide])
    avt_ref[...] = _dot(wvt_ref[0:A_WIDTH, :], xb, _NT).astype(avt_ref.dtype)
    bvt_ref[...] = _dot(wvt_ref[A_WIDTH:A_WIDTH + B_WIDTH, :], xb, _NT).astype(bvt_ref.dtype)


def _ffn_in_proj_kernel(x_ref, wg_ref, wu_ref, wd_ref, g_ref, b_ref, w_ref, wvt_ref,
                        ca_ref, sa_ref, cb_ref, sb_ref, h_ref, *out_refs, alpha):
    h = _ffn_ln_rows(x_ref[...], wg_ref, wu_ref, wd_ref, g_ref[...], b_ref[...], alpha)
    h_ref[...] = h
    _in_proj_rows(h, w_ref, wvt_ref, ca_ref, sa_ref, cb_ref, sb_ref, *out_refs)


def _ffn_in_proj(x, ffn_w, g, b, w_in, w_vt, tabs, s_len, alpha, layer):
    t, d = x.shape
    d_ff = ffn_w[0].shape[2]
    tm = min(ROW_TILE, s_len)
    spb = s_len // tm
    row = lambda i: (i, 0)
    tab = pl.BlockSpec((tm, LANES), lambda i: (i % spb, 0))
    qk = jax.ShapeDtypeStruct((t, A_WIDTH), BF16)
    vt = jax.ShapeDtypeStruct((A_WIDTH, t), BF16)
    qk_spec = pl.BlockSpec((tm, A_WIDTH), row)
    vt_spec = pl.BlockSpec((A_WIDTH, tm), lambda i: (0, i))
    return pl.pallas_call(
        functools.partial(_ffn_in_proj_kernel, alpha=alpha),
        out_shape=(jax.ShapeDtypeStruct((t, d), F32), qk, qk, vt, qk, qk, vt,
                   jax.ShapeDtypeStruct((t, C_COLS), F32)),
        grid=(t // tm,),
        in_specs=[pl.BlockSpec((tm, d), row)] + _ffn_specs(d, d_ff, layer)
        + [_resident_layer(w_in.shape[1:], layer), _resident_layer(w_vt.shape[1:], layer), tab, tab, tab, tab],
        out_specs=(pl.BlockSpec((tm, d), row), qk_spec, qk_spec, vt_spec, qk_spec, qk_spec, vt_spec,
                   pl.BlockSpec((tm, C_COLS), row)),
        compiler_params=_cparams(("parallel",)),
        name="ffn_in_proj",
    )(x, *ffn_w, g, b, w_in, w_vt, *tabs)


def _rope_tables(s_len, group):
    half = group // 2
    inv = ROPE_THETA ** (-jnp.arange(0, group, 2, dtype=F32) / group)
    ang = jnp.arange(s_len, dtype=F32)[:, None] * inv[None, :]
    lane = np.arange(LANES)
    idx = lane % half
    sign = np.where((lane % group) < half, -1.0, 1.0).astype(np.float32)
    cos = jnp.cos(ang)[:, idx]
    sin = jnp.sin(ang)[:, idx] * sign[None, :]
    return cos, sin


V_ROWS = HEAD_DIM + 16
DIFF_GROUPS = 4
DIL_GROUPS = 2


def _attn_two_pass(q_stack_ref, k_ref, vt_ref, s_refs, acc_ref, qi, blk, n_comb, n_groups,
                   off_bias, diag_bias):
    gs = blk // n_groups
    per_head = n_comb // 2
    gcols = n_comb * gs

    def cols(g):
        return slice(g * gcols, (g + 1) * gcols)

    def tiled(bias):
        return jnp.concatenate([bias] * n_comb, axis=1)

    def k_at(kb, size):
        return k_ref[0, kb * blk:kb * blk + size, :]

    vx_cache = {}

    def vx_at(h, kb, size):
        if (h, kb, size) not in vx_cache:
            v = vt_ref[h * HEAD_DIM:(h + 1) * HEAD_DIM, kb * blk:kb * blk + size]
            vx_cache[h, kb, size] = jnp.concatenate(
                [v, jnp.ones((V_ROWS - HEAD_DIM, size), v.dtype)], axis=0)
        return vx_cache[h, kb, size]

    def col_max(m, s):
        part = jnp.max(s.reshape(s.shape[0] // SUBLANES, SUBLANES, s.shape[1]), axis=0)
        return part if m is None else jnp.maximum(m, part)

    mx = [None] * n_groups
    m = [None] * n_groups
    acc = [[None, None] for _ in range(n_groups)]

    def pass_a(g, kb):
        s = _dot(k_at(kb, blk), q_stack_ref[cols(g), :], _NT)
        bias = off_bias(kb)
        if bias is not None:
            s = s + tiled(bias[:, g * gs:(g + 1) * gs])
        s_refs[g][kb] = s
        mx[g] = col_max(mx[g], s)

    def pass_a_diag(g):
        nk = (g + 1) * gs
        s = _dot(k_at(qi, nk), q_stack_ref[cols(g), :], _NT)
        bias = diag_bias(g)
        if bias is not None:
            s = s + tiled(bias)
        s_refs[g][qi, 0:nk, :] = s
        m[g] = jnp.max(col_max(mx[g], s), axis=0, keepdims=True)

    def keys_of(g, kb):
        return blk if kb < qi else (g + 1) * gs

    def probs(g, kb):
        return jnp.exp2(s_refs[g][kb, 0:keys_of(g, kb), :] - m[g]).astype(BF16)

    def pass_b(g, kb, p):
        for h in range(2):
            d = _dot(vx_at(h, kb, keys_of(g, kb)), p[:, h * per_head * gs:(h + 1) * per_head * gs])
            acc[g][h] = d if acc[g][h] is None else acc[g][h] + d

    for kb in range(qi):
        pass_a(0, kb)
    pass_a_diag(0)
    for g in range(1, n_groups + 1):
        p = probs(g - 1, 0)
        for kb in range(qi + 1):
            if g < n_groups and kb < qi:
                pass_a(g, kb)
            elif g < n_groups:
                pass_a_diag(g)
            p_next = probs(g - 1, kb + 1) if kb < qi else None
            pass_b(g - 1, kb, p)
            p = p_next
    for h in range(2):
        acc_ref[h] = jnp.concatenate([acc[g][h] for g in range(n_groups)], axis=1)


def _for_each_query_block(qi, n_q, fn):
    for n in range(n_q):
        @pl.when(qi == n)
        def _(n=n):
            fn(n)


def _normalized(acc_ref, c, n_comb, n_groups):
    per_head = n_comb // 2
    h, j = divmod(c, per_head)
    blk = acc_ref.shape[2] // per_head
    gs = blk // n_groups
    acc = acc_ref[h]
    acc = jnp.concatenate([acc[:, (g * per_head + j) * gs:(g * per_head + j + 1) * gs]
                           for g in range(n_groups)], axis=1)
    return acc[0:HEAD_DIM] / acc[HEAD_DIM:HEAD_DIM + 1]


def _store_q_stack(q_stack_ref, q, lane_masks, n_groups):
    gs = q.shape[0] // n_groups
    zero = jnp.zeros_like(q)
    n_comb = len(lane_masks)
    for g in range(n_groups):
        for c, mask in enumerate(lane_masks):
            r0 = (g * n_comb + c) * gs
            q_stack_ref[r0:r0 + gs, :] = jnp.where(mask, q, zero)[g * gs:(g + 1) * gs]


def _head_rms_t(o_t, g_t):
    sq = o_t * o_t
    ms = [jnp.sum(sq[h * HEAD_DIM:(h + 1) * HEAD_DIM], axis=0, keepdims=True) * (1.0 / HEAD_DIM)
          for h in range(2)]
    scale = jnp.concatenate([jnp.broadcast_to(lax.rsqrt(m + RMS_EPS), (HEAD_DIM, o_t.shape[1])) for m in ms],
                            axis=0)
    return o_t * scale * g_t


def _diff_attn_kernel(lam_ref, gt_ref, q_ref, k_ref, vt_ref, o_ref, q4_ref, acc_ref, *s_refs,
                      blk, lam_init):
    qi = pl.program_id(2)
    n_groups = DIFF_GROUPS
    gs = blk // n_groups
    lane = lax.broadcasted_iota(jnp.int32, (1, LANES), 1)
    _store_q_stack(q4_ref, q_ref[0], [(lane // A_QK_DIM) == c for c in range(4)], n_groups)

    def causal(g):
        key = lax.broadcasted_iota(jnp.int32, ((g + 1) * gs, gs), 0)
        qry = lax.broadcasted_iota(jnp.int32, ((g + 1) * gs, gs), 1) + g * gs
        return jnp.where(key <= qry, 0.0, NEG_INF)

    _for_each_query_block(qi, k_ref.shape[1] // blk, lambda n: _attn_two_pass(
        q4_ref, k_ref, vt_ref, s_refs, acc_ref, n, blk, 4, n_groups, lambda kb: None, causal))

    lp = lam_ref[...]
    lam = (jnp.exp(jnp.sum(lp[0:1] * lp[1:2], axis=1, keepdims=True))
           - jnp.exp(jnp.sum(lp[2:3] * lp[3:4], axis=1, keepdims=True)) + lam_init)
    o_t = jnp.concatenate([_normalized(acc_ref, 2 * h, 4, n_groups)
                           - lam * _normalized(acc_ref, 2 * h + 1, 4, n_groups)
                           for h in range(2)], axis=0)
    o_t = _head_rms_t(o_t, gt_ref[...]) * (1.0 - lam_init)
    o_ref[0] = o_t.T.astype(o_ref.dtype)


def _attn_call(kernel_fn, name, n_comb, n_groups, consts, const_specs, q, k, vt, blk):
    b, s, w = q.shape
    pairs = w // LANES
    return pl.pallas_call(
        kernel_fn,
        out_shape=jax.ShapeDtypeStruct((b, s, w), BF16),
        grid=(b, pairs, s // blk),
        in_specs=const_specs + [
            pl.BlockSpec((1, blk, LANES), lambda bi, p, i: (bi, i, p)),
            pl.BlockSpec((1, s, LANES), lambda bi, p, i: (bi, 0, p)),
            pl.BlockSpec((LANES, s), lambda bi, p, i: (p, bi)),
        ],
        out_specs=pl.BlockSpec((1, blk, LANES), lambda bi, p, i: (bi, i, p)),
        scratch_shapes=[pltpu.VMEM((n_comb * blk, LANES), BF16),
                        pltpu.VMEM((2, V_ROWS, n_comb // 2 * blk), F32)]
        + [pltpu.VMEM((s // blk, blk, n_comb * blk // n_groups), F32) for _ in range(n_groups)],
        compiler_params=_cparams(("parallel", "parallel", "arbitrary")),
        name=name,
    )(*consts, q, k, vt)


def _diff_attn(q, k, vt, lam_p, g_t, lam_init):
    blk = g_t.shape[1]
    return _attn_call(functools.partial(_diff_attn_kernel, blk=blk, lam_init=lam_init), "diff_attn", 4, DIFF_GROUPS,
                      [lam_p, g_t], [_resident((4, LANES)), _resident(g_t.shape)], q, k, vt, blk)


def _dilated_log2_counts(blk):
    far = max(w for w, _ in DILATED_PATTERNS if w < max(p[0] for p in DILATED_PATTERNS))
    n_off = far // blk + 2
    j = np.arange(blk)[:, None]
    i = np.arange(blk)[None, :]
    cnt = np.zeros((n_off, blk, blk), np.float64)
    for o in range(n_off):
        delta = o * blk + i - j
        for window, dil in DILATED_PATTERNS:
            cnt[o] += ((delta >= 0) & (delta <= window) & (delta % dil == 0))
    return np.where(cnt > 0, np.log2(np.maximum(cnt, 1.0)), NEG_INF).astype(np.float32)


def _dil_attn_kernel(bias_ref, gt_ref, q_ref, k_ref, vt_ref, o_ref, q2_ref, acc_ref, *s_refs,
                     blk, n_off):
    qi = pl.program_id(2)
    n_groups = DIL_GROUPS
    gs = blk // n_groups
    lane = lax.broadcasted_iota(jnp.int32, (1, LANES), 1)
    low = lane < HEAD_DIM
    _store_q_stack(q2_ref, q_ref[0], [low, jnp.logical_not(low)], n_groups)

    _for_each_query_block(qi, k_ref.shape[1] // blk, lambda n: _attn_two_pass(
        q2_ref, k_ref, vt_ref, s_refs, acc_ref, n, blk, 2, n_groups,
        lambda kb: bias_ref[min(n - kb, n_off - 1)],
        lambda g: bias_ref[0, 0:(g + 1) * gs, g * gs:(g + 1) * gs]))

    o_t = jnp.concatenate([_normalized(acc_ref, c, 2, n_groups) for c in range(2)], axis=0)
    o_ref[0] = _head_rms_t(o_t, gt_ref[...]).T.astype(o_ref.dtype)


def _dil_attn(q, k, vt, bias, g_t):
    blk = bias.shape[1]
    return _attn_call(functools.partial(_dil_attn_kernel, blk=blk, n_off=bias.shape[0]), "dil_attn", 2, DIL_GROUPS,
                      [bias, g_t], [_resident(bias.shape), _resident(g_t.shape)], q, k, vt, blk)


def _solve_unit_lower(a_list, x_list):
    n = len(x_list)
    c = x_list[0].shape[0]
    nb = c // SOLVE_BLOCK
    sub = SOLVE_BLOCK // SUBLANES
    a_split = [_split2(a[:, 0:c]) for a in a_list]
    xs = [[x[SOLVE_BLOCK * i:SOLVE_BLOCK * (i + 1)] for i in range(nb)] for x in x_list]
    zeros = jnp.zeros((SOLVE_BLOCK, LANES), F32)
    for b in range(nb):
        base = b * SOLVE_BLOCK
        rows = [[xs[u][b][SUBLANES * i:SUBLANES * (i + 1)] for i in range(sub)] for u in range(n)]
        for t in range(SOLVE_BLOCK - 1):
            bi = t // SUBLANES
            for u in range(n):
                xt = rows[u][bi][t % SUBLANES:t % SUBLANES + 1, :]
                for i in range(bi, sub):
                    r0 = base + SUBLANES * i
                    a_col = a_list[u][r0:r0 + SUBLANES, base + t:base + t + 1]
                    rows[u][i] = rows[u][i] - a_col * xt
        for u in range(n):
            xs[u][b] = jnp.concatenate(rows[u], axis=0)
        if b + 1 < nb:
            below = slice(base + SOLVE_BLOCK, c)
            for u in range(n):
                a_hi, a_lo = a_split[u]
                x_hi, x_lo = _split2(jnp.concatenate(
                    [zeros] * b + [xs[u][b]] + [zeros] * (nb - 1 - b), axis=0))
                upd = _dot(a_lo[below], x_hi) + _dot(a_hi[below], x_lo) + _dot(a_hi[below], x_hi)
                for j in range(b + 1, nb):
                    xs[u][j] = xs[u][j] - upd[(j - b - 1) * SOLVE_BLOCK:(j - b) * SOLVE_BLOCK]
    return [jnp.concatenate(x, axis=0) for x in xs]


def _to_slab(col_block, half, lower):
    lane = lax.broadcasted_iota(jnp.int32, (1, LANES), 1)
    want_low = lower
    have_low = half == 0
    src = col_block if want_low == have_low else pltpu.roll(col_block, HEAD_DIM, 1)
    keep = (lane < HEAD_DIM) if want_low else (lane >= HEAD_DIM)
    return jnp.where(keep, src, 0.0)


def _rwkv_kernel(*refs, n_chunks, has_vres):
    if has_vres:
        (cc_ref, vf_ref, mu_ref, w0_ref, w2_ref, a0_ref, a2_ref, g2_ref, kk_ref, ka_ref, rk_ref,
         gng_ref, gnb_ref, v0_ref, v1_ref, v2_ref, bd_ref, tri_ref,
         o_ref, carry_ref, hs_ref) = refs
    else:
        (cc_ref, mu_ref, w0_ref, w2_ref, a0_ref, a2_ref, g2_ref, kk_ref, ka_ref, rk_ref,
         gng_ref, gnb_ref, bd_ref, tri_ref,
         o_ref, vf_out_ref, carry_ref, hs_ref) = refs
    cw = C_WIDTH
    cl = RWKV_CHUNK

    @pl.when(pl.program_id(1) == 0)
    def _():
        carry_ref[...] = jnp.zeros(carry_ref.shape, F32)
        hs_ref[...] = jnp.zeros(hs_ref.shape, F32)

    c_raw = cc_ref[0]
    tb = c_raw.shape[0]
    row = lax.broadcasted_iota(jnp.int32, (tb, 1), 0)
    prev = jnp.where(row == 0, carry_ref[...], pltpu.roll(c_raw, 1, 0))
    carry_ref[...] = c_raw[tb - 1:tb, :]
    c = c_raw + (prev - c_raw) * mu_ref[...]

    r = c[:, 0:cw]
    k = c[:, cw:2 * cw]
    v = c[:, 2 * cw:3 * cw]
    xwa = c[:, 3 * cw:3 * cw + LANES]
    xg = c[:, 3 * cw + LANES:3 * cw + 2 * LANES]
    if has_vres:
        mix = jax.nn.sigmoid(v0_ref[...] + _dot2r(_dot2r(v, v1_ref[...]), v2_ref[...]))
        v = v + (vf_ref[0] - v) * mix
    else:
        vf_out_ref[0] = v
    bd = bd_ref[...]
    lw = -DECAY_RATE * jax.nn.sigmoid(w0_ref[...] + _dot1(jnp.tanh(xwa), w2_ref[...]))
    a = jax.nn.sigmoid(a0_ref[...] + _dot1(xwa, a2_ref[...]))
    gate = _dot2r(jax.nn.sigmoid(xg), g2_ref[...])
    kkv = k * kk_ref[...]
    kappa = kkv / jnp.maximum(jnp.sqrt(_dot_sel_rhs(kkv * kkv, bd)), 1e-12)
    k2 = k * (1.0 + (a - 1.0) * ka_ref[...])
    bonus = _dot_sel_rhs(r * k2 * rk_ref[...], bd)
    beta = a * kappa

    lane = lax.broadcasted_iota(jnp.int32, (1, LANES), 1)
    low = lane < HEAD_DIM
    lane2 = lax.broadcasted_iota(jnp.int32, (1, cw), 1)
    rr = lax.broadcasted_iota(jnp.int32, (2 * cl, LANES), 0)
    cidx = lax.broadcasted_iota(jnp.int32, (2 * cl, LANES), 1) % cl
    tri_mask = cidx < jnp.where(rr < cl, rr, rr - cl + 1)
    eye = (lax.broadcasted_iota(jnp.int32, (cl, LANES), 0)
           == lax.broadcasted_iota(jnp.int32, (cl, LANES), 1))
    tri = tri_ref[...]

    zeros = jnp.zeros((cl, LANES), F32)
    units = []
    for ci in range(n_chunks):
        sl = slice(ci * cl, (ci + 1) * cl)
        lwc = lw[sl]
        cum = _dot_sel_lhs(tri, lwc)
        cum_end = cum[cl - 1:cl, :]
        rt = r[sl] * jnp.exp(cum)
        kt = kappa[sl] * jnp.exp(cum - lwc)
        e_inv = jnp.exp(-cum)
        e_end = jnp.exp(cum_end - cum)
        bh = beta[sl] * e_end
        kh = k2[sl] * e_end
        p_end = jnp.exp(cum_end)
        vc = v[sl]
        right = jnp.concatenate([beta[sl] * e_inv, k2[sl] * e_inv], axis=0).astype(BF16)
        left_hi, left_lo = _split2(jnp.concatenate([kt, rt], axis=0))
        for h in range(C_HEADS):
            cb, half = divmod(h, 2)
            cs = slice(cb * LANES, (cb + 1) * LANES)
            hm = ((lane2 // HEAD_DIM) == h).astype(BF16)
            units.append(dict(
                ci=ci, h=h, right=right, left_hi=left_hi * hm, left_lo=left_lo * hm,
                vw=_to_slab(vc[:, cs], half, lower=False),
                kw=_to_slab(kt[:, cs], half, lower=True),
                rw=_to_slab(rt[:, cs], half, lower=True),
                bkw=jnp.concatenate([_to_slab(bh[:, cs], half, True), _to_slab(kh[:, cs], half, True)], axis=0),
                pe=_to_slab(jnp.broadcast_to(p_end[:, cs], (cl, LANES)), half, True)))
    def stage_m(us):
        for u in us:
            u["m"] = jnp.where(tri_mask, _dot(u["left_lo"], u["right"], _NT)
                               + _dot(u["left_hi"], u["right"], _NT), 0.0)

    def stage_rhs(us):
        for u in us:
            u["rhs"] = u["kw"] + _dot1(u["m"][0:cl], jnp.concatenate([zeros, u["vw"]], axis=0))

    def stage_solve(us):
        xs = _solve_unit_lower([u["m"][0:cl] for u in us], [u["rhs"] for u in us])
        for u, x in zip(us, xs):
            u["st"] = jnp.concatenate([-x, u["vw"]], axis=0)

    def stage_qy(us):
        for u in us:
            u["qy"] = u["rw"] + _dot2r(u["m"][cl:2 * cl], u["st"])

    def stage_gh(us):
        for u in us:
            u["gh"] = jnp.where(eye, u["pe"], 0.0) + _dot1(u["bkw"].T[0:cl], u["st"])

    def state_step(ci):
        for u in units[ci * C_HEADS:(ci + 1) * C_HEADS]:
            hs = hs_ref[u["h"]]
            u["y"] = u["qy"] + _dot1(u["qy"], hs)
            hs_ref[u["h"], 0:cl, :] = jnp.where(low, 0.0, u["gh"] + _dot2r(u["gh"], hs))

    n_first = n_chunks // 2
    first, second = units[:n_first * C_HEADS], units[n_first * C_HEADS:]
    for stage in (stage_m, stage_rhs, stage_solve, stage_qy, stage_gh):
        if first:
            stage(first)
    pending = list(range(n_first))
    for stage in (stage_m, stage_rhs, stage_solve, stage_qy, stage_gh):
        stage(second)
        if pending:
            state_step(pending.pop(0))
    for ci in pending + list(range(n_first, n_chunks)):
        state_step(ci)
    y_chunks = []
    for ci in range(n_chunks):
        yh = [u["y"] for u in units if u["ci"] == ci]
        cols = [jnp.where(low, pltpu.roll(yh[2 * cb], HEAD_DIM, 1), yh[2 * cb + 1])
                for cb in range(C_HEADS // 2)]
        y_chunks.append(jnp.concatenate(cols, axis=1))
    y = jnp.concatenate(y_chunks, axis=0) if n_chunks > 1 else y_chunks[0]

    inv = 1.0 / HEAD_DIM
    mean = _dot_sel_rhs(y, bd) * inv
    d = y - mean
    var = _dot_sel_rhs(d * d, bd) * inv
    yn = d * lax.rsqrt(var + C_GN_EPS) * gng_ref[...] + gnb_ref[...]
    out = (yn + bonus * v) * gate
    o_ref[0] = out.astype(o_ref.dtype)


def _rwkv(cc, v_first, p, consts):
    b, s, _ = cc.shape
    n_chunks = RWKV_CHUNKS_PER_STEP
    tb = n_chunks * RWKV_CHUNK
    has_vres = v_first is not None
    blk_cc = pl.BlockSpec((1, tb, C_COLS), lambda bi, j: (bi, j, 0))
    blk_cw = pl.BlockSpec((1, tb, C_WIDTH), lambda bi, j: (bi, j, 0))
    vec = lambda n: _resident((1, n))
    params = [p["mu"], p["w0"], p["w2"], p["a0"], p["a2"], p["g2"], p["k_k"], p["k_a"], p["r_k"],
              p["gn_g"], p["gn_b"]]
    specs = [vec(C_COLS), vec(C_WIDTH), _resident((LANES, C_WIDTH)), vec(C_WIDTH),
             _resident((LANES, C_WIDTH)), _resident((LANES, C_WIDTH)), vec(C_WIDTH), vec(C_WIDTH),
             vec(C_WIDTH), vec(C_WIDTH), vec(C_WIDTH)]
    if has_vres:
        params += [p["v0"], p["v1"], p["v2"]]
        specs += [vec(C_WIDTH), _resident((C_WIDTH, LANES)), _resident((LANES, C_WIDTH))]
    params += [consts["bd"], consts["tri"]]
    specs += [_resident((C_WIDTH, C_WIDTH)), _resident((RWKV_CHUNK, RWKV_CHUNK))]
    o_shape = jax.ShapeDtypeStruct((b, s, C_WIDTH), BF16)
    if has_vres:
        args = [cc, v_first] + params
        in_specs = [blk_cc, blk_cw] + specs
        out_shape, out_specs = o_shape, blk_cw
    else:
        args = [cc] + params
        in_specs = [blk_cc] + specs
        out_shape = (o_shape, jax.ShapeDtypeStruct((b, s, C_WIDTH), F32))
        out_specs = (blk_cw, blk_cw)
    return pl.pallas_call(
        functools.partial(_rwkv_kernel, n_chunks=n_chunks, has_vres=has_vres),
        out_shape=out_shape,
        grid=(b, s // tb),
        in_specs=in_specs,
        out_specs=out_specs,
        scratch_shapes=[pltpu.VMEM((1, C_COLS), F32),
                        pltpu.VMEM((C_HEADS, 2 * RWKV_CHUNK, LANES), F32)],
        compiler_params=_cparams(("parallel", "arbitrary")),
        name="rwkv_vres" if has_vres else "rwkv_first",
    )(*args)


def _out_proj_ffn_kernel(x_ref, oa_ref, ob_ref, oc_ref, w_ref, g1_ref, b1_ref,
                         wg_ref, wu_ref, wd_ref, g2_ref, b2_ref, o_ref, *, alpha):
    mix = _dot(jnp.concatenate([oa_ref[...], ob_ref[...], oc_ref[...]], axis=1), w_ref[...])
    h = _layer_norm(alpha * x_ref[...] + mix, g1_ref[...], b1_ref[...])
    o_ref[...] = _ffn_ln_rows(h, wg_ref, wu_ref, wd_ref, g2_ref[...], b2_ref[...], alpha)


def _out_proj_ffn(x, oa, ob, oc, w_out, g1, b1, ffn_w, g2, b2, alpha, layer):
    t, d = x.shape
    d_ff = ffn_w[0].shape[2]
    tm = min(ROW_TILE, t)
    row = lambda i: (i, 0)
    return pl.pallas_call(
        functools.partial(_out_proj_ffn_kernel, alpha=alpha),
        out_shape=jax.ShapeDtypeStruct((t, d), F32),
        grid=(t // tm,),
        in_specs=[pl.BlockSpec((tm, d), row), pl.BlockSpec((tm, A_WIDTH), row),
                  pl.BlockSpec((tm, B_WIDTH), row), pl.BlockSpec((tm, C_WIDTH), row),
                  _resident_layer(w_out.shape[1:], layer), _resident((1, d)), _resident((1, d))]
        + _ffn_specs(d, d_ff, layer),
        out_specs=pl.BlockSpec((tm, d), row),
        compiler_params=_cparams(("parallel",)),
        name="out_proj_ffn",
    )(x, oa, ob, oc, w_out, g1, b1, *ffn_w, g2, b2)


def _pad_rows(w, total, offset):
    return jnp.zeros((total, w.shape[1]), w.dtype).at[offset:offset + w.shape[0]].set(w)


def kernel(x, ffn_a_gate, ffn_a_up, ffn_a_down, ffn_b_gate, ffn_b_up, ffn_b_down, ln_g, ln_b, w_in, w_out, a_lam_q1, a_lam_k1, a_lam_q2, a_lam_k2, a_norm_g, b_norm_g, c_mu, c_w0, c_w2, c_a0, c_a2, c_g2, c_k_k, c_k_a, c_r_k, c_gn_g, c_gn_b, c_v0, c_v1, c_v2):
    bsz, s_len, d_model = x.shape
    depth = w_in.shape[0]
    alpha = (2 * depth) ** 0.25
    t = bsz * s_len

    tabs = _rope_tables(s_len, A_QK_DIM) + _rope_tables(s_len, HEAD_DIM)
    dil_bias = jnp.asarray(_dilated_log2_counts(min(ATT_BLOCK, s_len)))
    head_of = np.arange(C_WIDTH) // HEAD_DIM
    consts = {
        "bd": jnp.asarray(head_of[:, None] == head_of[None, :], BF16),
        "tri": jnp.asarray(np.tril(np.ones((RWKV_CHUNK, RWKV_CHUNK), np.float32)), BF16),
    }
    row = lambda vct: vct.reshape(1, -1)
    att_blk = min(ATT_BLOCK, s_len)
    gain_t = lambda vct: jnp.broadcast_to(jnp.tile(vct, LANES // HEAD_DIM)[:, None], (LANES, att_blk))

    ffn_a = (ffn_a_gate.astype(BF16), ffn_a_up.astype(BF16), ffn_a_down.astype(BF16))
    ffn_b = (ffn_b_gate.astype(BF16), ffn_b_up.astype(BF16), ffn_b_down.astype(BF16))
    w_in_b = w_in.astype(BF16)
    v_cols = np.r_[2 * A_WIDTH:3 * A_WIDTH, 3 * A_WIDTH + 2 * B_WIDTH:3 * A_WIDTH + 3 * B_WIDTH]
    w_vt_b = jnp.swapaxes(w_in_b[:, :, v_cols], 1, 2)
    w_out_b = w_out.astype(BF16)

    h = x.reshape(t, d_model)
    v_first = None
    for l in range(depth):
        h, aq, ak, avt, bq, bk, bvt, cc = _ffn_in_proj(h, ffn_a, row(ln_g[l, 0]), row(ln_b[l, 0]),
                                                       w_in_b, w_vt_b, tabs, s_len, alpha, l)
        shp = (bsz, s_len, A_WIDTH)

        lam_init = 0.8 - 0.6 * math.exp(-0.3 * l)
        lam_p = jnp.zeros((4, LANES), F32).at[:, :A_QK_DIM].set(
            jnp.stack([a_lam_q1[l], a_lam_k1[l], a_lam_q2[l], a_lam_k2[l]]))
        o_a = _diff_attn(aq.reshape(shp), ak.reshape(shp), avt, lam_p, gain_t(a_norm_g[l]), lam_init)
        o_b = _dil_attn(bq.reshape(shp), bk.reshape(shp), bvt, dil_bias, gain_t(b_norm_g[l]))

        p = {"mu": row(c_mu[l]), "w0": row(c_w0[l]), "a0": row(c_a0[l]),
             "w2": _pad_rows(c_w2[l], LANES, 0), "a2": _pad_rows(c_a2[l], LANES, C_W_RANK),
             "g2": c_g2[l], "k_k": row(c_k_k[l]), "k_a": row(c_k_a[l]), "r_k": row(c_r_k[l]),
             "gn_g": row(c_gn_g[l]), "gn_b": row(c_gn_b[l])}
        cc3 = cc.reshape(bsz, s_len, C_COLS)
        if l == 0:
            o_c, v_first = _rwkv(cc3, None, p, consts)
        else:
            p["v0"] = row(c_v0[l - 1])
            p["v1"] = jnp.zeros((C_WIDTH, LANES), F32).at[:, :C_V_RANK].set(c_v1[l - 1])
            p["v2"] = _pad_rows(c_v2[l - 1], LANES, 0)
            o_c = _rwkv(cc3, v_first, p, consts)

        h = _out_proj_ffn(h, o_a.reshape(t, A_WIDTH), o_b.reshape(t, B_WIDTH), o_c.reshape(t, C_WIDTH),
                          w_out_b, row(ln_g[l, 1]), row(ln_b[l, 1]),
                          ffn_b, row(ln_g[l, 2]), row(ln_b[l, 2]), alpha, l)
    return h.reshape(bsz, s_len, d_model)
```

```python
import functools
import math

import numpy as np
import jax
import jax.numpy as jnp
from jax import lax
from jax.experimental import pallas as pl
from jax.experimental.pallas import tpu as pltpu

F32 = jnp.float32
BF16 = jnp.bfloat16

HEAD_DIM = 64
A_HEADS = 6
B_HEADS = 6
C_HEADS = 4
A_QK_DIM = 32
A_WIDTH = A_HEADS * HEAD_DIM
B_WIDTH = B_HEADS * HEAD_DIM
C_WIDTH = C_HEADS * HEAD_DIM
C_W_RANK = 64
C_A_RANK = 64
C_V_RANK = 32
C_G_RANK = 128
C_COLS = 3 * C_WIDTH + C_W_RANK + C_A_RANK + C_G_RANK
DILATED_PATTERNS = ((128, 1), (512, 4), (2048, 16))
ROPE_THETA = 10000.0
LN_EPS = 1e-5
RMS_EPS = 1e-5
C_GN_EPS = 64e-5
NEG_INF = -1e30
DECAY_RATE = math.exp(-0.5)
LOG2_E = math.log2(math.e)

LANES = 128
SUBLANES = 8
BF16_SUBLANES = 16
VMEM_LIMIT_BYTES = 56 * 1024 * 1024

ROW_TILE = 512
FF_CHUNK = 256
ATT_BLOCK = 512
RWKV_CHUNK = 64
RWKV_CHUNKS_PER_STEP = 8
SOLVE_BLOCK = 16

_NN = (((1,), (0,)), ((), ()))
_NT = (((1,), (1,)), ((), ()))


def _dot(a, b, dims=_NN):
    return lax.dot_general(a, b, dims, preferred_element_type=F32)


def _split2(x):
    hi = x.astype(BF16)
    lo = (x - hi.astype(F32)).astype(BF16)
    return hi, lo


def _split3(x):
    hi = x.astype(BF16)
    r = x - hi.astype(F32)
    mid = r.astype(BF16)
    lo = (r - mid.astype(F32)).astype(BF16)
    return hi, mid, lo


def _dot2r(a, b, dims=_NN):
    a_hi = a.astype(BF16)
    b_hi, b_lo = _split2(b)
    return _dot(a_hi, b_lo, dims) + _dot(a_hi, b_hi, dims)


def _dot1(a, b, dims=_NN):
    return _dot(a.astype(BF16), b.astype(BF16), dims)


def _dot_sel_lhs(sel, x, dims=_NN):
    h, m, l = _split3(x)
    return _dot(sel, l, dims) + _dot(sel, m, dims) + _dot(sel, h, dims)


def _dot_sel_rhs(x, sel):
    h, l = _split2(x)
    return _dot(l, sel) + _dot(h, sel)


def _layer_norm(y, g, b):
    mu = jnp.mean(y, axis=-1, keepdims=True)
    d = y - mu
    var = jnp.mean(d * d, axis=-1, keepdims=True)
    return d * lax.rsqrt(var + LN_EPS) * g + b


def _cparams(sem):
    return pltpu.CompilerParams(dimension_semantics=sem, vmem_limit_bytes=VMEM_LIMIT_BYTES)


def _resident(shape):
    nd = len(shape)
    return pl.BlockSpec(shape, lambda *_: (0,) * nd, pipeline_mode=pl.Buffered(1))


def _resident_layer(shape, layer):
    nd = len(shape)
    return pl.BlockSpec((None,) + tuple(shape), lambda *_: (layer,) + (0,) * nd,
                        pipeline_mode=pl.Buffered(1))


def _ffn_ln_rows(x, wg_ref, wu_ref, wd_ref, g, b, alpha):
    d_ff = wg_ref.shape[1]
    xb = x.astype(BF16)
    acc = jnp.zeros(x.shape, F32)
    for c0 in range(0, d_ff, FF_CHUNK):
        gate = _dot(xb, wg_ref[:, c0:c0 + FF_CHUNK])
        up = _dot(xb, wu_ref[:, c0:c0 + FF_CHUNK])
        h = (gate * jax.nn.sigmoid(gate)) * up
        acc = acc + _dot(h.astype(BF16), wd_ref[c0:c0 + FF_CHUNK, :])
    return _layer_norm(alpha * x + 0.5 * acc, g, b)


def _ffn_specs(d, d_ff, layer):
    return [_resident_layer((d, d_ff), layer), _resident_layer((d, d_ff), layer),
            _resident_layer((d_ff, d), layer), _resident((1, d)), _resident((1, d))]


def _rope_lanes(y, cos, sin_signed, group):
    half = group // 2
    lane = lax.broadcasted_iota(jnp.int32, (1, LANES), 1)
    first = (lane % group) < half
    swapped = jnp.where(first, pltpu.roll(y, LANES - half, 1), pltpu.roll(y, half, 1))
    return y * cos + swapped * sin_signed


def _in_proj_rows(x, w_ref, wvt_ref, ca_ref, sa_ref, cb_ref, sb_ref,
                  aq_ref, ak_ref, avt_ref, bq_ref, bk_ref, bvt_ref, cc_ref):
    xb = x.astype(BF16)
    ca, sa, cb, sb = ca_ref[...], sa_ref[...], cb_ref[...], sb_ref[...]
    a_scale = A_QK_DIM ** -0.5 * LOG2_E
    b_scale = HEAD_DIM ** -0.5 * LOG2_E
    b_col = 3 * A_WIDTH
    plan = ((aq_ref, 0, (ca, sa, A_QK_DIM), a_scale),
            (ak_ref, A_WIDTH, (ca, sa, A_QK_DIM), None),
            (bq_ref, b_col, (cb, sb, HEAD_DIM), b_scale),
            (bk_ref, b_col + B_WIDTH, (cb, sb, HEAD_DIM), None))
    segs = [(ref, s0, col + s0, rope, scale) for ref, col, rope, scale in plan
            for s0 in range(0, A_WIDTH, LANES)]
    wide = 2 * LANES
    for i in range(0, len(segs), wide // LANES):
        c0 = segs[i][2]
        y = _dot(xb, w_ref[:, c0:c0 + wide])
        for j in range(wide // LANES):
            ref, s0, _, rope, scale = segs[i + j]
            z = _rope_lanes(y[:, j * LANES:(j + 1) * LANES], *rope)
            if scale is not None:
                z = z * scale
            ref[:, s0:s0 + LANES] = z.astype(ref.dtype)
    col = b_col + 3 * B_WIDTH
    for s0 in range(0, C_COLS, wide):
        cc_ref[:, s0:s0 + wide] = _dot(xb, w_ref[:, col + s0:col + s0 + wide])
    avt_ref[...] = _dot(wvt_ref[0:A_WIDTH, :], xb, _NT).astype(avt_ref.dtype)
    bvt_ref[...] = _dot(wvt_ref[A_WIDTH:A_WIDTH + B_WIDTH, :], xb, _NT).astype(bvt_ref.dtype)


def _ffn_in_proj_kernel(x_ref, wg_ref, wu_ref, wd_ref, g_ref, b_ref, w_ref, wvt_ref,
                        ca_ref, sa_ref, cb_ref, sb_ref, h_ref, *out_refs, alpha):
    h = _ffn_ln_rows(x_ref[...], wg_ref, wu_ref, wd_ref, g_ref[...], b_ref[...], alpha)
    h_ref[...] = h
    _in_proj_rows(h, w_ref, wvt_ref, ca_ref, sa_ref, cb_ref, sb_ref, *out_refs)


def _ffn_in_proj(x, ffn_w, g, b, w_in, w_vt, tabs, s_len, alpha, layer):
    t, d = x.shape
    d_ff = ffn_w[0].shape[2]
    tm = min(ROW_TILE, s_len)
    spb = s_len // tm
    row = lambda i: (i, 0)
    tab = pl.BlockSpec((tm, LANES), lambda i: (i % spb, 0))
    qk = jax.ShapeDtypeStruct((t, A_WIDTH), BF16)
    vt = jax.ShapeDtypeStruct((A_WIDTH, t), BF16)
    qk_spec = pl.BlockSpec((tm, A_WIDTH), row)
    vt_spec = pl.BlockSpec((A_WIDTH, tm), lambda i: (0, i))
    return pl.pallas_call(
        functools.partial(_ffn_in_proj_kernel, alpha=alpha),
        out_shape=(jax.ShapeDtypeStruct((t, d), F32), qk, qk, vt, qk, qk, vt,
                   jax.ShapeDtypeStruct((t, C_COLS), F32)),
        grid=(t // tm,),
        in_specs=[pl.BlockSpec((tm, d), row)] + _ffn_specs(d, d_ff, layer)
        + [_resident_layer(w_in.shape[1:], layer), _resident_layer(w_vt.shape[1:], layer), tab, tab, tab, tab],
        out_specs=(pl.BlockSpec((tm, d), row), qk_spec, qk_spec, vt_spec, qk_spec, qk_spec, vt_spec,
                   pl.BlockSpec((tm, C_COLS), row)),
        compiler_params=_cparams(("parallel",)),
        name="ffn_in_proj",
    )(x, *ffn_w, g, b, w_in, w_vt, *tabs)


def _rope_tables(s_len, group):
    half = group // 2
    inv = ROPE_THETA ** (-jnp.arange(0, group, 2, dtype=F32) / group)
    ang = jnp.arange(s_len, dtype=F32)[:, None] * inv[None, :]
    lane = np.arange(LANES)
    idx = lane % half
    sign = np.where((lane % group) < half, -1.0, 1.0).astype(np.float32)
    cos = jnp.cos(ang)[:, idx]
    sin = jnp.sin(ang)[:, idx] * sign[None, :]
    return cos, sin


V_ROWS = HEAD_DIM + BF16_SUBLANES
DIFF_GROUPS = 4
DIL_GROUPS = 2


def _attn_two_pass(q_stack_ref, k_ref, vt_ref, s_refs, acc_ref, qi, blk, n_comb, n_groups,
                   off_bias, diag_bias):
    gs = blk // n_groups
    per_head = n_comb // 2
    gcols = n_comb * gs

    def cols(g):
        return slice(g * gcols, (g + 1) * gcols)

    def tiled(bias):
        return jnp.concatenate([bias] * n_comb, axis=1)

    def k_at(kb, size):
        return k_ref[0, kb * blk:kb * blk + size, :]

    vx_cache = {}

    def vx_at(h, kb, size):
        if (h, kb, size) not in vx_cache:
            v = vt_ref[h * HEAD_DIM:(h + 1) * HEAD_DIM, kb * blk:kb * blk + size]
            vx_cache[h, kb, size] = jnp.concatenate(
                [v, jnp.ones((V_ROWS - HEAD_DIM, size), v.dtype)], axis=0)
        return vx_cache[h, kb, size]

    def col_max(m, s):
        part = jnp.max(s.reshape(s.shape[0] // SUBLANES, SUBLANES, s.shape[1]), axis=0)
        return part if m is None else jnp.maximum(m, part)

    mx = [None] * n_groups
    m = [None] * n_groups
    acc = [[None, None] for _ in range(n_groups)]

    def pass_a(g, kb):
        s = _dot(k_at(kb, blk), q_stack_ref[:, cols(g)])
        bias = off_bias(kb)
        if bias is not None:
            s = s + tiled(bias[:, g * gs:(g + 1) * gs])
        s_refs[g][kb] = s
        mx[g] = col_max(mx[g], s)

    def pass_a_diag(g):
        nk = (g + 1) * gs
        s = _dot(k_at(qi, nk), q_stack_ref[:, cols(g)])
        bias = diag_bias(g)
        if bias is not None:
            s = s + tiled(bias)
        s_refs[g][qi, 0:nk, :] = s
        m[g] = jnp.max(col_max(mx[g], s), axis=0, keepdims=True)

    def keys_of(g, kb):
        return blk if kb < qi else (g + 1) * gs

    def probs(g, kb):
        return jnp.exp2(s_refs[g][kb, 0:keys_of(g, kb), :] - m[g]).astype(BF16)

    def pass_b(g, kb, p):
        for h in range(2):
            d = _dot(vx_at(h, kb, keys_of(g, kb)), p[:, h * per_head * gs:(h + 1) * per_head * gs])
            acc[g][h] = d if acc[g][h] is None else acc[g][h] + d

    for kb in range(qi):
        pass_a(0, kb)
    pass_a_diag(0)
    for g in range(1, n_groups + 1):
        p = probs(g - 1, 0)
        for kb in range(qi + 1):
            if g < n_groups and kb < qi:
                pass_a(g, kb)
            elif g < n_groups:
                pass_a_diag(g)
            p_next = probs(g - 1, kb + 1) if kb < qi else None
            pass_b(g - 1, kb, p)
            p = p_next
    for h in range(2):
        acc_ref[h] = jnp.concatenate([acc[g][h] for g in range(n_groups)], axis=1)


def _for_each_query_block(qi, n_q, fn):
    for n in range(n_q):
        @pl.when(qi == n)
        def _(n=n):
            fn(n)


def _normalized(acc_ref, c, n_comb, n_groups):
    per_head = n_comb // 2
    h, j = divmod(c, per_head)
    blk = acc_ref.shape[2] // per_head
    gs = blk // n_groups
    acc = acc_ref[h]
    acc = jnp.concatenate([acc[:, (g * per_head + j) * gs:(g * per_head + j + 1) * gs]
                           for g in range(n_groups)], axis=1)
    return acc[0:HEAD_DIM] / acc[HEAD_DIM:HEAD_DIM + 1]


def _store_q_stack(q_stack_ref, q, n_comb, n_groups):
    gs = q.shape[0] // n_groups
    q_t = q.astype(F32).T.astype(q.dtype)
    zero = jnp.zeros_like(q_t)
    chan = lax.broadcasted_iota(jnp.int32, (LANES, 1), 0) // (LANES // n_comb)
    for c in range(n_comb):
        masked = jnp.where(chan == c, q_t, zero)
        for g in range(n_groups):
            c0 = (g * n_comb + c) * gs
            q_stack_ref[:, c0:c0 + gs] = masked[:, g * gs:(g + 1) * gs]


def _head_rms_t(o_t, g_t):
    sq = o_t * o_t
    ms = [jnp.sum(sq[h * HEAD_DIM:(h + 1) * HEAD_DIM], axis=0, keepdims=True) * (1.0 / HEAD_DIM)
          for h in range(2)]
    scale = jnp.concatenate([jnp.broadcast_to(lax.rsqrt(m + RMS_EPS), (HEAD_DIM, o_t.shape[1])) for m in ms],
                            axis=0)
    return o_t * scale * g_t


def _diff_attn_kernel(lam_ref, gt_ref, q_ref, k_ref, vt_ref, o_ref, q4_ref, acc_ref, *s_refs,
                      blk, lam_init):
    qi = pl.program_id(2)
    n_groups = DIFF_GROUPS
    gs = blk // n_groups
    _store_q_stack(q4_ref, q_ref[0], 4, n_groups)

    def causal(g):
        key = lax.broadcasted_iota(jnp.int32, ((g + 1) * gs, gs), 0)
        qry = lax.broadcasted_iota(jnp.int32, ((g + 1) * gs, gs), 1) + g * gs
        return jnp.where(key <= qry, 0.0, NEG_INF)

    _for_each_query_block(qi, k_ref.shape[1] // blk, lambda n: _attn_two_pass(
        q4_ref, k_ref, vt_ref, s_refs, acc_ref, n, blk, 4, n_groups, lambda kb: None, causal))

    lp = lam_ref[...]
    lam = (jnp.exp(jnp.sum(lp[0:1] * lp[1:2], axis=1, keepdims=True))
           - jnp.exp(jnp.sum(lp[2:3] * lp[3:4], axis=1, keepdims=True)) + lam_init)
    o_t = jnp.concatenate([_normalized(acc_ref, 2 * h, 4, n_groups)
                           - lam * _normalized(acc_ref, 2 * h + 1, 4, n_groups)
                           for h in range(2)], axis=0)
    o_t = _head_rms_t(o_t, gt_ref[...]) * (1.0 - lam_init)
    o_ref[0] = o_t.T.astype(o_ref.dtype)


def _attn_call(kernel_fn, name, n_comb, n_groups, consts, const_specs, q, k, vt, blk):
    b, s, w = q.shape
    pairs = w // LANES
    return pl.pallas_call(
        kernel_fn,
        out_shape=jax.ShapeDtypeStruct((b, s, w), BF16),
        grid=(b, pairs, s // blk),
        in_specs=const_specs + [
            pl.BlockSpec((1, blk, LANES), lambda bi, p, i: (bi, i, p)),
            pl.BlockSpec((1, s, LANES), lambda bi, p, i: (bi, 0, p)),
            pl.BlockSpec((LANES, s), lambda bi, p, i: (p, bi)),
        ],
        out_specs=pl.BlockSpec((1, blk, LANES), lambda bi, p, i: (bi, i, p)),
        scratch_shapes=[pltpu.VMEM((LANES, n_comb * blk), BF16),
                        pltpu.VMEM((2, V_ROWS, n_comb // 2 * blk), F32)]
        + [pltpu.VMEM((s // blk, blk, n_comb * blk // n_groups), F32) for _ in range(n_groups)],
        compiler_params=_cparams(("parallel", "parallel", "arbitrary")),
        name=name,
    )(*consts, q, k, vt)


def _diff_attn(q, k, vt, lam_p, g_t, lam_init):
    blk = g_t.shape[1]
    return _attn_call(functools.partial(_diff_attn_kernel, blk=blk, lam_init=lam_init), "diff_attn", 4, DIFF_GROUPS,
                      [lam_p, g_t], [_resident((4, LANES)), _resident(g_t.shape)], q, k, vt, blk)


def _dilated_log2_counts(blk):
    far = max(w for w, _ in DILATED_PATTERNS if w < max(p[0] for p in DILATED_PATTERNS))
    n_off = far // blk + 2
    j = np.arange(blk)[:, None]
    i = np.arange(blk)[None, :]
    cnt = np.zeros((n_off, blk, blk), np.float64)
    for o in range(n_off):
        delta = o * blk + i - j
        for window, dil in DILATED_PATTERNS:
            cnt[o] += ((delta >= 0) & (delta <= window) & (delta % dil == 0))
    return np.where(cnt > 0, np.log2(np.maximum(cnt, 1.0)), NEG_INF).astype(np.float32)


def _dil_attn_kernel(bias_ref, gt_ref, q_ref, k_ref, vt_ref, o_ref, q2_ref, acc_ref, *s_refs,
                     blk, n_off):
    qi = pl.program_id(2)
    n_groups = DIL_GROUPS
    gs = blk // n_groups
    _store_q_stack(q2_ref, q_ref[0], 2, n_groups)

    _for_each_query_block(qi, k_ref.shape[1] // blk, lambda n: _attn_two_pass(
        q2_ref, k_ref, vt_ref, s_refs, acc_ref, n, blk, 2, n_groups,
        lambda kb: bias_ref[min(n - kb, n_off - 1)],
        lambda g: bias_ref[0, 0:(g + 1) * gs, g * gs:(g + 1) * gs]))

    o_t = jnp.concatenate([_normalized(acc_ref, c, 2, n_groups) for c in range(2)], axis=0)
    o_ref[0] = _head_rms_t(o_t, gt_ref[...]).T.astype(o_ref.dtype)


def _dil_attn(q, k, vt, bias, g_t):
    blk = bias.shape[1]
    return _attn_call(functools.partial(_dil_attn_kernel, blk=blk, n_off=bias.shape[0]), "dil_attn", 2, DIL_GROUPS,
                      [bias, g_t], [_resident(bias.shape), _resident(g_t.shape)], q, k, vt, blk)


def _solve_unit_lower(a_list, x_list):
    n = len(x_list)
    c = x_list[0].shape[0]
    nb = c // SOLVE_BLOCK
    sub = SOLVE_BLOCK // SUBLANES
    a_split = [_split2(a[:, 0:c]) for a in a_list]
    xs = [[x[SOLVE_BLOCK * i:SOLVE_BLOCK * (i + 1)] for i in range(nb)] for x in x_list]
    zeros = jnp.zeros((SOLVE_BLOCK, LANES), F32)
    for b in range(nb):
        base = b * SOLVE_BLOCK
        rows = [[xs[u][b][SUBLANES * i:SUBLANES * (i + 1)] for i in range(sub)] for u in range(n)]
        for t in range(SOLVE_BLOCK - 1):
            bi = t // SUBLANES
            for u in range(n):
                xt = rows[u][bi][t % SUBLANES:t % SUBLANES + 1, :]
                for i in range(bi, sub):
                    r0 = base + SUBLANES * i
                    a_col = a_list[u][r0:r0 + SUBLANES, base + t:base + t + 1]
                    rows[u][i] = rows[u][i] - a_col * xt
        for u in range(n):
            xs[u][b] = jnp.concatenate(rows[u], axis=0)
        if b + 1 < nb:
            below = slice(base + SOLVE_BLOCK, c)
            for u in range(n):
                a_hi, a_lo = a_split[u]
                x_hi, x_lo = _split2(jnp.concatenate(
                    [zeros] * b + [xs[u][b]] + [zeros] * (nb - 1 - b), axis=0))
                upd = _dot(a_lo[below], x_hi) + _dot(a_hi[below], x_lo) + _dot(a_hi[below], x_hi)
                for j in range(b + 1, nb):
                    xs[u][j] = xs[u][j] - upd[(j - b - 1) * SOLVE_BLOCK:(j - b) * SOLVE_BLOCK]
    return [jnp.concatenate(x, axis=0) for x in xs]


def _to_slab(col_block, half, lower):
    lane = lax.broadcasted_iota(jnp.int32, (1, LANES), 1)
    want_low = lower
    have_low = half == 0
    src = col_block if want_low == have_low else pltpu.roll(col_block, HEAD_DIM, 1)
    keep = (lane < HEAD_DIM) if want_low else (lane >= HEAD_DIM)
    return jnp.where(keep, src, 0.0)


def _rwkv_kernel(*refs, n_chunks, has_vres):
    if has_vres:
        (cc_ref, vf_ref, mu_ref, w0_ref, w2_ref, a0_ref, a2_ref, g2_ref, kk_ref, ka_ref, rk_ref,
         gng_ref, gnb_ref, v0_ref, v1_ref, v2_ref, bd_ref, tri_ref,
         o_ref, carry_ref, hs_ref) = refs
    else:
        (cc_ref, mu_ref, w0_ref, w2_ref, a0_ref, a2_ref, g2_ref, kk_ref, ka_ref, rk_ref,
         gng_ref, gnb_ref, bd_ref, tri_ref,
         o_ref, vf_out_ref, carry_ref, hs_ref) = refs
    cw = C_WIDTH
    cl = RWKV_CHUNK

    @pl.when(pl.program_id(1) == 0)
    def _():
        carry_ref[...] = jnp.zeros(carry_ref.shape, F32)
        hs_ref[...] = jnp.zeros(hs_ref.shape, F32)

    c_raw = cc_ref[0]
    tb = c_raw.shape[0]
    row = lax.broadcasted_iota(jnp.int32, (tb, 1), 0)
    prev = jnp.where(row == 0, carry_ref[...], pltpu.roll(c_raw, 1, 0))
    carry_ref[...] = c_raw[tb - 1:tb, :]
    c = c_raw + (prev - c_raw) * mu_ref[...]

    r = c[:, 0:cw]
    k = c[:, cw:2 * cw]
    v = c[:, 2 * cw:3 * cw]
    xwa = c[:, 3 * cw:3 * cw + LANES]
    xg = c[:, 3 * cw + LANES:3 * cw + 2 * LANES]
    if has_vres:
        mix = jax.nn.sigmoid(v0_ref[...] + _dot2r(_dot2r(v, v1_ref[...]), v2_ref[...]))
        v = v + (vf_ref[0] - v) * mix
    else:
        vf_out_ref[0] = v
    bd = bd_ref[...]
    lw = -DECAY_RATE * jax.nn.sigmoid(w0_ref[...] + _dot1(jnp.tanh(xwa), w2_ref[...]))
    a = jax.nn.sigmoid(a0_ref[...] + _dot1(xwa, a2_ref[...]))
    gate = _dot2r(jax.nn.sigmoid(xg), g2_ref[...])
    kkv = k * kk_ref[...]
    kappa = kkv / jnp.maximum(jnp.sqrt(_dot_sel_rhs(kkv * kkv, bd)), 1e-12)
    k2 = k * (1.0 + (a - 1.0) * ka_ref[...])
    bonus = _dot_sel_rhs(r * k2 * rk_ref[...], bd)
    beta = a * kappa

    lane = lax.broadcasted_iota(jnp.int32, (1, LANES), 1)
    low = lane < HEAD_DIM
    lane2 = lax.broadcasted_iota(jnp.int32, (1, cw), 1)
    rr = lax.broadcasted_iota(jnp.int32, (2 * cl, LANES), 0)
    cidx = lax.broadcasted_iota(jnp.int32, (2 * cl, LANES), 1) % cl
    tri_mask = cidx < jnp.where(rr < cl, rr, rr - cl + 1)
    eye = (lax.broadcasted_iota(jnp.int32, (cl, LANES), 0)
           == lax.broadcasted_iota(jnp.int32, (cl, LANES), 1))
    tri = tri_ref[...]

    zeros = jnp.zeros((cl, LANES), F32)
    units = []
    for ci in range(n_chunks):
        sl = slice(ci * cl, (ci + 1) * cl)
        lwc = lw[sl]
        cum = _dot_sel_lhs(tri, lwc)
        cum_end = cum[cl - 1:cl, :]
        rt = r[sl] * jnp.exp(cum)
        kt = kappa[sl] * jnp.exp(cum - lwc)
        e_inv = jnp.exp(-cum)
        e_end = jnp.exp(cum_end - cum)
        bh = beta[sl] * e_end
        kh = k2[sl] * e_end
        p_end = jnp.exp(cum_end)
        vc = v[sl]
        right = jnp.concatenate([beta[sl] * e_inv, k2[sl] * e_inv], axis=0).astype(BF16)
        left_hi, left_lo = _split2(jnp.concatenate([kt, rt], axis=0))
        for h in range(C_HEADS):
            cb, half = divmod(h, 2)
            cs = slice(cb * LANES, (cb + 1) * LANES)
            hm = ((lane2 // HEAD_DIM) == h).astype(BF16)
            units.append(dict(
                ci=ci, h=h, right=right, left_hi=left_hi * hm, left_lo=left_lo * hm,
                vw=_to_slab(vc[:, cs], half, lower=False),
                kw=_to_slab(kt[:, cs], half, lower=True),
                rw=_to_slab(rt[:, cs], half, lower=True),
                bkw=jnp.concatenate([_to_slab(bh[:, cs], half, True), _to_slab(kh[:, cs], half, True)], axis=0),
                pe=_to_slab(jnp.broadcast_to(p_end[:, cs], (cl, LANES)), half, True)))
    def stage_m(us):
        for u in us:
            u["m"] = jnp.where(tri_mask, _dot(u["left_lo"], u["right"], _NT)
                               + _dot(u["left_hi"], u["right"], _NT), 0.0)

    def stage_rhs(us):
        for u in us:
            u["rhs"] = u["kw"] + _dot1(u["m"][0:cl], jnp.concatenate([zeros, u["vw"]], axis=0))

    def stage_solve(us):
        xs = _solve_unit_lower([u["m"][0:cl] for u in us], [u["rhs"] for u in us])
        for u, x in zip(us, xs):
            u["st"] = jnp.concatenate([-x, u["vw"]], axis=0)

    def stage_qy(us):
        for u in us:
            u["qy"] = u["rw"] + _dot2r(u["m"][cl:2 * cl], u["st"])

    def stage_gh(us):
        for u in us:
            u["gh"] = jnp.where(eye, u["pe"], 0.0) + _dot1(u["bkw"].T[0:cl], u["st"])

    def state_step(ci):
        for u in units[ci * C_HEADS:(ci + 1) * C_HEADS]:
            hs = hs_ref[u["h"]]
            u["y"] = u["qy"] + _dot1(u["qy"], hs)
            hs_ref[u["h"], 0:cl, :] = jnp.where(low, 0.0, u["gh"] + _dot2r(u["gh"], hs))

    n_first = n_chunks // 2
    first, second = units[:n_first * C_HEADS], units[n_first * C_HEADS:]
    for stage in (stage_m, stage_rhs, stage_solve, stage_qy, stage_gh):
        if first:
            stage(first)
    pending = list(range(n_first))
    for stage in (stage_m, stage_rhs, stage_solve, stage_qy, stage_gh):
        stage(second)
        if pending:
            state_step(pending.pop(0))
    for ci in pending + list(range(n_first, n_chunks)):
        state_step(ci)
    y_chunks = []
    for ci in range(n_chunks):
        yh = [u["y"] for u in units if u["ci"] == ci]
        cols = [jnp.where(low, pltpu.roll(yh[2 * cb], HEAD_DIM, 1), yh[2 * cb + 1])
                for cb in range(C_HEADS // 2)]
        y_chunks.append(jnp.concatenate(cols, axis=1))
    y = jnp.concatenate(y_chunks, axis=0) if n_chunks > 1 else y_chunks[0]

    inv = 1.0 / HEAD_DIM
    mean = _dot_sel_rhs(y, bd) * inv
    d = y - mean
    var = _dot_sel_rhs(d * d, bd) * inv
    yn = d * lax.rsqrt(var + C_GN_EPS) * gng_ref[...] + gnb_ref[...]
    out = (yn + bonus * v) * gate
    o_ref[0] = out.astype(o_ref.dtype)


def _rwkv(cc, v_first, p, consts):
    b, s, _ = cc.shape
    n_chunks = RWKV_CHUNKS_PER_STEP
    tb = n_chunks * RWKV_CHUNK
    has_vres = v_first is not None
    blk_cc = pl.BlockSpec((1, tb, C_COLS), lambda bi, j: (bi, j, 0))
    blk_cw = pl.BlockSpec((1, tb, C_WIDTH), lambda bi, j: (bi, j, 0))
    vec = lambda n: _resident((1, n))
    params = [p["mu"], p["w0"], p["w2"], p["a0"], p["a2"], p["g2"], p["k_k"], p["k_a"], p["r_k"],
              p["gn_g"], p["gn_b"]]
    specs = [vec(C_COLS), vec(C_WIDTH), _resident((LANES, C_WIDTH)), vec(C_WIDTH),
             _resident((LANES, C_WIDTH)), _resident((LANES, C_WIDTH)), vec(C_WIDTH), vec(C_WIDTH),
             vec(C_WIDTH), vec(C_WIDTH), vec(C_WIDTH)]
    if has_vres:
        params += [p["v0"], p["v1"], p["v2"]]
        specs += [vec(C_WIDTH), _resident((C_WIDTH, LANES)), _resident((LANES, C_WIDTH))]
    params += [consts["bd"], consts["tri"]]
    specs += [_resident((C_WIDTH, C_WIDTH)), _resident((RWKV_CHUNK, RWKV_CHUNK))]
    o_shape = jax.ShapeDtypeStruct((b, s, C_WIDTH), BF16)
    if has_vres:
        args = [cc, v_first] + params
        in_specs = [blk_cc, blk_cw] + specs
        out_shape, out_specs = o_shape, blk_cw
    else:
        args = [cc] + params
        in_specs = [blk_cc] + specs
        out_shape = (o_shape, jax.ShapeDtypeStruct((b, s, C_WIDTH), F32))
        out_specs = (blk_cw, blk_cw)
    return pl.pallas_call(
        functools.partial(_rwkv_kernel, n_chunks=n_chunks, has_vres=has_vres),
        out_shape=out_shape,
        grid=(b, s // tb),
        in_specs=in_specs,
        out_specs=out_specs,
        scratch_shapes=[pltpu.VMEM((1, C_COLS), F32),
                        pltpu.VMEM((C_HEADS, 2 * RWKV_CHUNK, LANES), F32)],
        compiler_params=_cparams(("parallel", "arbitrary")),
        name="rwkv_vres" if has_vres else "rwkv_first",
    )(*args)


def _out_proj_ffn_kernel(x_ref, oa_ref, ob_ref, oc_ref, w_ref, g1_ref, b1_ref,
                         wg_ref, wu_ref, wd_ref, g2_ref, b2_ref, o_ref, *, alpha):
    mix = _dot(jnp.concatenate([oa_ref[...], ob_ref[...], oc_ref[...]], axis=1), w_ref[...])
    h = _layer_norm(alpha * x_ref[...] + mix, g1_ref[...], b1_ref[...])
    o_ref[...] = _ffn_ln_rows(h, wg_ref, wu_ref, wd_ref, g2_ref[...], b2_ref[...], alpha)


def _out_proj_ffn(x, oa, ob, oc, w_out, g1, b1, ffn_w, g2, b2, alpha, layer):
    t, d = x.shape
    d_ff = ffn_w[0].shape[2]
    tm = min(ROW_TILE, t)
    row = lambda i: (i, 0)
    return pl.pallas_call(
        functools.partial(_out_proj_ffn_kernel, alpha=alpha),
        out_shape=jax.ShapeDtypeStruct((t, d), F32),
        grid=(t // tm,),
        in_specs=[pl.BlockSpec((tm, d), row), pl.BlockSpec((tm, A_WIDTH), row),
                  pl.BlockSpec((tm, B_WIDTH), row), pl.BlockSpec((tm, C_WIDTH), row),
                  _resident_layer(w_out.shape[1:], layer), _resident((1, d)), _resident((1, d))]
        + _ffn_specs(d, d_ff, layer),
        out_specs=pl.BlockSpec((tm, d), row),
        compiler_params=_cparams(("parallel",)),
        name="out_proj_ffn",
    )(x, oa, ob, oc, w_out, g1, b1, *ffn_w, g2, b2)


def _pad_rows(w, total, offset):
    return jnp.zeros((total, w.shape[1]), w.dtype).at[offset:offset + w.shape[0]].set(w)


def kernel(x, ffn_a_gate, ffn_a_up, ffn_a_down, ffn_b_gate, ffn_b_up, ffn_b_down, ln_g, ln_b, w_in, w_out, a_lam_q1, a_lam_k1, a_lam_q2, a_lam_k2, a_norm_g, b_norm_g, c_mu, c_w0, c_w2, c_a0, c_a2, c_g2, c_k_k, c_k_a, c_r_k, c_gn_g, c_gn_b, c_v0, c_v1, c_v2):
    bsz, s_len, d_model = x.shape
    depth = w_in.shape[0]
    alpha = (2 * depth) ** 0.25
    t = bsz * s_len

    tabs = _rope_tables(s_len, A_QK_DIM) + _rope_tables(s_len, HEAD_DIM)
    dil_bias = jnp.asarray(_dilated_log2_counts(min(ATT_BLOCK, s_len)))
    head_of = np.arange(C_WIDTH) // HEAD_DIM
    consts = {
        "bd": jnp.asarray(head_of[:, None] == head_of[None, :], BF16),
        "tri": jnp.asarray(np.tril(np.ones((RWKV_CHUNK, RWKV_CHUNK), np.float32)), BF16),
    }
    row = lambda vct: vct.reshape(1, -1)
    att_blk = min(ATT_BLOCK, s_len)
    gain_t = lambda vct: jnp.broadcast_to(jnp.tile(vct, LANES // HEAD_DIM)[:, None], (LANES, att_blk))

    ffn_a = (ffn_a_gate.astype(BF16), ffn_a_up.astype(BF16), ffn_a_down.astype(BF16))
    ffn_b = (ffn_b_gate.astype(BF16), ffn_b_up.astype(BF16), ffn_b_down.astype(BF16))
    w_in_b = w_in.astype(BF16)
    v_cols = np.r_[2 * A_WIDTH:3 * A_WIDTH, 3 * A_WIDTH + 2 * B_WIDTH:3 * A_WIDTH + 3 * B_WIDTH]
    w_vt_b = jnp.swapaxes(w_in_b[:, :, v_cols], 1, 2)
    w_out_b = w_out.astype(BF16)

    h = x.reshape(t, d_model)
    v_first = None
    for l in range(depth):
        h, aq, ak, avt, bq, bk, bvt, cc = _ffn_in_proj(h, ffn_a, row(ln_g[l, 0]), row(ln_b[l, 0]),
                                                       w_in_b, w_vt_b, tabs, s_len, alpha, l)
        shp = (bsz, s_len, A_WIDTH)

        lam_init = 0.8 - 0.6 * math.exp(-0.3 * l)
        lam_p = jnp.zeros((4, LANES), F32).at[:, :A_QK_DIM].set(
            jnp.stack([a_lam_q1[l], a_lam_k1[l], a_lam_q2[l], a_lam_k2[l]]))
        o_a = _diff_attn(aq.reshape(shp), ak.reshape(shp), avt, lam_p, gain_t(a_norm_g[l]), lam_init)
        o_b = _dil_attn(bq.reshape(shp), bk.reshape(shp), bvt, dil_bias, gain_t(b_norm_g[l]))

        p = {"mu": row(c_mu[l]), "w0": row(c_w0[l]), "a0": row(c_a0[l]),
             "w2": _pad_rows(c_w2[l], LANES, 0), "a2": _pad_rows(c_a2[l], LANES, C_W_RANK),
             "g2": c_g2[l], "k_k": row(c_k_k[l]), "k_a": row(c_k_a[l]), "r_k": row(c_r_k[l]),
             "gn_g": row(c_gn_g[l]), "gn_b": row(c_gn_b[l])}
        cc3 = cc.reshape(bsz, s_len, C_COLS)
        if l == 0:
            o_c, v_first = _rwkv(cc3, None, p, consts)
        else:
            p["v0"] = row(c_v0[l - 1])
            p["v1"] = jnp.zeros((C_WIDTH, LANES), F32).at[:, :C_V_RANK].set(c_v1[l - 1])
            p["v2"] = _pad_rows(c_v2[l - 1], LANES, 0)
            o_c = _rwkv(cc3, v_first, p, consts)

        h = _out_proj_ffn(h, o_a.reshape(t, A_WIDTH), o_b.reshape(t, B_WIDTH), o_c.reshape(t, C_WIDTH),
                          w_out_b, row(ln_g[l, 1]), row(ln_b[l, 1]),
                          ffn_b, row(ln_g[l, 2]), row(ln_b[l, 2]), alpha, l)
    return h.reshape(bsz, s_len, d_model)
```

```python
import functools
import math

import numpy as np
import jax
import jax.numpy as jnp
from jax import lax
from jax.experimental import pallas as pl
from jax.experimental.pallas import tpu as pltpu

F32 = jnp.float32
BF16 = jnp.bfloat16

HEAD_DIM = 64
A_HEADS = 6
B_HEADS = 6
C_HEADS = 4
A_QK_DIM = 32
A_WIDTH = A_HEADS * HEAD_DIM
B_WIDTH = B_HEADS * HEAD_DIM
C_WIDTH = C_HEADS * HEAD_DIM
C_W_RANK = 64
C_A_RANK = 64
C_V_RANK = 32
C_G_RANK = 128
C_COLS = 3 * C_WIDTH + C_W_RANK + C_A_RANK + C_G_RANK
DILATED_PATTERNS = ((128, 1), (512, 4), (2048, 16))
ROPE_THETA = 10000.0
LN_EPS = 1e-5
RMS_EPS = 1e-5
C_GN_EPS = 64e-5
NEG_INF = -1e30
DECAY_RATE = math.exp(-0.5)
LOG2_E = math.log2(math.e)

LANES = 128
SUBLANES = 8
BF16_SUBLANES = 16
VMEM_LIMIT_BYTES = 56 * 1024 * 1024

ROW_TILE = 512
FF_CHUNK = 256
ATT_BLOCK = 512
RWKV_CHUNK = 64
RWKV_CHUNKS_PER_STEP = 8
SOLVE_BLOCK = 16

_NN = (((1,), (0,)), ((), ()))
_NT = (((1,), (1,)), ((), ()))


def _dot(a, b, dims=_NN):
    return lax.dot_general(a, b, dims, preferred_element_type=F32)


def _split2(x):
    hi = x.astype(BF16)
    lo = (x - hi.astype(F32)).astype(BF16)
    return hi, lo


def _split3(x):
    hi = x.astype(BF16)
    r = x - hi.astype(F32)
    mid = r.astype(BF16)
    lo = (r - mid.astype(F32)).astype(BF16)
    return hi, mid, lo


def _dot2r(a, b, dims=_NN):
    a_hi = a.astype(BF16)
    b_hi, b_lo = _split2(b)
    return _dot(a_hi, b_lo, dims) + _dot(a_hi, b_hi, dims)


def _dot1(a, b, dims=_NN):
    return _dot(a.astype(BF16), b.astype(BF16), dims)


def _dot_sel_lhs(sel, x, dims=_NN):
    h, m, l = _split3(x)
    return _dot(sel, l, dims) + _dot(sel, m, dims) + _dot(sel, h, dims)


def _dot_sel_rhs(x, sel):
    h, l = _split2(x)
    return _dot(l, sel) + _dot(h, sel)


def _layer_norm(y, g, b):
    mu = jnp.mean(y, axis=-1, keepdims=True)
    d = y - mu
    var = jnp.mean(d * d, axis=-1, keepdims=True)
    return d * lax.rsqrt(var + LN_EPS) * g + b


def _cparams(sem):
    return pltpu.CompilerParams(dimension_semantics=sem, vmem_limit_bytes=VMEM_LIMIT_BYTES)


def _resident(shape):
    nd = len(shape)
    return pl.BlockSpec(shape, lambda *_: (0,) * nd, pipeline_mode=pl.Buffered(1))


def _resident_layer(shape, layer):
    nd = len(shape)
    return pl.BlockSpec((None,) + tuple(shape), lambda *_: (layer,) + (0,) * nd,
                        pipeline_mode=pl.Buffered(1))


def _ffn_ln_rows(x, wg_ref, wu_ref, wd_ref, g, b, alpha):
    d_ff = wg_ref.shape[1]
    xb = x.astype(BF16)
    acc = jnp.zeros(x.shape, F32)
    for c0 in range(0, d_ff, FF_CHUNK):
        gate = _dot(xb, wg_ref[:, c0:c0 + FF_CHUNK])
        up = _dot(xb, wu_ref[:, c0:c0 + FF_CHUNK])
        h = (gate * jax.nn.sigmoid(gate)) * up
        acc = acc + _dot(h.astype(BF16), wd_ref[c0:c0 + FF_CHUNK, :])
    return _layer_norm(alpha * x + 0.5 * acc, g, b)


def _ffn_specs(d, d_ff, layer):
    return [_resident_layer((d, d_ff), layer), _resident_layer((d, d_ff), layer),
            _resident_layer((d_ff, d), layer), _resident((1, d)), _resident((1, d))]


def _rope_lanes(y, cos, sin_signed, group):
    half = group // 2
    lane = lax.broadcasted_iota(jnp.int32, (1, LANES), 1)
    first = (lane % group) < half
    swapped = jnp.where(first, pltpu.roll(y, LANES - half, 1), pltpu.roll(y, half, 1))
    return y * cos + swapped * sin_signed


def _in_proj_rows(x, w_ref, wvt_ref, ca_ref, sa_ref, cb_ref, sb_ref,
                  aq_ref, ak_ref, avt_ref, bq_ref, bk_ref, bvt_ref, cc_ref):
    xb = x.astype(BF16)
    ca, sa, cb, sb = ca_ref[...], sa_ref[...], cb_ref[...], sb_ref[...]
    a_scale = A_QK_DIM ** -0.5 * LOG2_E
    b_scale = HEAD_DIM ** -0.5 * LOG2_E
    b_col = 3 * A_WIDTH
    plan = ((aq_ref, 0, (ca, sa, A_QK_DIM), a_scale),
            (ak_ref, A_WIDTH, (ca, sa, A_QK_DIM), None),
            (bq_ref, b_col, (cb, sb, HEAD_DIM), b_scale),
            (bk_ref, b_col + B_WIDTH, (cb, sb, HEAD_DIM), None))
    segs = [(ref, s0, col + s0, rope, scale) for ref, col, rope, scale in plan
            for s0 in range(0, A_WIDTH, LANES)]
    wide = 2 * LANES
    for i in range(0, len(segs), wide // LANES):
        c0 = segs[i][2]
        y = _dot(xb, w_ref[:, c0:c0 + wide])
        for j in range(wide // LANES):
            ref, s0, _, rope, scale = segs[i + j]
            z = _rope_lanes(y[:, j * LANES:(j + 1) * LANES], *rope)
            if scale is not None:
                z = z * scale
            ref[:, s0:s0 + LANES] = z.astype(ref.dtype)
    col = b_col + 3 * B_WIDTH
    for s0 in range(0, C_COLS, wide):
        cc_ref[:, s0:s0 + wide] = _dot(xb, w_ref[:, col + s0:col + s0 + wide])
    avt_ref[...] = _dot(wvt_ref[0:A_WIDTH, :], xb, _NT).astype(avt_ref.dtype)
    bvt_ref[...] = _dot(wvt_ref[A_WIDTH:A_WIDTH + B_WIDTH, :], xb, _NT).astype(bvt_ref.dtype)


def _ffn_in_proj_kernel(x_ref, wg_ref, wu_ref, wd_ref, g_ref, b_ref, w_ref, wvt_ref,
                        ca_ref, sa_ref, cb_ref, sb_ref, h_ref, *out_refs, alpha):
    h = _ffn_ln_rows(x_ref[...], wg_ref, wu_ref, wd_ref, g_ref[...], b_ref[...], alpha)
    h_ref[...] = h
    _in_proj_rows(h, w_ref, wvt_ref, ca_ref, sa_ref, cb_ref, sb_ref, *out_refs)


def _ffn_in_proj(x, ffn_w, g, b, w_in, w_vt, tabs, s_len, alpha, layer):
    t, d = x.shape
    d_ff = ffn_w[0].shape[2]
    tm = min(ROW_TILE, s_len)
    spb = s_len // tm
    row = lambda i: (i, 0)
    tab = pl.BlockSpec((tm, LANES), lambda i: (i % spb, 0))
    qk = jax.ShapeDtypeStruct((t, A_WIDTH), BF16)
    vt = jax.ShapeDtypeStruct((A_WIDTH, t), BF16)
    qk_spec = pl.BlockSpec((tm, A_WIDTH), row)
    vt_spec = pl.BlockSpec((A_WIDTH, tm), lambda i: (0, i))
    return pl.pallas_call(
        functools.partial(_ffn_in_proj_kernel, alpha=alpha),
        out_shape=(jax.ShapeDtypeStruct((t, d), F32), qk, qk, vt, qk, qk, vt,
                   jax.ShapeDtypeStruct((t, C_COLS), F32)),
        grid=(t // tm,),
        in_specs=[pl.BlockSpec((tm, d), row)] + _ffn_specs(d, d_ff, layer)
        + [_resident_layer(w_in.shape[1:], layer), _resident_layer(w_vt.shape[1:], layer), tab, tab, tab, tab],
        out_specs=(pl.BlockSpec((tm, d), row), qk_spec, qk_spec, vt_spec, qk_spec, qk_spec, vt_spec,
                   pl.BlockSpec((tm, C_COLS), row)),
        compiler_params=_cparams(("parallel",)),
        name="ffn_in_proj",
    )(x, *ffn_w, g, b, w_in, w_vt, *tabs)


def _rope_tables(s_len, group):
    half = group // 2
    inv = ROPE_THETA ** (-jnp.arange(0, group, 2, dtype=F32) / group)
    ang = jnp.arange(s_len, dtype=F32)[:, None] * inv[None, :]
    lane = np.arange(LANES)
    idx = lane % half
    sign = np.where((lane % group) < half, -1.0, 1.0).astype(np.float32)
    cos = jnp.cos(ang)[:, idx]
    sin = jnp.sin(ang)[:, idx] * sign[None, :]
    return cos, sin


V_ROWS = HEAD_DIM + BF16_SUBLANES
DIFF_GROUPS = 4
DIL_GROUPS = 2


def _attn_online(q_stack_ref, k_ref, vt_ref, acc_ref, qi, blk, n_comb, n_groups, off_bias, diag_bias):
    gs = blk // n_groups
    per_head = n_comb // 2
    gcols = n_comb * gs
    hcols = per_head * gs

    def cols(g):
        return slice(g * gcols, (g + 1) * gcols)

    def tiled(bias):
        return jnp.concatenate([bias] * n_comb, axis=1)

    vx_cache = {}

    def vx_at(h, kb, size):
        if (h, kb, size) not in vx_cache:
            v = vt_ref[h * HEAD_DIM:(h + 1) * HEAD_DIM, kb * blk:kb * blk + size]
            vx_cache[h, kb, size] = jnp.concatenate(
                [v, jnp.ones((V_ROWS - HEAD_DIM, size), v.dtype)], axis=0)
        return vx_cache[h, kb, size]

    def keys_of(g, kb):
        return blk if kb < qi else (g + 1) * gs

    def scores(g, kb):
        s = _dot(k_ref[0, kb * blk:kb * blk + keys_of(g, kb), :], q_stack_ref[:, cols(g)])
        if kb == qi:
            bias = diag_bias(g)
        else:
            bias = None if off_bias is None else off_bias(kb)[:, g * gs:(g + 1) * gs]
        return s if bias is None else s + tiled(bias)

    m = [None] * n_groups
    acc = [[None, None] for _ in range(n_groups)]
    s_next = [scores(g, 0) for g in range(n_groups)]
    for kb in range(qi + 1):
        s_cur = s_next
        if kb < qi:
            s_next = [scores(g, kb + 1) for g in range(n_groups)]
        for g in range(n_groups):
            nk = keys_of(g, kb)
            s = s_cur[g]
            m_blk = jnp.max(s, axis=0, keepdims=True)
            m_new = m_blk if m[g] is None else jnp.maximum(m[g], m_blk)
            p = jnp.exp2(s - m_new).astype(BF16)
            for h in range(2):
                hc = slice(h * hcols, (h + 1) * hcols)
                d = _dot(vx_at(h, kb, nk), p[:, hc])
                if acc[g][h] is None:
                    acc[g][h] = d
                else:
                    acc[g][h] = acc[g][h] * jnp.exp2(m[g][:, hc] - m_new[:, hc]) + d
            m[g] = m_new
    for h in range(2):
        acc_ref[h] = jnp.concatenate([acc[g][h] for g in range(n_groups)], axis=1)


def _for_each_query_block(qi, n_q, fn):
    for n in range(n_q):
        @pl.when(qi == n)
        def _(n=n):
            fn(n)


def _normalized(acc_ref, c, n_comb, n_groups):
    per_head = n_comb // 2
    h, j = divmod(c, per_head)
    blk = acc_ref.shape[2] // per_head
    gs = blk // n_groups
    acc = acc_ref[h]
    acc = jnp.concatenate([acc[:, (g * per_head + j) * gs:(g * per_head + j + 1) * gs]
                           for g in range(n_groups)], axis=1)
    return acc[0:HEAD_DIM] / acc[HEAD_DIM:HEAD_DIM + 1]


def _store_q_stack(q_stack_ref, q, n_comb, n_groups):
    gs = q.shape[0] // n_groups
    q_t = q.astype(F32).T.astype(q.dtype)
    zero = jnp.zeros_like(q_t)
    chan = lax.broadcasted_iota(jnp.int32, (LANES, 1), 0) // (LANES // n_comb)
    for c in range(n_comb):
        masked = jnp.where(chan == c, q_t, zero)
        for g in range(n_groups):
            c0 = (g * n_comb + c) * gs
            q_stack_ref[:, c0:c0 + gs] = masked[:, g * gs:(g + 1) * gs]


def _head_rms_t(o_t, g_t):
    sq = o_t * o_t
    ms = [jnp.sum(sq[h * HEAD_DIM:(h + 1) * HEAD_DIM], axis=0, keepdims=True) * (1.0 / HEAD_DIM)
          for h in range(2)]
    scale = jnp.concatenate([jnp.broadcast_to(lax.rsqrt(m + RMS_EPS), (HEAD_DIM, o_t.shape[1])) for m in ms],
                            axis=0)
    return o_t * scale * g_t


def _diff_attn_kernel(lam_ref, gt_ref, q_ref, k_ref, vt_ref, o_ref, q4_ref, acc_ref, *, blk, lam_init):
    qi = pl.program_id(2)
    n_groups = DIFF_GROUPS
    gs = blk // n_groups
    _store_q_stack(q4_ref, q_ref[0], 4, n_groups)

    def causal(g):
        key = lax.broadcasted_iota(jnp.int32, ((g + 1) * gs, gs), 0)
        qry = lax.broadcasted_iota(jnp.int32, ((g + 1) * gs, gs), 1) + g * gs
        return jnp.where(key <= qry, 0.0, NEG_INF)

    _for_each_query_block(qi, k_ref.shape[1] // blk, lambda n: _attn_online(
        q4_ref, k_ref, vt_ref, acc_ref, n, blk, 4, n_groups, None, causal))

    lp = lam_ref[...]
    lam = (jnp.exp(jnp.sum(lp[0:1] * lp[1:2], axis=1, keepdims=True))
           - jnp.exp(jnp.sum(lp[2:3] * lp[3:4], axis=1, keepdims=True)) + lam_init)
    o_t = jnp.concatenate([_normalized(acc_ref, 2 * h, 4, n_groups)
                           - lam * _normalized(acc_ref, 2 * h + 1, 4, n_groups)
                           for h in range(2)], axis=0)
    o_t = _head_rms_t(o_t, gt_ref[...]) * (1.0 - lam_init)
    o_ref[0] = o_t.T.astype(o_ref.dtype)


def _attn_call(kernel_fn, name, n_comb, consts, const_specs, q, k, vt, blk):
    b, s, w = q.shape
    pairs = w // LANES
    return pl.pallas_call(
        kernel_fn,
        out_shape=jax.ShapeDtypeStruct((b, s, w), BF16),
        grid=(b, pairs, s // blk),
        in_specs=const_specs + [
            pl.BlockSpec((1, blk, LANES), lambda bi, p, i: (bi, i, p)),
            pl.BlockSpec((1, s, LANES), lambda bi, p, i: (bi, 0, p)),
            pl.BlockSpec((LANES, s), lambda bi, p, i: (p, bi)),
        ],
        out_specs=pl.BlockSpec((1, blk, LANES), lambda bi, p, i: (bi, i, p)),
        scratch_shapes=[pltpu.VMEM((LANES, n_comb * blk), BF16),
                        pltpu.VMEM((2, V_ROWS, n_comb // 2 * blk), F32)],
        compiler_params=_cparams(("parallel", "parallel", "arbitrary")),
        name=name,
    )(*consts, q, k, vt)


def _diff_attn(q, k, vt, lam_p, g_t, lam_init):
    blk = g_t.shape[1]
    return _attn_call(functools.partial(_diff_attn_kernel, blk=blk, lam_init=lam_init), "diff_attn", 4,
                      [lam_p, g_t], [_resident((4, LANES)), _resident(g_t.shape)], q, k, vt, blk)


def _dilated_log2_counts(blk):
    far = max(w for w, _ in DILATED_PATTERNS if w < max(p[0] for p in DILATED_PATTERNS))
    n_off = far // blk + 2
    j = np.arange(blk)[:, None]
    i = np.arange(blk)[None, :]
    cnt = np.zeros((n_off, blk, blk), np.float64)
    for o in range(n_off):
        delta = o * blk + i - j
        for window, dil in DILATED_PATTERNS:
            cnt[o] += ((delta >= 0) & (delta <= window) & (delta % dil == 0))
    return np.where(cnt > 0, np.log2(np.maximum(cnt, 1.0)), NEG_INF).astype(np.float32)


def _dil_attn_kernel(bias_ref, gt_ref, q_ref, k_ref, vt_ref, o_ref, q2_ref, acc_ref, *, blk, n_off):
    qi = pl.program_id(2)
    n_groups = DIL_GROUPS
    gs = blk // n_groups
    _store_q_stack(q2_ref, q_ref[0], 2, n_groups)

    _for_each_query_block(qi, k_ref.shape[1] // blk, lambda n: _attn_online(
        q2_ref, k_ref, vt_ref, acc_ref, n, blk, 2, n_groups,
        lambda kb: bias_ref[min(n - kb, n_off - 1)],
        lambda g: bias_ref[0, 0:(g + 1) * gs, g * gs:(g + 1) * gs]))

    o_t = jnp.concatenate([_normalized(acc_ref, c, 2, n_groups) for c in range(2)], axis=0)
    o_ref[0] = _head_rms_t(o_t, gt_ref[...]).T.astype(o_ref.dtype)


def _dil_attn(q, k, vt, bias, g_t):
    blk = bias.shape[1]
    return _attn_call(functools.partial(_dil_attn_kernel, blk=blk, n_off=bias.shape[0]), "dil_attn", 2,
                      [bias, g_t], [_resident(bias.shape), _resident(g_t.shape)], q, k, vt, blk)


def _solve_unit_lower(a_list, x_list):
    n = len(x_list)
    c = x_list[0].shape[0]
    nb = c // SOLVE_BLOCK
    sub = SOLVE_BLOCK // SUBLANES
    a_split = [_split2(a[:, 0:c]) for a in a_list]
    xs = [[x[SOLVE_BLOCK * i:SOLVE_BLOCK * (i + 1)] for i in range(nb)] for x in x_list]
    zeros = jnp.zeros((SOLVE_BLOCK, LANES), F32)
    for b in range(nb):
        base = b * SOLVE_BLOCK
        rows = [[xs[u][b][SUBLANES * i:SUBLANES * (i + 1)] for i in range(sub)] for u in range(n)]
        for t in range(SOLVE_BLOCK - 1):
            bi = t // SUBLANES
            for u in range(n):
                xt = rows[u][bi][t % SUBLANES:t % SUBLANES + 1, :]
                for i in range(bi, sub):
                    r0 = base + SUBLANES * i
                    a_col = a_list[u][r0:r0 + SUBLANES, base + t:base + t + 1]
                    rows[u][i] = rows[u][i] - a_col * xt
        for u in range(n):
            xs[u][b] = jnp.concatenate(rows[u], axis=0)
        if b + 1 < nb:
            below = slice(base + SOLVE_BLOCK, c)
            for u in range(n):
                a_hi, a_lo = a_split[u]
                x_hi, x_lo = _split2(jnp.concatenate(
                    [zeros] * b + [xs[u][b]] + [zeros] * (nb - 1 - b), axis=0))
                upd = _dot(a_lo[below], x_hi) + _dot(a_hi[below], x_lo) + _dot(a_hi[below], x_hi)
                for j in range(b + 1, nb):
                    xs[u][j] = xs[u][j] - upd[(j - b - 1) * SOLVE_BLOCK:(j - b) * SOLVE_BLOCK]
    return [jnp.concatenate(x, axis=0) for x in xs]


def _to_slab(col_block, half, lower):
    lane = lax.broadcasted_iota(jnp.int32, (1, LANES), 1)
    want_low = lower
    have_low = half == 0
    src = col_block if want_low == have_low else pltpu.roll(col_block, HEAD_DIM, 1)
    keep = (lane < HEAD_DIM) if want_low else (lane >= HEAD_DIM)
    return jnp.where(keep, src, 0.0)


def _rwkv_kernel(*refs, n_chunks, has_vres):
    if has_vres:
        (cc_ref, vf_ref, mu_ref, w0_ref, w2_ref, a0_ref, a2_ref, g2_ref, kk_ref, ka_ref, rk_ref,
         gng_ref, gnb_ref, v0_ref, v1_ref, v2_ref, bd_ref, tri_ref,
         o_ref, carry_ref, hs_ref) = refs
    else:
        (cc_ref, mu_ref, w0_ref, w2_ref, a0_ref, a2_ref, g2_ref, kk_ref, ka_ref, rk_ref,
         gng_ref, gnb_ref, bd_ref, tri_ref,
         o_ref, vf_out_ref, carry_ref, hs_ref) = refs
    cw = C_WIDTH
    cl = RWKV_CHUNK

    @pl.when(pl.program_id(1) == 0)
    def _():
        carry_ref[...] = jnp.zeros(carry_ref.shape, F32)
        hs_ref[...] = jnp.zeros(hs_ref.shape, F32)

    c_raw = cc_ref[0]
    tb = c_raw.shape[0]
    row = lax.broadcasted_iota(jnp.int32, (tb, 1), 0)
    prev = jnp.where(row == 0, carry_ref[...], pltpu.roll(c_raw, 1, 0))
    carry_ref[...] = c_raw[tb - 1:tb, :]
    c = c_raw + (prev - c_raw) * mu_ref[...]

    r = c[:, 0:cw]
    k = c[:, cw:2 * cw]
    v = c[:, 2 * cw:3 * cw]
    xwa = c[:, 3 * cw:3 * cw + LANES]
    xg = c[:, 3 * cw + LANES:3 * cw + 2 * LANES]
    if has_vres:
        mix = jax.nn.sigmoid(v0_ref[...] + _dot2r(_dot2r(v, v1_ref[...]), v2_ref[...]))
        v = v + (vf_ref[0] - v) * mix
    else:
        vf_out_ref[0] = v
    bd = bd_ref[...]
    lw = -DECAY_RATE * jax.nn.sigmoid(w0_ref[...] + _dot1(jnp.tanh(xwa), w2_ref[...]))
    a = jax.nn.sigmoid(a0_ref[...] + _dot1(xwa, a2_ref[...]))
    gate = _dot2r(jax.nn.sigmoid(xg), g2_ref[...])
    kkv = k * kk_ref[...]
    kappa = kkv / jnp.maximum(jnp.sqrt(_dot_sel_rhs(kkv * kkv, bd)), 1e-12)
    k2 = k * (1.0 + (a - 1.0) * ka_ref[...])
    bonus = _dot_sel_rhs(r * k2 * rk_ref[...], bd)
    beta = a * kappa

    lane = lax.broadcasted_iota(jnp.int32, (1, LANES), 1)
    low = lane < HEAD_DIM
    lane2 = lax.broadcasted_iota(jnp.int32, (1, cw), 1)
    rr = lax.broadcasted_iota(jnp.int32, (2 * cl, LANES), 0)
    cidx = lax.broadcasted_iota(jnp.int32, (2 * cl, LANES), 1) % cl
    tri_mask = cidx < jnp.where(rr < cl, rr, rr - cl + 1)
    eye = (lax.broadcasted_iota(jnp.int32, (cl, LANES), 0)
           == lax.broadcasted_iota(jnp.int32, (cl, LANES), 1))
    tri = tri_ref[...]

    zeros = jnp.zeros((cl, LANES), F32)
    units = []
    for ci in range(n_chunks):
        sl = slice(ci * cl, (ci + 1) * cl)
        lwc = lw[sl]
        cum = _dot_sel_lhs(tri, lwc)
        cum_end = cum[cl - 1:cl, :]
        rt = r[sl] * jnp.exp(cum)
        kt = kappa[sl] * jnp.exp(cum - lwc)
        e_inv = jnp.exp(-cum)
        e_end = jnp.exp(cum_end - cum)
        bh = beta[sl] * e_end
        kh = k2[sl] * e_end
        p_end = jnp.exp(cum_end)
        vc = v[sl]
        right = jnp.concatenate([beta[sl] * e_inv, k2[sl] * e_inv], axis=0).astype(BF16)
        left_hi, left_lo = _split2(jnp.concatenate([kt, rt], axis=0))
        for h in range(C_HEADS):
            cb, half = divmod(h, 2)
            cs = slice(cb * LANES, (cb + 1) * LANES)
            hm = ((lane2 // HEAD_DIM) == h).astype(BF16)
            units.append(dict(
                ci=ci, h=h, right=right, left_hi=left_hi * hm, left_lo=left_lo * hm,
                vw=_to_slab(vc[:, cs], half, lower=False),
                kw=_to_slab(kt[:, cs], half, lower=True),
                rw=_to_slab(rt[:, cs], half, lower=True),
                bkw=jnp.concatenate([_to_slab(bh[:, cs], half, True), _to_slab(kh[:, cs], half, True)], axis=0),
                pe=_to_slab(jnp.broadcast_to(p_end[:, cs], (cl, LANES)), half, True)))
    def stage_m(us):
        for u in us:
            u["m"] = jnp.where(tri_mask, _dot(u["left_lo"], u["right"], _NT)
                               + _dot(u["left_hi"], u["right"], _NT), 0.0)

    def stage_rhs(us):
        for u in us:
            u["rhs"] = u["kw"] + _dot1(u["m"][0:cl], jnp.concatenate([zeros, u["vw"]], axis=0))

    def stage_solve(us):
        xs = _solve_unit_lower([u["m"][0:cl] for u in us], [u["rhs"] for u in us])
        for u, x in zip(us, xs):
            u["st"] = jnp.concatenate([-x, u["vw"]], axis=0)

    def stage_qy(us):
        for u in us:
            u["qy"] = u["rw"] + _dot2r(u["m"][cl:2 * cl], u["st"])

    def stage_gh(us):
        for u in us:
            u["gh"] = jnp.where(eye, u["pe"], 0.0) + _dot1(u["bkw"].T[0:cl], u["st"])

    def state_step(ci):
        for u in units[ci * C_HEADS:(ci + 1) * C_HEADS]:
            hs = hs_ref[u["h"]]
            u["y"] = u["qy"] + _dot1(u["qy"], hs)
            hs_ref[u["h"], 0:cl, :] = jnp.where(low, 0.0, u["gh"] + _dot2r(u["gh"], hs))

    n_first = n_chunks // 2
    first, second = units[:n_first * C_HEADS], units[n_first * C_HEADS:]
    for stage in (stage_m, stage_rhs, stage_solve, stage_qy, stage_gh):
        if first:
            stage(first)
    pending = list(range(n_first))
    for stage in (stage_m, stage_rhs, stage_solve, stage_qy, stage_gh):
        stage(second)
        if pending:
            state_step(pending.pop(0))
    for ci in pending + list(range(n_first, n_chunks)):
        state_step(ci)
    y_chunks = []
    for ci in range(n_chunks):
        yh = [u["y"] for u in units if u["ci"] == ci]
        cols = [jnp.where(low, pltpu.roll(yh[2 * cb], HEAD_DIM, 1), yh[2 * cb + 1])
                for cb in range(C_HEADS // 2)]
        y_chunks.append(jnp.concatenate(cols, axis=1))
    y = jnp.concatenate(y_chunks, axis=0) if n_chunks > 1 else y_chunks[0]

    inv = 1.0 / HEAD_DIM
    mean = _dot_sel_rhs(y, bd) * inv
    d = y - mean
    var = _dot_sel_rhs(d * d, bd) * inv
    yn = d * lax.rsqrt(var + C_GN_EPS) * gng_ref[...] + gnb_ref[...]
    out = (yn + bonus * v) * gate
    o_ref[0] = out.astype(o_ref.dtype)


def _rwkv(cc, v_first, p, consts):
    b, s, _ = cc.shape
    n_chunks = RWKV_CHUNKS_PER_STEP
    tb = n_chunks * RWKV_CHUNK
    has_vres = v_first is not None
    blk_cc = pl.BlockSpec((1, tb, C_COLS), lambda bi, j: (bi, j, 0))
    blk_cw = pl.BlockSpec((1, tb, C_WIDTH), lambda bi, j: (bi, j, 0))
    vec = lambda n: _resident((1, n))
    params = [p["mu"], p["w0"], p["w2"], p["a0"], p["a2"], p["g2"], p["k_k"], p["k_a"], p["r_k"],
              p["gn_g"], p["gn_b"]]
    specs = [vec(C_COLS), vec(C_WIDTH), _resident((LANES, C_WIDTH)), vec(C_WIDTH),
             _resident((LANES, C_WIDTH)), _resident((LANES, C_WIDTH)), vec(C_WIDTH), vec(C_WIDTH),
             vec(C_WIDTH), vec(C_WIDTH), vec(C_WIDTH)]
    if has_vres:
        params += [p["v0"], p["v1"], p["v2"]]
        specs += [vec(C_WIDTH), _resident((C_WIDTH, LANES)), _resident((LANES, C_WIDTH))]
    params += [consts["bd"], consts["tri"]]
    specs += [_resident((C_WIDTH, C_WIDTH)), _resident((RWKV_CHUNK, RWKV_CHUNK))]
    o_shape = jax.ShapeDtypeStruct((b, s, C_WIDTH), BF16)
    if has_vres:
        args = [cc, v_first] + params
        in_specs = [blk_cc, blk_cw] + specs
        out_shape, out_specs = o_shape, blk_cw
    else:
        args = [cc] + params
        in_specs = [blk_cc] + specs
        out_shape = (o_shape, jax.ShapeDtypeStruct((b, s, C_WIDTH), F32))
        out_specs = (blk_cw, blk_cw)
    return pl.pallas_call(
        functools.partial(_rwkv_kernel, n_chunks=n_chunks, has_vres=has_vres),
        out_shape=out_shape,
        grid=(b, s // tb),
        in_specs=in_specs,
        out_specs=out_specs,
        scratch_shapes=[pltpu.VMEM((1, C_COLS), F32),
                        pltpu.VMEM((C_HEADS, 2 * RWKV_CHUNK, LANES), F32)],
        compiler_params=_cparams(("parallel", "arbitrary")),
        name="rwkv_vres" if has_vres else "rwkv_first",
    )(*args)


def _out_proj_ffn_kernel(x_ref, oa_ref, ob_ref, oc_ref, w_ref, g1_ref, b1_ref,
                         wg_ref, wu_ref, wd_ref, g2_ref, b2_ref, o_ref, *, alpha):
    mix = _dot(jnp.concatenate([oa_ref[...], ob_ref[...], oc_ref[...]], axis=1), w_ref[...])
    h = _layer_norm(alpha * x_ref[...] + mix, g1_ref[...], b1_ref[...])
    o_ref[...] = _ffn_ln_rows(h, wg_ref, wu_ref, wd_ref, g2_ref[...], b2_ref[...], alpha)


def _out_proj_ffn(x, oa, ob, oc, w_out, g1, b1, ffn_w, g2, b2, alpha, layer):
    t, d = x.shape
    d_ff = ffn_w[0].shape[2]
    tm = min(ROW_TILE, t)
    row = lambda i: (i, 0)
    return pl.pallas_call(
        functools.partial(_out_proj_ffn_kernel, alpha=alpha),
        out_shape=jax.ShapeDtypeStruct((t, d), F32),
        grid=(t // tm,),
        in_specs=[pl.BlockSpec((tm, d), row), pl.BlockSpec((tm, A_WIDTH), row),
                  pl.BlockSpec((tm, B_WIDTH), row), pl.BlockSpec((tm, C_WIDTH), row),
                  _resident_layer(w_out.shape[1:], layer), _resident((1, d)), _resident((1, d))]
        + _ffn_specs(d, d_ff, layer),
        out_specs=pl.BlockSpec((tm, d), row),
        compiler_params=_cparams(("parallel",)),
        name="out_proj_ffn",
    )(x, oa, ob, oc, w_out, g1, b1, *ffn_w, g2, b2)


def _pad_rows(w, total, offset):
    return jnp.zeros((total, w.shape[1]), w.dtype).at[offset:offset + w.shape[0]].set(w)


def kernel(x, ffn_a_gate, ffn_a_up, ffn_a_down, ffn_b_gate, ffn_b_up, ffn_b_down, ln_g, ln_b, w_in, w_out, a_lam_q1, a_lam_k1, a_lam_q2, a_lam_k2, a_norm_g, b_norm_g, c_mu, c_w0, c_w2, c_a0, c_a2, c_g2, c_k_k, c_k_a, c_r_k, c_gn_g, c_gn_b, c_v0, c_v1, c_v2):
    bsz, s_len, d_model = x.shape
    depth = w_in.shape[0]
    alpha = (2 * depth) ** 0.25
    t = bsz * s_len

    tabs = _rope_tables(s_len, A_QK_DIM) + _rope_tables(s_len, HEAD_DIM)
    dil_bias = jnp.asarray(_dilated_log2_counts(min(ATT_BLOCK, s_len)))
    head_of = np.arange(C_WIDTH) // HEAD_DIM
    consts = {
        "bd": jnp.asarray(head_of[:, None] == head_of[None, :], BF16),
        "tri": jnp.asarray(np.tril(np.ones((RWKV_CHUNK, RWKV_CHUNK), np.float32)), BF16),
    }
    row = lambda vct: vct.reshape(1, -1)
    att_blk = min(ATT_BLOCK, s_len)
    gain_t = lambda vct: jnp.broadcast_to(jnp.tile(vct, LANES // HEAD_DIM)[:, None], (LANES, att_blk))

    ffn_a = (ffn_a_gate.astype(BF16), ffn_a_up.astype(BF16), ffn_a_down.astype(BF16))
    ffn_b = (ffn_b_gate.astype(BF16), ffn_b_up.astype(BF16), ffn_b_down.astype(BF16))
    w_in_b = w_in.astype(BF16)
    v_cols = np.r_[2 * A_WIDTH:3 * A_WIDTH, 3 * A_WIDTH + 2 * B_WIDTH:3 * A_WIDTH + 3 * B_WIDTH]
    w_vt_b = jnp.swapaxes(w_in_b[:, :, v_cols], 1, 2)
    w_out_b = w_out.astype(BF16)

    h = x.reshape(t, d_model)
    v_first = None
    for l in range(depth):
        h, aq, ak, avt, bq, bk, bvt, cc = _ffn_in_proj(h, ffn_a, row(ln_g[l, 0]), row(ln_b[l, 0]),
                                                       w_in_b, w_vt_b, tabs, s_len, alpha, l)
        shp = (bsz, s_len, A_WIDTH)

        lam_init = 0.8 - 0.6 * math.exp(-0.3 * l)
        lam_p = jnp.zeros((4, LANES), F32).at[:, :A_QK_DIM].set(
            jnp.stack([a_lam_q1[l], a_lam_k1[l], a_lam_q2[l], a_lam_k2[l]]))
        o_a = _diff_attn(aq.reshape(shp), ak.reshape(shp), avt, lam_p, gain_t(a_norm_g[l]), lam_init)
        o_b = _dil_attn(bq.reshape(shp), bk.reshape(shp), bvt, dil_bias, gain_t(b_norm_g[l]))

        p = {"mu": row(c_mu[l]), "w0": row(c_w0[l]), "a0": row(c_a0[l]),
             "w2": _pad_rows(c_w2[l], LANES, 0), "a2": _pad_rows(c_a2[l], LANES, C_W_RANK),
             "g2": c_g2[l], "k_k": row(c_k_k[l]), "k_a": row(c_k_a[l]), "r_k": row(c_r_k[l]),
             "gn_g": row(c_gn_g[l]), "gn_b": row(c_gn_b[l])}
        cc3 = cc.reshape(bsz, s_len, C_COLS)
        if l == 0:
            o_c, v_first = _rwkv(cc3, None, p, consts)
        else:
            p["v0"] = row(c_v0[l - 1])
            p["v1"] = jnp.zeros((C_WIDTH, LANES), F32).at[:, :C_V_RANK].set(c_v1[l - 1])
            p["v2"] = _pad_rows(c_v2[l - 1], LANES, 0)
            o_c = _rwkv(cc3, v_first, p, consts)

        h = _out_proj_ffn(h, o_a.reshape(t, A_WIDTH), o_b.reshape(t, B_WIDTH), o_c.reshape(t, C_WIDTH),
                          w_out_b, row(ln_g[l, 1]), row(ln_b[l, 1]),
                          ffn_b, row(ln_g[l, 2]), row(ln_b[l, 2]), alpha, l)
    return h.reshape(bsz, s_len, d_model)
```

```python
import functools
import math

import numpy as np
import jax
import jax.numpy as jnp
from jax import lax
from jax.experimental import pallas as pl
from jax.experimental.pallas import tpu as pltpu

F32 = jnp.float32
BF16 = jnp.bfloat16

HEAD_DIM = 64
A_HEADS = 6
B_HEADS = 6
C_HEADS = 4
A_QK_DIM = 32
A_WIDTH = A_HEADS * HEAD_DIM
B_WIDTH = B_HEADS * HEAD_DIM
C_WIDTH = C_HEADS * HEAD_DIM
C_W_RANK = 64
C_A_RANK = 64
C_V_RANK = 32
C_G_RANK = 128
C_COLS = 3 * C_WIDTH + C_W_RANK + C_A_RANK + C_G_RANK
DILATED_PATTERNS = ((128, 1), (512, 4), (2048, 16))
ROPE_THETA = 10000.0
LN_EPS = 1e-5
RMS_EPS = 1e-5
C_GN_EPS = 64e-5
NEG_INF = -1e30
DECAY_RATE = math.exp(-0.5)
LOG2_E = math.log2(math.e)

LANES = 128
SUBLANES = 8
BF16_SUBLANES = 16
VMEM_LIMIT_BYTES = 56 * 1024 * 1024

ROW_TILE = 512
FF_CHUNK = 256
ATT_BLOCK = 512
RWKV_CHUNK = 64
RWKV_CHUNKS_PER_STEP = 8
SOLVE_BLOCK = 16

_NN = (((1,), (0,)), ((), ()))
_NT = (((1,), (1,)), ((), ()))


def _dot(a, b, dims=_NN):
    return lax.dot_general(a, b, dims, preferred_element_type=F32)


def _split2(x):
    hi = x.astype(BF16)
    lo = (x - hi.astype(F32)).astype(BF16)
    return hi, lo


def _split3(x):
    hi = x.astype(BF16)
    r = x - hi.astype(F32)
    mid = r.astype(BF16)
    lo = (r - mid.astype(F32)).astype(BF16)
    return hi, mid, lo


def _dot2r(a, b, dims=_NN):
    a_hi = a.astype(BF16)
    b_hi, b_lo = _split2(b)
    return _dot(a_hi, b_lo, dims) + _dot(a_hi, b_hi, dims)


def _dot1(a, b, dims=_NN):
    return _dot(a.astype(BF16), b.astype(BF16), dims)


def _dot_sel_lhs(sel, x, dims=_NN):
    h, m, l = _split3(x)
    return _dot(sel, l, dims) + _dot(sel, m, dims) + _dot(sel, h, dims)


def _dot_sel_rhs(x, sel):
    h, l = _split2(x)
    return _dot(l, sel) + _dot(h, sel)


def _layer_norm(y, g, b):
    mu = jnp.mean(y, axis=-1, keepdims=True)
    d = y - mu
    var = jnp.mean(d * d, axis=-1, keepdims=True)
    return d * lax.rsqrt(var + LN_EPS) * g + b


def _cparams(sem):
    return pltpu.CompilerParams(dimension_semantics=sem, vmem_limit_bytes=VMEM_LIMIT_BYTES)


def _resident(shape):
    nd = len(shape)
    return pl.BlockSpec(shape, lambda *_: (0,) * nd, pipeline_mode=pl.Buffered(1))


def _resident_layer(shape, layer):
    nd = len(shape)
    return pl.BlockSpec((None,) + tuple(shape), lambda *_: (layer,) + (0,) * nd,
                        pipeline_mode=pl.Buffered(1))


def _ffn_ln_rows(x, wg_ref, wu_ref, wd_ref, g, b, alpha):
    d_ff = wg_ref.shape[1]
    xb = x.astype(BF16)
    acc = jnp.zeros(x.shape, F32)
    for c0 in range(0, d_ff, FF_CHUNK):
        gate = _dot(xb, wg_ref[:, c0:c0 + FF_CHUNK])
        up = _dot(xb, wu_ref[:, c0:c0 + FF_CHUNK])
        h = (gate * jax.nn.sigmoid(gate)) * up
        acc = acc + _dot(h.astype(BF16), wd_ref[c0:c0 + FF_CHUNK, :])
    return _layer_norm(alpha * x + 0.5 * acc, g, b)


def _ffn_specs(d, d_ff, layer):
    return [_resident_layer((d, d_ff), layer), _resident_layer((d, d_ff), layer),
            _resident_layer((d_ff, d), layer), _resident((1, d)), _resident((1, d))]


def _rope_lanes(y, cos, sin_signed, group):
    half = group // 2
    lane = lax.broadcasted_iota(jnp.int32, (1, LANES), 1)
    first = (lane % group) < half
    swapped = jnp.where(first, pltpu.roll(y, LANES - half, 1), pltpu.roll(y, half, 1))
    return y * cos + swapped * sin_signed


def _in_proj_rows(x, w_ref, wvt_ref, ca_ref, sa_ref, cb_ref, sb_ref,
                  aq_ref, ak_ref, avt_ref, bq_ref, bk_ref, bvt_ref, cc_ref):
    xb = x.astype(BF16)
    ca, sa, cb, sb = ca_ref[...], sa_ref[...], cb_ref[...], sb_ref[...]
    a_scale = A_QK_DIM ** -0.5 * LOG2_E
    b_scale = HEAD_DIM ** -0.5 * LOG2_E
    b_col = 3 * A_WIDTH
    plan = ((aq_ref, 0, (ca, sa, A_QK_DIM), a_scale),
            (ak_ref, A_WIDTH, (ca, sa, A_QK_DIM), None),
            (bq_ref, b_col, (cb, sb, HEAD_DIM), b_scale),
            (bk_ref, b_col + B_WIDTH, (cb, sb, HEAD_DIM), None))
    segs = [(ref, s0, col + s0, rope, scale) for ref, col, rope, scale in plan
            for s0 in range(0, A_WIDTH, LANES)]
    wide = 2 * LANES
    for i in range(0, len(segs), wide // LANES):
        c0 = segs[i][2]
        y = _dot(xb, w_ref[:, c0:c0 + wide])
        for j in range(wide // LANES):
            ref, s0, _, rope, scale = segs[i + j]
            z = _rope_lanes(y[:, j * LANES:(j + 1) * LANES], *rope)
            if scale is not None:
                z = z * scale
            ref[:, s0:s0 + LANES] = z.astype(ref.dtype)
    col = b_col + 3 * B_WIDTH
    for s0 in range(0, C_COLS, wide):
        cc_ref[:, s0:s0 + wide] = _dot(xb, w_ref[:, col + s0:col + s0 + wide])
    avt_ref[...] = _dot(wvt_ref[0:A_WIDTH, :], xb, _NT).astype(avt_ref.dtype)
    bvt_ref[...] = _dot(wvt_ref[A_WIDTH:A_WIDTH + B_WIDTH, :], xb, _NT).astype(bvt_ref.dtype)


def _ffn_in_proj_kernel(x_ref, wg_ref, wu_ref, wd_ref, g_ref, b_ref, w_ref, wvt_ref,
                        ca_ref, sa_ref, cb_ref, sb_ref, h_ref, *out_refs, alpha):
    h = _ffn_ln_rows(x_ref[...], wg_ref, wu_ref, wd_ref, g_ref[...], b_ref[...], alpha)
    h_ref[...] = h
    _in_proj_rows(h, w_ref, wvt_ref, ca_ref, sa_ref, cb_ref, sb_ref, *out_refs)


def _ffn_in_proj(x, ffn_w, g, b, w_in, w_vt, tabs, s_len, alpha, layer):
    t, d = x.shape
    d_ff = ffn_w[0].shape[2]
    tm = min(ROW_TILE, s_len)
    spb = s_len // tm
    row = lambda i: (i, 0)
    tab = pl.BlockSpec((tm, LANES), lambda i: (i % spb, 0))
    qk = jax.ShapeDtypeStruct((t, A_WIDTH), BF16)
    vt = jax.ShapeDtypeStruct((A_WIDTH, t), BF16)
    qk_spec = pl.BlockSpec((tm, A_WIDTH), row)
    vt_spec = pl.BlockSpec((A_WIDTH, tm), lambda i: (0, i))
    return pl.pallas_call(
        functools.partial(_ffn_in_proj_kernel, alpha=alpha),
        out_shape=(jax.ShapeDtypeStruct((t, d), F32), qk, qk, vt, qk, qk, vt,
                   jax.ShapeDtypeStruct((t, C_COLS), F32)),
        grid=(t // tm,),
        in_specs=[pl.BlockSpec((tm, d), row)] + _ffn_specs(d, d_ff, layer)
        + [_resident_layer(w_in.shape[1:], layer), _resident_layer(w_vt.shape[1:], layer), tab, tab, tab, tab],
        out_specs=(pl.BlockSpec((tm, d), row), qk_spec, qk_spec, vt_spec, qk_spec, qk_spec, vt_spec,
                   pl.BlockSpec((tm, C_COLS), row)),
        compiler_params=_cparams(("parallel",)),
        name="ffn_in_proj",
    )(x, *ffn_w, g, b, w_in, w_vt, *tabs)


def _rope_tables(s_len, group):
    half = group // 2
    inv = ROPE_THETA ** (-jnp.arange(0, group, 2, dtype=F32) / group)
    ang = jnp.arange(s_len, dtype=F32)[:, None] * inv[None, :]
    lane = np.arange(LANES)
    idx = lane % half
    sign = np.where((lane % group) < half, -1.0, 1.0).astype(np.float32)
    cos = jnp.cos(ang)[:, idx]
    sin = jnp.sin(ang)[:, idx] * sign[None, :]
    return cos, sin


V_ROWS = HEAD_DIM + BF16_SUBLANES
DIFF_GROUPS = 4
DIL_GROUPS = 2


def _attn_two_pass(q_stack_ref, k_ref, vt_ref, s_refs, acc_ref, qi, blk, n_comb, n_groups,
                   off_bias, diag_bias):
    gs = blk // n_groups
    per_head = n_comb // 2
    gcols = n_comb * gs

    def cols(g):
        return slice(g * gcols, (g + 1) * gcols)

    def tiled(bias):
        return jnp.concatenate([bias] * n_comb, axis=1)

    def k_at(kb, size):
        return k_ref[0, kb * blk:kb * blk + size, :]

    vx_cache = {}

    def vx_at(h, kb, size):
        if (h, kb, size) not in vx_cache:
            v = vt_ref[h * HEAD_DIM:(h + 1) * HEAD_DIM, kb * blk:kb * blk + size]
            vx_cache[h, kb, size] = jnp.concatenate(
                [v, jnp.ones((V_ROWS - HEAD_DIM, size), v.dtype)], axis=0)
        return vx_cache[h, kb, size]

    def col_max(m, s):
        part = jnp.max(s.reshape(s.shape[0] // SUBLANES, SUBLANES, s.shape[1]), axis=0)
        return part if m is None else jnp.maximum(m, part)

    mx = [None] * n_groups
    m = [None] * n_groups
    acc = [[None, None] for _ in range(n_groups)]

    def pass_a(g, kb):
        s = _dot(k_at(kb, blk), q_stack_ref[:, cols(g)])
        bias = off_bias(kb)
        if bias is not None:
            s = s + tiled(bias[:, g * gs:(g + 1) * gs])
        s_refs[g][kb] = s
        mx[g] = col_max(mx[g], s)

    def pass_a_diag(g):
        nk = (g + 1) * gs
        s = _dot(k_at(qi, nk), q_stack_ref[:, cols(g)])
        bias = diag_bias(g)
        if bias is not None:
            s = s + tiled(bias)
        s_refs[g][qi, 0:nk, :] = s
        m[g] = jnp.max(col_max(mx[g], s), axis=0, keepdims=True)

    def keys_of(g, kb):
        return blk if kb < qi else (g + 1) * gs

    def probs(g, kb):
        return jnp.exp2(s_refs[g][kb, 0:keys_of(g, kb), :] - m[g]).astype(BF16)

    def pass_b(g, kb, p):
        for h in range(2):
            d = _dot(vx_at(h, kb, keys_of(g, kb)), p[:, h * per_head * gs:(h + 1) * per_head * gs])
            acc[g][h] = d if acc[g][h] is None else acc[g][h] + d

    for kb in range(qi):
        pass_a(0, kb)
    pass_a_diag(0)
    for g in range(1, n_groups + 1):
        p = probs(g - 1, 0)
        for kb in range(qi + 1):
            if g < n_groups and kb < qi:
                pass_a(g, kb)
            elif g < n_groups:
                pass_a_diag(g)
            p_next = probs(g - 1, kb + 1) if kb < qi else None
            pass_b(g - 1, kb, p)
            p = p_next
    for h in range(2):
        acc_ref[h] = jnp.concatenate([acc[g][h] for g in range(n_groups)], axis=1)


def _for_each_query_block(qi, n_q, fn):
    for n in range(n_q):
        @pl.when(qi == n)
        def _(n=n):
            fn(n)


def _normalized(acc_ref, c, n_comb, n_groups):
    per_head = n_comb // 2
    h, j = divmod(c, per_head)
    blk = acc_ref.shape[2] // per_head
    gs = blk // n_groups
    acc = acc_ref[h]
    acc = jnp.concatenate([acc[:, (g * per_head + j) * gs:(g * per_head + j + 1) * gs]
                           for g in range(n_groups)], axis=1)
    return acc[0:HEAD_DIM] / acc[HEAD_DIM:HEAD_DIM + 1]


def _store_q_stack(q_stack_ref, q, n_comb, n_groups):
    gs = q.shape[0] // n_groups
    q_t = q.astype(F32).T.astype(q.dtype)
    zero = jnp.zeros_like(q_t)
    chan = lax.broadcasted_iota(jnp.int32, (LANES, 1), 0) // (LANES // n_comb)
    for c in range(n_comb):
        masked = jnp.where(chan == c, q_t, zero)
        for g in range(n_groups):
            c0 = (g * n_comb + c) * gs
            q_stack_ref[:, c0:c0 + gs] = masked[:, g * gs:(g + 1) * gs]


def _head_rms_t(o_t, g_t):
    sq = o_t * o_t
    ms = [jnp.sum(sq[h * HEAD_DIM:(h + 1) * HEAD_DIM], axis=0, keepdims=True) * (1.0 / HEAD_DIM)
          for h in range(2)]
    scale = jnp.concatenate([jnp.broadcast_to(lax.rsqrt(m + RMS_EPS), (HEAD_DIM, o_t.shape[1])) for m in ms],
                            axis=0)
    return o_t * scale * g_t


def _diff_attn_kernel(lam_ref, gt_ref, q_ref, k_ref, vt_ref, o_ref, q4_ref, acc_ref, *s_refs,
                      blk, lam_init):
    qi = pl.program_id(2)
    n_groups = DIFF_GROUPS
    gs = blk // n_groups
    _store_q_stack(q4_ref, q_ref[0], 4, n_groups)

    def causal(g):
        key = lax.broadcasted_iota(jnp.int32, ((g + 1) * gs, gs), 0)
        qry = lax.broadcasted_iota(jnp.int32, ((g + 1) * gs, gs), 1) + g * gs
        return jnp.where(key <= qry, 0.0, NEG_INF)

    _for_each_query_block(qi, k_ref.shape[1] // blk, lambda n: _attn_two_pass(
        q4_ref, k_ref, vt_ref, s_refs, acc_ref, n, blk, 4, n_groups, lambda kb: None, causal))

    lp = lam_ref[...]
    lam = (jnp.exp(jnp.sum(lp[0:1] * lp[1:2], axis=1, keepdims=True))
           - jnp.exp(jnp.sum(lp[2:3] * lp[3:4], axis=1, keepdims=True)) + lam_init)
    o_t = jnp.concatenate([_normalized(acc_ref, 2 * h, 4, n_groups)
                           - lam * _normalized(acc_ref, 2 * h + 1, 4, n_groups)
                           for h in range(2)], axis=0)
    o_t = _head_rms_t(o_t, gt_ref[...]) * (1.0 - lam_init)
    o_ref[0] = o_t.T.astype(o_ref.dtype)


def _attn_call(kernel_fn, name, n_comb, n_groups, consts, const_specs, q, k, vt, blk):
    b, s, w = q.shape
    pairs = w // LANES
    return pl.pallas_call(
        kernel_fn,
        out_shape=jax.ShapeDtypeStruct((b, s, w), BF16),
        grid=(b, pairs, s // blk),
        in_specs=const_specs + [
            pl.BlockSpec((1, blk, LANES), lambda bi, p, i: (bi, i, p)),
            pl.BlockSpec((1, s, LANES), lambda bi, p, i: (bi, 0, p)),
            pl.BlockSpec((LANES, s), lambda bi, p, i: (p, bi)),
        ],
        out_specs=pl.BlockSpec((1, blk, LANES), lambda bi, p, i: (bi, i, p)),
        scratch_shapes=[pltpu.VMEM((LANES, n_comb * blk), BF16),
                        pltpu.VMEM((2, V_ROWS, n_comb // 2 * blk), F32)]
        + [pltpu.VMEM((s // blk, blk, n_comb * blk // n_groups), F32) for _ in range(n_groups)],
        compiler_params=_cparams(("parallel", "parallel", "arbitrary")),
        name=name,
    )(*consts, q, k, vt)


def _diff_attn(q, k, vt, lam_p, g_t, lam_init):
    blk = g_t.shape[1]
    return _attn_call(functools.partial(_diff_attn_kernel, blk=blk, lam_init=lam_init), "diff_attn", 4, DIFF_GROUPS,
                      [lam_p, g_t], [_resident((4, LANES)), _resident(g_t.shape)], q, k, vt, blk)


def _dilated_log2_counts(blk):
    far = max(w for w, _ in DILATED_PATTERNS if w < max(p[0] for p in DILATED_PATTERNS))
    n_off = far // blk + 2
    j = np.arange(blk)[:, None]
    i = np.arange(blk)[None, :]
    cnt = np.zeros((n_off, blk, blk), np.float64)
    for o in range(n_off):
        delta = o * blk + i - j
        for window, dil in DILATED_PATTERNS:
            cnt[o] += ((delta >= 0) & (delta <= window) & (delta % dil == 0))
    return np.where(cnt > 0, np.log2(np.maximum(cnt, 1.0)), NEG_INF).astype(np.float32)


def _dil_attn_kernel(bias_ref, gt_ref, q_ref, k_ref, vt_ref, o_ref, q2_ref, acc_ref, *s_refs,
                     blk, n_off):
    qi = pl.program_id(2)
    n_groups = DIL_GROUPS
    gs = blk // n_groups
    _store_q_stack(q2_ref, q_ref[0], 2, n_groups)

    _for_each_query_block(qi, k_ref.shape[1] // blk, lambda n: _attn_two_pass(
        q2_ref, k_ref, vt_ref, s_refs, acc_ref, n, blk, 2, n_groups,
        lambda kb: bias_ref[min(n - kb, n_off - 1)],
        lambda g: bias_ref[0, 0:(g + 1) * gs, g * gs:(g + 1) * gs]))

    o_t = jnp.concatenate([_normalized(acc_ref, c, 2, n_groups) for c in range(2)], axis=0)
    o_ref[0] = _head_rms_t(o_t, gt_ref[...]).T.astype(o_ref.dtype)


def _dil_attn(q, k, vt, bias, g_t):
    blk = bias.shape[1]
    return _attn_call(functools.partial(_dil_attn_kernel, blk=blk, n_off=bias.shape[0]), "dil_attn", 2, DIL_GROUPS,
                      [bias, g_t], [_resident(bias.shape), _resident(g_t.shape)], q, k, vt, blk)


def _solve_unit_lower(a_list, x_list):
    n = len(x_list)
    c = x_list[0].shape[0]
    nb = c // SOLVE_BLOCK
    sub = SOLVE_BLOCK // SUBLANES
    a_split = [_split2(a[:, 0:c]) for a in a_list]
    xs = [[x[SOLVE_BLOCK * i:SOLVE_BLOCK * (i + 1)] for i in range(nb)] for x in x_list]
    zeros = jnp.zeros((SOLVE_BLOCK, LANES), F32)
    for b in range(nb):
        base = b * SOLVE_BLOCK
        rows = [[xs[u][b][SUBLANES * i:SUBLANES * (i + 1)] for i in range(sub)] for u in range(n)]
        for t in range(SOLVE_BLOCK - 1):
            bi = t // SUBLANES
            for u in range(n):
                xt = rows[u][bi][t % SUBLANES:t % SUBLANES + 1, :]
                for i in range(bi, sub):
                    r0 = base + SUBLANES * i
                    a_col = a_list[u][r0:r0 + SUBLANES, base + t:base + t + 1]
                    rows[u][i] = rows[u][i] - a_col * xt
        for u in range(n):
            xs[u][b] = jnp.concatenate(rows[u], axis=0)
        if b + 1 < nb:
            below = slice(base + SOLVE_BLOCK, c)
            for u in range(n):
                a_hi, a_lo = a_split[u]
                x_hi, x_lo = _split2(jnp.concatenate(
                    [zeros] * b + [xs[u][b]] + [zeros] * (nb - 1 - b), axis=0))
                upd = _dot(a_lo[below], x_hi) + _dot(a_hi[below], x_lo) + _dot(a_hi[below], x_hi)
                for j in range(b + 1, nb):
                    xs[u][j] = xs[u][j] - upd[(j - b - 1) * SOLVE_BLOCK:(j - b) * SOLVE_BLOCK]
    return [jnp.concatenate(x, axis=0) for x in xs]


def _to_slab(col_block, half, lower):
    lane = lax.broadcasted_iota(jnp.int32, (1, LANES), 1)
    want_low = lower
    have_low = half == 0
    src = col_block if want_low == have_low else pltpu.roll(col_block, HEAD_DIM, 1)
    keep = (lane < HEAD_DIM) if want_low else (lane >= HEAD_DIM)
    return jnp.where(keep, src, 0.0)


def _rwkv_kernel(*refs, n_chunks, has_vres):
    if has_vres:
        (cc_ref, vf_ref, mu_ref, w0_ref, w2_ref, a0_ref, a2_ref, g2_ref, kk_ref, ka_ref, rk_ref,
         gng_ref, gnb_ref, v0_ref, v1_ref, v2_ref, bd_ref, tri_ref,
         o_ref, carry_ref, hs_ref) = refs
    else:
        (cc_ref, mu_ref, w0_ref, w2_ref, a0_ref, a2_ref, g2_ref, kk_ref, ka_ref, rk_ref,
         gng_ref, gnb_ref, bd_ref, tri_ref,
         o_ref, vf_out_ref, carry_ref, hs_ref) = refs
    cw = C_WIDTH
    cl = RWKV_CHUNK

    @pl.when(pl.program_id(1) == 0)
    def _():
        carry_ref[...] = jnp.zeros(carry_ref.shape, F32)
        hs_ref[...] = jnp.zeros(hs_ref.shape, F32)

    c_raw = cc_ref[0]
    tb = c_raw.shape[0]
    row = lax.broadcasted_iota(jnp.int32, (tb, 1), 0)
    prev = jnp.where(row == 0, carry_ref[...], pltpu.roll(c_raw, 1, 0))
    carry_ref[...] = c_raw[tb - 1:tb, :]
    c = c_raw + (prev - c_raw) * mu_ref[...]

    r = c[:, 0:cw]
    k = c[:, cw:2 * cw]
    v = c[:, 2 * cw:3 * cw]
    xwa = c[:, 3 * cw:3 * cw + LANES]
    xg = c[:, 3 * cw + LANES:3 * cw + 2 * LANES]
    if has_vres:
        mix = jax.nn.sigmoid(v0_ref[...] + _dot2r(_dot2r(v, v1_ref[...]), v2_ref[...]))
        v = v + (vf_ref[0] - v) * mix
    else:
        vf_out_ref[0] = v
    bd = bd_ref[...]
    lw = -DECAY_RATE * jax.nn.sigmoid(w0_ref[...] + _dot1(jnp.tanh(xwa), w2_ref[...]))
    a = jax.nn.sigmoid(a0_ref[...] + _dot1(xwa, a2_ref[...]))
    gate = _dot2r(jax.nn.sigmoid(xg), g2_ref[...])
    kkv = k * kk_ref[...]
    kappa = kkv / jnp.maximum(jnp.sqrt(_dot_sel_rhs(kkv * kkv, bd)), 1e-12)
    k2 = k * (1.0 + (a - 1.0) * ka_ref[...])
    bonus = _dot_sel_rhs(r * k2 * rk_ref[...], bd)
    beta = a * kappa

    lane = lax.broadcasted_iota(jnp.int32, (1, LANES), 1)
    low = lane < HEAD_DIM
    lane2 = lax.broadcasted_iota(jnp.int32, (1, cw), 1)
    rr = lax.broadcasted_iota(jnp.int32, (2 * cl, LANES), 0)
    cidx = lax.broadcasted_iota(jnp.int32, (2 * cl, LANES), 1) % cl
    tri_mask = cidx < jnp.where(rr < cl, rr, rr - cl + 1)
    eye = (lax.broadcasted_iota(jnp.int32, (cl, LANES), 0)
           == lax.broadcasted_iota(jnp.int32, (cl, LANES), 1))
    tri = tri_ref[...]

    zeros = jnp.zeros((cl, LANES), F32)
    units = []
    for ci in range(n_chunks):
        sl = slice(ci * cl, (ci + 1) * cl)
        lwc = lw[sl]
        cum = _dot_sel_lhs(tri, lwc)
        cum_end = cum[cl - 1:cl, :]
        rt = r[sl] * jnp.exp(cum)
        kt = kappa[sl] * jnp.exp(cum - lwc)
        e_inv = jnp.exp(-cum)
        e_end = jnp.exp(cum_end - cum)
        bh = beta[sl] * e_end
        kh = k2[sl] * e_end
        p_end = jnp.exp(cum_end)
        vc = v[sl]
        right = jnp.concatenate([beta[sl] * e_inv, k2[sl] * e_inv], axis=0).astype(BF16)
        left_hi, left_lo = _split2(jnp.concatenate([kt, rt], axis=0))
        for h in range(C_HEADS):
            cb, half = divmod(h, 2)
            cs = slice(cb * LANES, (cb + 1) * LANES)
            hm = ((lane2 // HEAD_DIM) == h).astype(BF16)
            units.append(dict(
                ci=ci, h=h, right=right, left_hi=left_hi * hm, left_lo=left_lo * hm,
                vw=_to_slab(vc[:, cs], half, lower=False),
                kw=_to_slab(kt[:, cs], half, lower=True),
                rw=_to_slab(rt[:, cs], half, lower=True),
                bkw=jnp.concatenate([_to_slab(bh[:, cs], half, True), _to_slab(kh[:, cs], half, True)], axis=0),
                pe=_to_slab(jnp.broadcast_to(p_end[:, cs], (cl, LANES)), half, True)))
    def stage_m(us):
        for u in us:
            u["m"] = jnp.where(tri_mask, _dot(u["left_lo"], u["right"], _NT)
                               + _dot(u["left_hi"], u["right"], _NT), 0.0)

    def stage_rhs(us):
        for u in us:
            u["rhs"] = u["kw"] + _dot1(u["m"][0:cl], jnp.concatenate([zeros, u["vw"]], axis=0))

    def stage_solve(us):
        xs = _solve_unit_lower([u["m"][0:cl] for u in us], [u["rhs"] for u in us])
        for u, x in zip(us, xs):
            u["st"] = jnp.concatenate([-x, u["vw"]], axis=0)

    def stage_qy(us):
        for u in us:
            u["qy"] = u["rw"] + _dot2r(u["m"][cl:2 * cl], u["st"])

    def stage_gh(us):
        for u in us:
            u["gh"] = jnp.where(eye, u["pe"], 0.0) + _dot1(u["bkw"].T[0:cl], u["st"])

    def state_step(ci):
        for u in units[ci * C_HEADS:(ci + 1) * C_HEADS]:
            hs = hs_ref[u["h"]]
            u["y"] = u["qy"] + _dot1(u["qy"], hs)
            hs_ref[u["h"], 0:cl, :] = jnp.where(low, 0.0, u["gh"] + _dot2r(u["gh"], hs))

    n_first = n_chunks // 2
    first, second = units[:n_first * C_HEADS], units[n_first * C_HEADS:]
    for stage in (stage_m, stage_rhs, stage_solve, stage_qy, stage_gh):
        if first:
            stage(first)
    pending = list(range(n_first))
    for stage in (stage_m, stage_rhs, stage_solve, stage_qy, stage_gh):
        stage(second)
        if pending:
            state_step(pending.pop(0))
    for ci in pending + list(range(n_first, n_chunks)):
        state_step(ci)
    y_chunks = []
    for ci in range(n_chunks):
        yh = [u["y"] for u in units if u["ci"] == ci]
        cols = [jnp.where(low, pltpu.roll(yh[2 * cb], HEAD_DIM, 1), yh[2 * cb + 1])
                for cb in range(C_HEADS // 2)]
        y_chunks.append(jnp.concatenate(cols, axis=1))
    y = jnp.concatenate(y_chunks, axis=0) if n_chunks > 1 else y_chunks[0]

    inv = 1.0 / HEAD_DIM
    mean = _dot_sel_rhs(y, bd) * inv
    d = y - mean
    var = _dot_sel_rhs(d * d, bd) * inv
    yn = d * lax.rsqrt(var + C_GN_EPS) * gng_ref[...] + gnb_ref[...]
    out = (yn + bonus * v) * gate
    o_ref[0] = out.astype(o_ref.dtype)


def _rwkv(cc, v_first, p, consts):
    b, s, _ = cc.shape
    n_chunks = RWKV_CHUNKS_PER_STEP
    tb = n_chunks * RWKV_CHUNK
    has_vres = v_first is not None
    blk_cc = pl.BlockSpec((1, tb, C_COLS), lambda bi, j: (bi, j, 0))
    blk_cw = pl.BlockSpec((1, tb, C_WIDTH), lambda bi, j: (bi, j, 0))
    vec = lambda n: _resident((1, n))
    params = [p["mu"], p["w0"], p["w2"], p["a0"], p["a2"], p["g2"], p["k_k"], p["k_a"], p["r_k"],
              p["gn_g"], p["gn_b"]]
    specs = [vec(C_COLS), vec(C_WIDTH), _resident((LANES, C_WIDTH)), vec(C_WIDTH),
             _resident((LANES, C_WIDTH)), _resident((LANES, C_WIDTH)), vec(C_WIDTH), vec(C_WIDTH),
             vec(C_WIDTH), vec(C_WIDTH), vec(C_WIDTH)]
    if has_vres:
        params += [p["v0"], p["v1"], p["v2"]]
        specs += [vec(C_WIDTH), _resident((C_WIDTH, LANES)), _resident((LANES, C_WIDTH))]
    params += [consts["bd"], consts["tri"]]
    specs += [_resident((C_WIDTH, C_WIDTH)), _resident((RWKV_CHUNK, RWKV_CHUNK))]
    o_shape = jax.ShapeDtypeStruct((b, s, C_WIDTH), BF16)
    if has_vres:
        args = [cc, v_first] + params
        in_specs = [blk_cc, blk_cw] + specs
        out_shape, out_specs = o_shape, blk_cw
    else:
        args = [cc] + params
        in_specs = [blk_cc] + specs
        out_shape = (o_shape, jax.ShapeDtypeStruct((b, s, C_WIDTH), F32))
        out_specs = (blk_cw, blk_cw)
    return pl.pallas_call(
        functools.partial(_rwkv_kernel, n_chunks=n_chunks, has_vres=has_vres),
        out_shape=out_shape,
        grid=(b, s // tb),
        in_specs=in_specs,
        out_specs=out_specs,
        scratch_shapes=[pltpu.VMEM((1, C_COLS), F32),
                        pltpu.VMEM((C_HEADS, 2 * RWKV_CHUNK, LANES), F32)],
        compiler_params=_cparams(("parallel", "arbitrary")),
        name="rwkv_vres" if has_vres else "rwkv_first",
    )(*args)


def _out_proj_ffn_kernel(x_ref, oa_ref, ob_ref, oc_ref, w_ref, g1_ref, b1_ref,
                         wg_ref, wu_ref, wd_ref, g2_ref, b2_ref, o_ref, *, alpha):
    mix = _dot(jnp.concatenate([oa_ref[...], ob_ref[...], oc_ref[...]], axis=1), w_ref[...])
    h = _layer_norm(alpha * x_ref[...] + mix, g1_ref[...], b1_ref[...])
    o_ref[...] = _ffn_ln_rows(h, wg_ref, wu_ref, wd_ref, g2_ref[...], b2_ref[...], alpha)


def _out_proj_ffn(x, oa, ob, oc, w_out, g1, b1, ffn_w, g2, b2, alpha, layer):
    t, d = x.shape
    d_ff = ffn_w[0].shape[2]
    tm = min(2 * ROW_TILE, t)
    row = lambda i: (i, 0)
    return pl.pallas_call(
        functools.partial(_out_proj_ffn_kernel, alpha=alpha),
        out_shape=jax.ShapeDtypeStruct((t, d), F32),
        grid=(t // tm,),
        in_specs=[pl.BlockSpec((tm, d), row), pl.BlockSpec((tm, A_WIDTH), row),
                  pl.BlockSpec((tm, B_WIDTH), row), pl.BlockSpec((tm, C_WIDTH), row),
                  _resident_layer(w_out.shape[1:], layer), _resident((1, d)), _resident((1, d))]
        + _ffn_specs(d, d_ff, layer),
        out_specs=pl.BlockSpec((tm, d), row),
        compiler_params=_cparams(("parallel",)),
        name="out_proj_ffn",
    )(x, oa, ob, oc, w_out, g1, b1, *ffn_w, g2, b2)


def _pad_rows(w, total, offset):
    return jnp.zeros((total, w.shape[1]), w.dtype).at[offset:offset + w.shape[0]].set(w)


def kernel(x, ffn_a_gate, ffn_a_up, ffn_a_down, ffn_b_gate, ffn_b_up, ffn_b_down, ln_g, ln_b, w_in, w_out, a_lam_q1, a_lam_k1, a_lam_q2, a_lam_k2, a_norm_g, b_norm_g, c_mu, c_w0, c_w2, c_a0, c_a2, c_g2, c_k_k, c_k_a, c_r_k, c_gn_g, c_gn_b, c_v0, c_v1, c_v2):
    bsz, s_len, d_model = x.shape
    depth = w_in.shape[0]
    alpha = (2 * depth) ** 0.25
    t = bsz * s_len

    tabs = _rope_tables(s_len, A_QK_DIM) + _rope_tables(s_len, HEAD_DIM)
    dil_bias = jnp.asarray(_dilated_log2_counts(min(ATT_BLOCK, s_len)))
    head_of = np.arange(C_WIDTH) // HEAD_DIM
    consts = {
        "bd": jnp.asarray(head_of[:, None] == head_of[None, :], BF16),
        "tri": jnp.asarray(np.tril(np.ones((RWKV_CHUNK, RWKV_CHUNK), np.float32)), BF16),
    }
    row = lambda vct: vct.reshape(1, -1)
    att_blk = min(ATT_BLOCK, s_len)
    gain_t = lambda vct: jnp.broadcast_to(jnp.tile(vct, LANES // HEAD_DIM)[:, None], (LANES, att_blk))

    ffn_a = (ffn_a_gate.astype(BF16), ffn_a_up.astype(BF16), ffn_a_down.astype(BF16))
    ffn_b = (ffn_b_gate.astype(BF16), ffn_b_up.astype(BF16), ffn_b_down.astype(BF16))
    w_in_b = w_in.astype(BF16)
    v_cols = np.r_[2 * A_WIDTH:3 * A_WIDTH, 3 * A_WIDTH + 2 * B_WIDTH:3 * A_WIDTH + 3 * B_WIDTH]
    w_vt_b = jnp.swapaxes(w_in_b[:, :, v_cols], 1, 2)
    w_out_b = w_out.astype(BF16)

    h = x.reshape(t, d_model)
    v_first = None
    for l in range(depth):
        h, aq, ak, avt, bq, bk, bvt, cc = _ffn_in_proj(h, ffn_a, row(ln_g[l, 0]), row(ln_b[l, 0]),
                                                       w_in_b, w_vt_b, tabs, s_len, alpha, l)
        shp = (bsz, s_len, A_WIDTH)

        lam_init = 0.8 - 0.6 * math.exp(-0.3 * l)
        lam_p = jnp.zeros((4, LANES), F32).at[:, :A_QK_DIM].set(
            jnp.stack([a_lam_q1[l], a_lam_k1[l], a_lam_q2[l], a_lam_k2[l]]))
        o_a = _diff_attn(aq.reshape(shp), ak.reshape(shp), avt, lam_p, gain_t(a_norm_g[l]), lam_init)
        o_b = _dil_attn(bq.reshape(shp), bk.reshape(shp), bvt, dil_bias, gain_t(b_norm_g[l]))

        p = {"mu": row(c_mu[l]), "w0": row(c_w0[l]), "a0": row(c_a0[l]),
             "w2": _pad_rows(c_w2[l], LANES, 0), "a2": _pad_rows(c_a2[l], LANES, C_W_RANK),
             "g2": c_g2[l], "k_k": row(c_k_k[l]), "k_a": row(c_k_a[l]), "r_k": row(c_r_k[l]),
             "gn_g": row(c_gn_g[l]), "gn_b": row(c_gn_b[l])}
        cc3 = cc.reshape(bsz, s_len, C_COLS)
        if l == 0:
            o_c, v_first = _rwkv(cc3, None, p, consts)
        else:
            p["v0"] = row(c_v0[l - 1])
            p["v1"] = jnp.zeros((C_WIDTH, LANES), F32).at[:, :C_V_RANK].set(c_v1[l - 1])
            p["v2"] = _pad_rows(c_v2[l - 1], LANES, 0)
            o_c = _rwkv(cc3, v_first, p, consts)

        h = _out_proj_ffn(h, o_a.reshape(t, A_WIDTH), o_b.reshape(t, B_WIDTH), o_c.reshape(t, C_WIDTH),
                          w_out_b, row(ln_g[l, 1]), row(ln_b[l, 1]),
                          ffn_b, row(ln_g[l, 2]), row(ln_b[l, 2]), alpha, l)
    return h.reshape(bsz, s_len, d_model)
```
